```python
import math
import jax, jax.numpy as jnp
from jax import lax
import numpy as np

D_MODEL = 2048
BATCH = 4
SEQ = 4096
DEPTH = 1

MIX_WIDTH = D_MODEL
POOL_WIDTH = MIX_WIDTH // 2
LRU_WIDTH = MIX_WIDTH - POOL_WIDTH
POOL_WINDOWS = (2, 4, 8, 16)
N_POOL_GROUPS = len(POOL_WINDOWS)
POOL_GROUP = POOL_WIDTH // N_POOL_GROUPS
LRU_HEADS = 8
LRU_HEAD_DIM = LRU_WIDTH // LRU_HEADS
CONV_WIDTH = 4
LRU_C = 8.0
N_MEM = 256
XATTN_HEADS = 4
XATTN_HEAD_DIM = D_MODEL // XATTN_HEADS
D_FF = 4 * D_MODEL
LN_EPS = 1e-5
DEEPNORM_ALPHA = (2.0 * DEPTH) ** 0.25
DEEPNORM_BETA = (8.0 * DEPTH) ** -0.25

kernel_name = "hymba_pool_rglru_deepnorm_layer"


def layer_norm(x, g, b):
    xf = x.astype(jnp.float32)
    mu = jnp.mean(xf, axis=-1, keepdims=True)
    var = jnp.mean(jnp.square(xf - mu), axis=-1, keepdims=True)
    y = (xf - mu) * lax.rsqrt(var + LN_EPS)
    return (y * g.astype(jnp.float32) + b.astype(jnp.float32)).astype(x.dtype)


def multiscale_pool(u, w_pool, b_pool, pool_scale):
    B, S, _ = u.shape
    uf = u.astype(jnp.float32).reshape(B, S, N_POOL_GROUPS, POOL_GROUP)
    csum = jnp.cumsum(uf, axis=1)
    t = jnp.arange(S)
    means = []
    for g, w in enumerate(POOL_WINDOWS):
        c = csum[:, :, g]
        c_prev = jnp.pad(c, ((0, 0), (w, 0), (0, 0)))[:, :S]
        cnt = jnp.minimum(t + 1, w).astype(jnp.float32)[None, :, None]
        means.append((c - c_prev) / cnt)
    mixed = (jnp.stack(means, axis=2) - uf).astype(u.dtype)
    y = jnp.einsum('bsgc,gcd->bsgd', mixed, w_pool) + b_pool
    return y.reshape(B, S, POOL_WIDTH) * pool_scale


def causal_depthwise_conv(x, w, b):
    S = x.shape[1]
    xp = jnp.pad(x, ((0, 0), (CONV_WIDTH - 1, 0), (0, 0)))
    y = xp[:, 0:S] * w[0]
    for k in range(1, CONV_WIDTH):
        y = y + xp[:, k:k + S] * w[k]
    return y + b


def rg_lru(x, w_a, b_a, w_x, b_x, lam):
    B, S, _ = x.shape
    xh = x.reshape(B, S, LRU_HEADS, LRU_HEAD_DIM)
    r = jax.nn.sigmoid(jnp.einsum('bshi,hij->bshj', xh, w_a) + b_a).reshape(B, S, LRU_WIDTH)
    i = jax.nn.sigmoid(jnp.einsum('bshi,hij->bshj', xh, w_x) + b_x).reshape(B, S, LRU_WIDTH)
    log_a = -LRU_C * r.astype(jnp.float32) * jax.nn.softplus(-lam.astype(jnp.float32))
    a = jnp.exp(log_a)
    mult = jnp.sqrt(-jnp.expm1(2.0 * log_a))
    mult = jnp.where((jnp.arange(S) == 0)[None, :, None], 1.0, mult)
    bterm = mult * (i * x).astype(jnp.float32)

    def combine(lhs, rhs):
        a1, b1 = lhs
        a2, b2 = rhs
        return a1 * a2, a2 * b1 + b2

    _, h = lax.associative_scan(combine, (a, bterm), axis=1)
    return h.astype(x.dtype)


def hybrid_mixer(x, w_in, conv_w, conv_b, w_a, b_a, w_x, b_x, lam,
                 w_pool, b_pool, pool_scale, w_out):
    proj = x @ w_in
    u_pool = proj[..., :POOL_WIDTH]
    u_lru = proj[..., POOL_WIDTH:POOL_WIDTH + LRU_WIDTH]
    u_gate = proj[..., POOL_WIDTH + LRU_WIDTH:]
    y_pool = multiscale_pool(u_pool, w_pool, b_pool, pool_scale)
    h = rg_lru(causal_depthwise_conv(u_lru, conv_w, conv_b), w_a, b_a, w_x, b_x, lam)
    y_lru = h * jax.nn.gelu(u_gate)
    return jnp.concatenate([y_pool, y_lru], axis=-1) @ w_out


def memory_cross_attention(x, mem, w_q, w_k, w_v, w_o):
    B, S, _ = x.shape
    M = mem.shape[1]
    q = (x @ w_q).reshape(B, S, XATTN_HEADS, XATTN_HEAD_DIM)
    k = (mem @ w_k).reshape(B, M, XATTN_HEADS, XATTN_HEAD_DIM)
    v = (mem @ w_v).reshape(B, M, XATTN_HEADS, XATTN_HEAD_DIM)
    s = jnp.einsum('bqhd,bmhd->bhqm', q, k).astype(jnp.float32) * (XATTN_HEAD_DIM ** -0.5)
    p = jax.nn.softmax(s, axis=-1).astype(v.dtype)
    o = jnp.einsum('bhqm,bmhd->bqhd', p, v).reshape(B, S, D_MODEL)
    return o @ w_o


def squared_relu_mlp(x, w1, w2):
    return jnp.square(jax.nn.relu(x @ w1)) @ w2


def setup_inputs(seed: int = 0) -> dict:
    key = jax.random.key(seed)
    ks = jax.random.split(key, 32)
    f32 = jnp.float32

    def nrm(k, shape, scale):
        return jax.random.normal(k, shape, f32) * scale

    L = DEPTH
    u = jax.random.uniform(ks[10], (L, LRU_WIDTH), f32, 0.9, 0.999)
    s = u ** (1.0 / LRU_C)
    lam = jnp.log(s) - jnp.log1p(-s)
    return {
        "x": nrm(ks[0], (BATCH, SEQ, D_MODEL), 1.0),
        "mem": nrm(ks[1], (BATCH, N_MEM, D_MODEL), 1.0),
        "w_in": nrm(ks[2], (L, D_MODEL, POOL_WIDTH + 2 * LRU_WIDTH), D_MODEL ** -0.5),
        "conv_w": nrm(ks[3], (L, CONV_WIDTH, LRU_WIDTH), CONV_WIDTH ** -0.5),
        "conv_b": nrm(ks[4], (L, LRU_WIDTH), 0.01),
        "w_a": nrm(ks[5], (L, LRU_HEADS, LRU_HEAD_DIM, LRU_HEAD_DIM), LRU_HEAD_DIM ** -0.5),
        "b_a": nrm(ks[6], (L, LRU_HEADS, LRU_HEAD_DIM), 0.01),
        "w_x": nrm(ks[7], (L, LRU_HEADS, LRU_HEAD_DIM, LRU_HEAD_DIM), LRU_HEAD_DIM ** -0.5),
        "b_x": nrm(ks[8], (L, LRU_HEADS, LRU_HEAD_DIM), 0.01),
        "lru_lambda": lam,
        "w_pool": nrm(ks[11], (L, N_POOL_GROUPS, POOL_GROUP, POOL_GROUP), POOL_GROUP ** -0.5),
        "b_pool": nrm(ks[12], (L, N_POOL_GROUPS, POOL_GROUP), 0.01),
        "pool_scale": 1.0 + nrm(ks[13], (L, POOL_WIDTH), 0.1),
        "w_out": nrm(ks[14], (L, MIX_WIDTH, D_MODEL), MIX_WIDTH ** -0.5 * DEEPNORM_BETA),
        "ln1_g": 1.0 + nrm(ks[15], (L, D_MODEL), 0.05),
        "ln1_b": nrm(ks[16], (L, D_MODEL), 0.01),
        "w_q": nrm(ks[17], (L, D_MODEL, D_MODEL), D_MODEL ** -0.5),
        "w_k": nrm(ks[18], (L, D_MODEL, D_MODEL), D_MODEL ** -0.5),
        "w_v": nrm(ks[19], (L, D_MODEL, D_MODEL), D_MODEL ** -0.5 * DEEPNORM_BETA),
        "w_o": nrm(ks[20], (L, D_MODEL, D_MODEL), D_MODEL ** -0.5 * DEEPNORM_BETA),
        "ln2_g": 1.0 + nrm(ks[21], (L, D_MODEL), 0.05),
        "ln2_b": nrm(ks[22], (L, D_MODEL), 0.01),
        "w_ff1": nrm(ks[23], (L, D_MODEL, D_FF), D_MODEL ** -0.5 * DEEPNORM_BETA),
        "w_ff2": nrm(ks[24], (L, D_FF, D_MODEL), D_FF ** -0.5 * DEEPNORM_BETA),
        "ln3_g": 1.0 + nrm(ks[25], (L, D_MODEL), 0.05),
        "ln3_b": nrm(ks[26], (L, D_MODEL), 0.01),
    }


def reference(x, mem, w_in, conv_w, conv_b, w_a, b_a, w_x, b_x, lru_lambda,
              w_pool, b_pool, pool_scale, w_out, ln1_g, ln1_b,
              w_q, w_k, w_v, w_o, ln2_g, ln2_b, w_ff1, w_ff2, ln3_g, ln3_b):
    for l in range(DEPTH):
        y = hybrid_mixer(x, w_in[l], conv_w[l], conv_b[l], w_a[l], b_a[l], w_x[l], b_x[l],
                         lru_lambda[l], w_pool[l], b_pool[l], pool_scale[l], w_out[l])
        x = layer_norm(DEEPNORM_ALPHA * x + y, ln1_g[l], ln1_b[l])
        y = memory_cross_attention(x, mem, w_q[l], w_k[l], w_v[l], w_o[l])
        x = layer_norm(DEEPNORM_ALPHA * x + y, ln2_g[l], ln2_b[l])
        y = squared_relu_mlp(x, w_ff1[l], w_ff2[l])
        x = layer_norm(DEEPNORM_ALPHA * x + y, ln3_g[l], ln3_b[l])
    return x
```

```python
import functools

import jax
import jax.numpy as jnp
from jax import lax
from jax.experimental import pallas as pl
from jax.experimental.pallas import tpu as pltpu

D_MODEL = 2048
POOL_WIDTH = 1024
LRU_WIDTH = 1024
POOL_WINDOWS = (2, 4, 8, 16)
POOL_GROUP = 256
LRU_HEADS = 8
LRU_HEAD_DIM = 128
CONV_WIDTH = 4
LRU_C = 8.0
N_MEM = 256
XATTN_HEADS = 4
XATTN_HEAD_DIM = 512
D_FF = 4 * D_MODEL
LN_EPS = 1e-5
DEPTH = 1
DEEPNORM_ALPHA = (2.0 * DEPTH) ** 0.25

POOL_HIST = 16
CONV_HIST = 8

MIXER_TILE = 256
XATTN_TILE = 512
MLP_TILE_M = 512
MLP_TILE_F = 1024
KV_TILE_N = 512
VMEM_LIMIT_BYTES = 56 * 1024 * 1024

_F32 = jnp.float32
_BF16 = jnp.bfloat16


def _layer_norm(v, g, b):
    mu = jnp.mean(v, axis=-1, keepdims=True)
    c = v - mu
    var = jnp.mean(jnp.square(c), axis=-1, keepdims=True)
    return c * lax.rsqrt(var + LN_EPS) * g + b


def _dot(a, b):
    return jnp.dot(a, b, preferred_element_type=_F32)


def _shift_rows(a, d, fill, row):
    n = a.shape[0]
    if d % 8 == 0:
        head = jnp.full((d, a.shape[1]), fill, a.dtype)
        return jnp.concatenate([head, a[: n - d]], axis=0)
    return jnp.where(row < d, fill, pltpu.roll(a, d, 0))


def _mixer_kernel(x_ref, w_in_ref, conv_w_ref, conv_b_ref, w_ax_ref, b_a_ref, b_x_ref, lam_ref,
                  w_pool_ref, b_pool_ref, pool_scale_ref, w_out_ref, g_ref, b_ref,
                  o_ref, pool_hist, conv_hist, h_carry):
    T = x_ref.shape[1]
    s = pl.program_id(1)

    @pl.when(s == 0)
    def _():
        pool_hist[...] = jnp.zeros_like(pool_hist)
        conv_hist[...] = jnp.zeros_like(conv_hist)
        h_carry[...] = jnp.zeros_like(h_carry)

    x = x_ref[0]
    proj = _dot(x.astype(_BF16), w_in_ref[...])
    u_pool = proj[:, :POOL_WIDTH]
    u_lru = proj[:, POOL_WIDTH:POOL_WIDTH + LRU_WIDTH]
    u_gate = proj[:, POOL_WIDTH + LRU_WIDTH:]

    row = lax.broadcasted_iota(jnp.int32, (T, 1), 0)
    t = s * T + row

    ext = jnp.concatenate([pool_hist[...], u_pool], axis=0)
    pool_hist[...] = u_pool[T - POOL_HIST:]
    s2 = ext + pltpu.roll(ext, 1, 0)
    s2r = s2[:, POOL_GROUP:]
    s4 = s2r + pltpu.roll(s2r, 2, 0)
    s4r = s4[:, POOL_GROUP:]
    s8 = s4r + pltpu.roll(s4r, 4, 0)
    s8r = s8[:, POOL_GROUP:]
    s16 = s8r + pltpu.roll(s8r, 8, 0)
    sums = (s2[POOL_HIST:, :POOL_GROUP], s4[POOL_HIST:, :POOL_GROUP],
            s8[POOL_HIST:, :POOL_GROUP], s16[POOL_HIST:])
    y_pool = []
    for g, w in enumerate(POOL_WINDOWS):
        cs = slice(g * POOL_GROUP, (g + 1) * POOL_GROUP)
        inv_cnt = 1.0 / jnp.minimum(t + 1, w).astype(_F32)
        mixed = sums[g] * inv_cnt - u_pool[:, cs]
        yg = _dot(mixed.astype(_BF16), w_pool_ref[g]) + b_pool_ref[:, cs]
        y_pool.append(yg * pool_scale_ref[:, cs])

    ext2 = jnp.concatenate([conv_hist[...], u_lru], axis=0)
    conv_hist[...] = u_lru[T - CONV_HIST:]
    xc = pltpu.roll(ext2, 3, 0)[CONV_HIST:] * conv_w_ref[0:1, :]
    xc = xc + pltpu.roll(ext2, 2, 0)[CONV_HIST:] * conv_w_ref[1:2, :]
    xc = xc + pltpu.roll(ext2, 1, 0)[CONV_HIST:] * conv_w_ref[2:3, :]
    xc = xc + u_lru * conv_w_ref[3:4, :]
    xc = xc + conv_b_ref[...]

    xcb = xc.astype(_BF16)
    r_parts, i_parts = [], []
    for h in range(LRU_HEADS):
        hs = slice(h * LRU_HEAD_DIM, (h + 1) * LRU_HEAD_DIM)
        pre = _dot(xcb[:, hs], w_ax_ref[h])
        r_parts.append(jax.nn.sigmoid(pre[:, :LRU_HEAD_DIM] + b_a_ref[:, hs]))
        i_parts.append(jax.nn.sigmoid(pre[:, LRU_HEAD_DIM:] + b_x_ref[:, hs]))
    r = jnp.concatenate(r_parts, axis=1)
    i = jnp.concatenate(i_parts, axis=1)

    log_a = (-LRU_C * r) * jax.nn.softplus(-lam_ref[...])
    a = jnp.exp(log_a)
    mult = jnp.sqrt(-jnp.tanh(log_a) * (a * a + 1.0))
    mult = jnp.where(t == 0, 1.0, mult)
    bv = mult * (i * xc)

    d = 1
    while d < T:
        a_s = _shift_rows(a, d, 1.0, row)
        b_s = _shift_rows(bv, d, 0.0, row)
        bv = a * b_s + bv
        a = a * a_s
        d *= 2
    hseq = bv + a * h_carry[0:1, :]
    h_carry[...] = jnp.broadcast_to(hseq[T - 1:T, :], h_carry.shape)
    y_lru = hseq * jax.nn.gelu(u_gate)

    z = jnp.concatenate(y_pool + [y_lru], axis=1).astype(_BF16)
    y = _dot(z, w_out_ref[...])
    o_ref[0] = _layer_norm(DEEPNORM_ALPHA * x + y, g_ref[...], b_ref[...])


def _const_spec(shape, single_buffer=False):
    nd = len(shape)
    kwargs = {"pipeline_mode": pl.Buffered(1)} if single_buffer else {}
    return pl.BlockSpec(shape, lambda *_: (0,) * nd, **kwargs)


def _mixer(x, w_in, conv_w, conv_b, w_ax, b_a, b_x, lam, w_pool, b_pool, pool_scale, w_out, g, b):
    B, S, D = x.shape
    T = MIXER_TILE
    tile = pl.BlockSpec((1, T, D), lambda bi, si: (bi, si, 0))
    return pl.pallas_call(
        _mixer_kernel,
        grid=(B, S // T),
        in_specs=[
            tile,
            _const_spec(w_in.shape, True),
            _const_spec(conv_w.shape), _const_spec(conv_b.shape),
            _const_spec(w_ax.shape, True),
            _const_spec(b_a.shape), _const_spec(b_x.shape), _const_spec(lam.shape),
            _const_spec(w_pool.shape, True),
            _const_spec(b_pool.shape), _const_spec(pool_scale.shape),
            _const_spec(w_out.shape, True),
            _const_spec(g.shape), _const_spec(b.shape),
        ],
        out_specs=tile,
        out_shape=jax.ShapeDtypeStruct(x.shape, _F32),
        scratch_shapes=[
            pltpu.VMEM((POOL_HIST, POOL_WIDTH), _F32),
            pltpu.VMEM((CONV_HIST, LRU_WIDTH), _F32),
            pltpu.VMEM((8, LRU_WIDTH), _F32),
        ],
        compiler_params=pltpu.CompilerParams(
            dimension_semantics=("arbitrary", "arbitrary"),
            vmem_limit_bytes=VMEM_LIMIT_BYTES),
        name="mixer",
    )(x, w_in, conv_w, conv_b, w_ax, b_a, b_x, lam, w_pool, b_pool, pool_scale, w_out, g, b)


def _kv_kernel(mem_ref, wk_ref, wv_ref, k_ref, v_ref):
    m = mem_ref[...].astype(_BF16)
    k_ref[...] = _dot(m, wk_ref[...]).astype(_BF16)
    v_ref[...] = _dot(m, wv_ref[...]).astype(_BF16)


def _kv_proj(mem2d, w_k, w_v):
    M, D = mem2d.shape
    tn = KV_TILE_N
    wspec = pl.BlockSpec((D, tn), lambda j: (0, j))
    ospec = pl.BlockSpec((M, tn), lambda j: (0, j))
    return pl.pallas_call(
        _kv_kernel,
        grid=(D // tn,),
        in_specs=[_const_spec(mem2d.shape), wspec, wspec],
        out_specs=[ospec, ospec],
        out_shape=[jax.ShapeDtypeStruct((M, D), _BF16)] * 2,
        compiler_params=pltpu.CompilerParams(
            dimension_semantics=("arbitrary",), vmem_limit_bytes=VMEM_LIMIT_BYTES),
        name="kv_proj",
    )(mem2d, w_k, w_v)


def _xattn_kernel(x_ref, k_ref, v_ref, wq_ref, wo_ref, g_ref, b_ref, o_ref):
    x = x_ref[0]
    q = _dot(x.astype(_BF16), wq_ref[...]).astype(_BF16)
    scale = XATTN_HEAD_DIM ** -0.5
    o_parts = []
    for h in range(XATTN_HEADS):
        hs = slice(h * XATTN_HEAD_DIM, (h + 1) * XATTN_HEAD_DIM)
        sc = lax.dot_general(q[:, hs], k_ref[0, :, hs], (((1,), (1,)), ((), ())),
                             preferred_element_type=_F32) * scale
        e = jnp.exp(sc - jnp.max(sc, axis=-1, keepdims=True))
        p = e / jnp.sum(e, axis=-1, keepdims=True)
        o_parts.append(_dot(p.astype(_BF16), v_ref[0, :, hs]))
    o = jnp.concatenate(o_parts, axis=1).astype(_BF16)
    y = _dot(o, wo_ref[...])
    o_ref[0] = _layer_norm(DEEPNORM_ALPHA * x + y, g_ref[...], b_ref[...])


def _xattn(x, k, v, w_q, w_o, g, b):
    B, S, D = x.shape
    T = XATTN_TILE
    tile = pl.BlockSpec((1, T, D), lambda bi, si: (bi, si, 0))
    kvspec = pl.BlockSpec((1, N_MEM, D), lambda bi, si: (bi, 0, 0))
    return pl.pallas_call(
        _xattn_kernel,
        grid=(B, S // T),
        in_specs=[tile, kvspec, kvspec,
                  _const_spec(w_q.shape, True), _const_spec(w_o.shape, True),
                  _const_spec(g.shape), _const_spec(b.shape)],
        out_specs=tile,
        out_shape=jax.ShapeDtypeStruct(x.shape, _F32),
        compiler_params=pltpu.CompilerParams(
            dimension_semantics=("arbitrary", "arbitrary"),
            vmem_limit_bytes=VMEM_LIMIT_BYTES),
        name="xattn",
    )(x, k, v, w_q, w_o, g, b)


def _mlp_kernel(x_ref, w1_ref, w2_ref, g_ref, b_ref, o_ref, xb_ref, acc_ref):
    j = pl.program_id(1)

    @pl.when(j == 0)
    def _():
        xb_ref[...] = x_ref[...].astype(_BF16)

    h = jnp.square(jnp.maximum(_dot(xb_ref[...], w1_ref[...]), 0.0))
    contrib = _dot(h.astype(_BF16), w2_ref[...])

    @pl.when(j == 0)
    def _():
        acc_ref[...] = contrib

    @pl.when(j > 0)
    def _():
        acc_ref[...] += contrib

    @pl.when(j == pl.num_programs(1) - 1)
    def _():
        o_ref[...] = _layer_norm(DEEPNORM_ALPHA * x_ref[...] + acc_ref[...], g_ref[...], b_ref[...])


def _mlp(x2d, w1, w2, g, b):
    M, D = x2d.shape
    F = w1.shape[1]
    tm, tf = MLP_TILE_M, MLP_TILE_F
    xspec = pl.BlockSpec((tm, D), lambda i, j: (i, 0))
    return pl.pallas_call(
        _mlp_kernel,
        grid=(M // tm, F // tf),
        in_specs=[xspec,
                  pl.BlockSpec((D, tf), lambda i, j: (0, j)),
                  pl.BlockSpec((tf, D), lambda i, j: (j, 0)),
                  _const_spec(g.shape), _const_spec(b.shape)],
        out_specs=xspec,
        out_shape=jax.ShapeDtypeStruct((M, D), _F32),
        scratch_shapes=[pltpu.VMEM((tm, D), _BF16), pltpu.VMEM((tm, D), _F32)],
        compiler_params=pltpu.CompilerParams(
            dimension_semantics=("arbitrary", "arbitrary"),
            vmem_limit_bytes=VMEM_LIMIT_BYTES),
        name="mlp",
    )(x2d, w1, w2, g, b)


def kernel(x, mem, w_in, conv_w, conv_b, w_a, b_a, w_x, b_x, lru_lambda, w_pool, b_pool, pool_scale,
           w_out, ln1_g, ln1_b, w_q, w_k, w_v, w_o, ln2_g, ln2_b, w_ff1, w_ff2, ln3_g, ln3_b):
    B, S, D = x.shape
    row = lambda p: p.reshape(1, -1)
    for l in range(DEPTH):
        w_ax = jnp.concatenate([w_a[l], w_x[l]], axis=-1).astype(_BF16)
        x = _mixer(x, w_in[l].astype(_BF16), conv_w[l], row(conv_b[l]), w_ax,
                   row(b_a[l]), row(b_x[l]), row(lru_lambda[l]),
                   w_pool[l].astype(_BF16), row(b_pool[l]), row(pool_scale[l]),
                   w_out[l].astype(_BF16), row(ln1_g[l]), row(ln1_b[l]))
        k, v = _kv_proj(mem.reshape(B * N_MEM, D), w_k[l].astype(_BF16), w_v[l].astype(_BF16))
        x = _xattn(x, k.reshape(B, N_MEM, D), v.reshape(B, N_MEM, D),
                   w_q[l].astype(_BF16), w_o[l].astype(_BF16), row(ln2_g[l]), row(ln2_b[l]))
        x = _mlp(x.reshape(B * S, D), w_ff1[l].astype(_BF16), w_ff2[l].astype(_BF16),
                 row(ln3_g[l]), row(ln3_b[l])).reshape(B, S, D)
    return x
```

```python
import functools

import jax
import jax.numpy as jnp
from jax import lax
from jax.experimental import pallas as pl
from jax.experimental.pallas import tpu as pltpu

D_MODEL = 2048
POOL_WIDTH = 1024
LRU_WIDTH = 1024
POOL_WINDOWS = (2, 4, 8, 16)
POOL_GROUP = 256
LRU_HEADS = 8
LRU_HEAD_DIM = 128
CONV_WIDTH = 4
LRU_C = 8.0
N_MEM = 256
XATTN_HEADS = 4
XATTN_HEAD_DIM = 512
D_FF = 4 * D_MODEL
LN_EPS = 1e-5
DEPTH = 1
DEEPNORM_ALPHA = (2.0 * DEPTH) ** 0.25

POOL_HIST = 16
CONV_HIST = 8

MIXER_TILE = 256
XATTN_TILE = 512
MLP_TILE_M = 512
MLP_TILE_F = 1024
KV_TILE_N = 512
VMEM_LIMIT_BYTES = 56 * 1024 * 1024

_F32 = jnp.float32
_BF16 = jnp.bfloat16


def _layer_norm(v, g, b):
    mu = jnp.mean(v, axis=-1, keepdims=True)
    c = v - mu
    var = jnp.mean(jnp.square(c), axis=-1, keepdims=True)
    return c * lax.rsqrt(var + LN_EPS) * g + b


def _dot(a, b):
    return jnp.dot(a, b, preferred_element_type=_F32)


def _shift_rows(a, d, fill, row):
    n = a.shape[0]
    if d % 8 == 0:
        head = jnp.full((d, a.shape[1]), fill, a.dtype)
        return jnp.concatenate([head, a[: n - d]], axis=0)
    return jnp.where(row < d, fill, pltpu.roll(a, d, 0))


def _mixer_kernel(x_ref, w_in_ref, conv_w_ref, conv_b_ref, w_ax_ref, b_a_ref, b_x_ref, lam_ref,
                  w_pool_ref, b_pool_ref, pool_scale_ref, w_out_ref, g_ref, b_ref,
                  o_ref, pool_hist, conv_hist, h_carry):
    T = x_ref.shape[1]
    s = pl.program_id(1)

    @pl.when(s == 0)
    def _():
        pool_hist[...] = jnp.zeros_like(pool_hist)
        conv_hist[...] = jnp.zeros_like(conv_hist)
        h_carry[...] = jnp.zeros_like(h_carry)

    x = x_ref[0]
    proj = _dot(x.astype(_BF16), w_in_ref[...])
    u_pool = proj[:, :POOL_WIDTH]
    u_lru = proj[:, POOL_WIDTH:POOL_WIDTH + LRU_WIDTH]
    u_gate = proj[:, POOL_WIDTH + LRU_WIDTH:]

    row = lax.broadcasted_iota(jnp.int32, (T, 1), 0)
    t = s * T + row

    ext = jnp.concatenate([pool_hist[...], u_pool], axis=0)
    pool_hist[...] = u_pool[T - POOL_HIST:]
    s2 = ext + pltpu.roll(ext, 1, 0)
    s2r = s2[:, POOL_GROUP:]
    s4 = s2r + pltpu.roll(s2r, 2, 0)
    s4r = s4[:, POOL_GROUP:]
    s8 = s4r + pltpu.roll(s4r, 4, 0)
    s8r = s8[:, POOL_GROUP:]
    s16 = s8r + pltpu.roll(s8r, 8, 0)
    sums = (s2[POOL_HIST:, :POOL_GROUP], s4[POOL_HIST:, :POOL_GROUP],
            s8[POOL_HIST:, :POOL_GROUP], s16[POOL_HIST:])
    y_pool = []
    for g, w in enumerate(POOL_WINDOWS):
        cs = slice(g * POOL_GROUP, (g + 1) * POOL_GROUP)
        inv_cnt = 1.0 / jnp.minimum(t + 1, w).astype(_F32)
        mixed = sums[g] * inv_cnt - u_pool[:, cs]
        yg = _dot(mixed.astype(_BF16), w_pool_ref[g]) + b_pool_ref[:, cs]
        y_pool.append(yg * pool_scale_ref[:, cs])

    ext2 = jnp.concatenate([conv_hist[...], u_lru], axis=0)
    conv_hist[...] = u_lru[T - CONV_HIST:]
    xc = pltpu.roll(ext2, 3, 0)[CONV_HIST:] * conv_w_ref[0:1, :]
    xc = xc + pltpu.roll(ext2, 2, 0)[CONV_HIST:] * conv_w_ref[1:2, :]
    xc = xc + pltpu.roll(ext2, 1, 0)[CONV_HIST:] * conv_w_ref[2:3, :]
    xc = xc + u_lru * conv_w_ref[3:4, :]
    xc = xc + conv_b_ref[...]

    xcb = xc.astype(_BF16)
    r_parts, i_parts = [], []
    for h in range(LRU_HEADS):
        hs = slice(h * LRU_HEAD_DIM, (h + 1) * LRU_HEAD_DIM)
        pre = _dot(xcb[:, hs], w_ax_ref[h])
        r_parts.append(jax.nn.sigmoid(pre[:, :LRU_HEAD_DIM] + b_a_ref[:, hs]))
        i_parts.append(jax.nn.sigmoid(pre[:, LRU_HEAD_DIM:] + b_x_ref[:, hs]))
    r = jnp.concatenate(r_parts, axis=1)
    i = jnp.concatenate(i_parts, axis=1)

    log_a = (-LRU_C * r) * jax.nn.softplus(-lam_ref[...])
    a = jnp.exp(log_a)
    mult = jnp.sqrt(-jnp.tanh(log_a) * (a * a + 1.0))
    mult = jnp.where(t == 0, 1.0, mult)
    bv = mult * (i * xc)

    d = 1
    while d < T:
        a_s = _shift_rows(a, d, 1.0, row)
        b_s = _shift_rows(bv, d, 0.0, row)
        bv = a * b_s + bv
        a = a * a_s
        d *= 2
    hseq = bv + a * h_carry[0:1, :]
    h_carry[...] = jnp.broadcast_to(hseq[T - 1:T, :], h_carry.shape)
    y_lru = hseq * jax.nn.gelu(u_gate)

    z = jnp.concatenate(y_pool + [y_lru], axis=1).astype(_BF16)
    y = _dot(z, w_out_ref[...])
    o_ref[0] = _layer_norm(DEEPNORM_ALPHA * x + y, g_ref[...], b_ref[...])


def _const_spec(shape, single_buffer=False):
    nd = len(shape)
    kwargs = {"pipeline_mode": pl.Buffered(1)} if single_buffer else {}
    return pl.BlockSpec(shape, lambda *_: (0,) * nd, **kwargs)


def _mixer(x, w_in, conv_w, conv_b, w_ax, b_a, b_x, lam, w_pool, b_pool, pool_scale, w_out, g, b):
    B, S, D = x.shape
    T = MIXER_TILE
    tile = pl.BlockSpec((1, T, D), lambda bi, si: (bi, si, 0))
    return pl.pallas_call(
        _mixer_kernel,
        grid=(B, S // T),
        in_specs=[
            tile,
            _const_spec(w_in.shape, True),
            _const_spec(conv_w.shape), _const_spec(conv_b.shape),
            _const_spec(w_ax.shape, True),
            _const_spec(b_a.shape), _const_spec(b_x.shape), _const_spec(lam.shape),
            _const_spec(w_pool.shape, True),
            _const_spec(b_pool.shape), _const_spec(pool_scale.shape),
            _const_spec(w_out.shape, True),
            _const_spec(g.shape), _const_spec(b.shape),
        ],
        out_specs=tile,
        out_shape=jax.ShapeDtypeStruct(x.shape, _F32),
        scratch_shapes=[
            pltpu.VMEM((POOL_HIST, POOL_WIDTH), _F32),
            pltpu.VMEM((CONV_HIST, LRU_WIDTH), _F32),
            pltpu.VMEM((8, LRU_WIDTH), _F32),
        ],
        compiler_params=pltpu.CompilerParams(
            dimension_semantics=("arbitrary", "arbitrary"),
            vmem_limit_bytes=VMEM_LIMIT_BYTES),
        name="mixer",
    )(x, w_in, conv_w, conv_b, w_ax, b_a, b_x, lam, w_pool, b_pool, pool_scale, w_out, g, b)


def _kv_kernel(mem_ref, wk_ref, wv_ref, k_ref, v_ref):
    m = mem_ref[...].astype(_BF16)
    k_ref[...] = _dot(m, wk_ref[...]).astype(_BF16)
    v_ref[...] = _dot(m, wv_ref[...]).astype(_BF16)


def _kv_proj(mem2d, w_k, w_v):
    M, D = mem2d.shape
    tn = KV_TILE_N
    wspec = pl.BlockSpec((D, tn), lambda j: (0, j))
    ospec = pl.BlockSpec((M, tn), lambda j: (0, j))
    return pl.pallas_call(
        _kv_kernel,
        grid=(D // tn,),
        in_specs=[_const_spec(mem2d.shape), wspec, wspec],
        out_specs=[ospec, ospec],
        out_shape=[jax.ShapeDtypeStruct((M, D), _BF16)] * 2,
        compiler_params=pltpu.CompilerParams(
            dimension_semantics=("arbitrary",), vmem_limit_bytes=VMEM_LIMIT_BYTES),
        name="kv_proj",
    )(mem2d, w_k, w_v)


def _xattn_kernel(x_ref, k_ref, v_ref, wq_ref, wo_ref, g_ref, b_ref, o_ref):
    x = x_ref[0]
    q = _dot(x.astype(_BF16), wq_ref[...]).astype(_BF16)
    scale = XATTN_HEAD_DIM ** -0.5
    o_parts = []
    for h in range(XATTN_HEADS):
        hs = slice(h * XATTN_HEAD_DIM, (h + 1) * XATTN_HEAD_DIM)
        sc = lax.dot_general(q[:, hs], k_ref[0, :, hs], (((1,), (1,)), ((), ())),
                             preferred_element_type=_F32) * scale
        e = jnp.exp(sc - jnp.max(sc, axis=-1, keepdims=True))
        p = e / jnp.sum(e, axis=-1, keepdims=True)
        o_parts.append(_dot(p.astype(_BF16), v_ref[0, :, hs]))
    o = jnp.concatenate(o_parts, axis=1).astype(_BF16)
    y = _dot(o, wo_ref[...])
    o_ref[0] = _layer_norm(DEEPNORM_ALPHA * x + y, g_ref[...], b_ref[...])


def _xattn(x, k, v, w_q, w_o, g, b):
    B, S, D = x.shape
    T = XATTN_TILE
    tile = pl.BlockSpec((1, T, D), lambda bi, si: (bi, si, 0))
    kvspec = pl.BlockSpec((1, N_MEM, D), lambda bi, si: (bi, 0, 0))
    return pl.pallas_call(
        _xattn_kernel,
        grid=(B, S // T),
        in_specs=[tile, kvspec, kvspec,
                  _const_spec(w_q.shape, True), _const_spec(w_o.shape, True),
                  _const_spec(g.shape), _const_spec(b.shape)],
        out_specs=tile,
        out_shape=jax.ShapeDtypeStruct(x.shape, _F32),
        compiler_params=pltpu.CompilerParams(
            dimension_semantics=("arbitrary", "arbitrary"),
            vmem_limit_bytes=VMEM_LIMIT_BYTES),
        name="xattn",
    )(x, k, v, w_q, w_o, g, b)


def _mlp_kernel(x_ref, w1_ref, w2_ref, g_ref, b_ref, o_ref, xb_ref, acc_ref):
    j = pl.program_id(1)

    @pl.when(j == 0)
    def _():
        xb_ref[...] = x_ref[...].astype(_BF16)
        acc_ref[...] = jnp.zeros_like(acc_ref)

    h = jnp.square(jnp.maximum(_dot(xb_ref[...], w1_ref[...]), 0.0))
    acc_ref[...] += _dot(h.astype(_BF16), w2_ref[...])

    @pl.when(j == pl.num_programs(1) - 1)
    def _():
        o_ref[...] = _layer_norm(DEEPNORM_ALPHA * x_ref[...] + acc_ref[...], g_ref[...], b_ref[...])


def _mlp(x2d, w1, w2, g, b):
    M, D = x2d.shape
    F = w1.shape[1]
    tm, tf = MLP_TILE_M, MLP_TILE_F
    xspec = pl.BlockSpec((tm, D), lambda i, j: (i, 0))
    return pl.pallas_call(
        _mlp_kernel,
        grid=(M // tm, F // tf),
        in_specs=[xspec,
                  pl.BlockSpec((D, tf), lambda i, j: (0, j)),
                  pl.BlockSpec((tf, D), lambda i, j: (j, 0)),
                  _const_spec(g.shape), _const_spec(b.shape)],
        out_specs=xspec,
        out_shape=jax.ShapeDtypeStruct((M, D), _F32),
        scratch_shapes=[pltpu.VMEM((tm, D), _BF16), pltpu.VMEM((tm, D), _F32)],
        compiler_params=pltpu.CompilerParams(
            dimension_semantics=("arbitrary", "arbitrary"),
            vmem_limit_bytes=VMEM_LIMIT_BYTES),
        name="mlp",
    )(x2d, w1, w2, g, b)


def kernel(x, mem, w_in, conv_w, conv_b, w_a, b_a, w_x, b_x, lru_lambda, w_pool, b_pool, pool_scale,
           w_out, ln1_g, ln1_b, w_q, w_k, w_v, w_o, ln2_g, ln2_b, w_ff1, w_ff2, ln3_g, ln3_b):
    B, S, D = x.shape
    row = lambda p: p.reshape(1, -1)
    for l in range(DEPTH):
        w_ax = jnp.concatenate([w_a[l], w_x[l]], axis=-1).astype(_BF16)
        x = _mixer(x, w_in[l].astype(_BF16), conv_w[l], row(conv_b[l]), w_ax,
                   row(b_a[l]), row(b_x[l]), row(lru_lambda[l]),
                   w_pool[l].astype(_BF16), row(b_pool[l]), row(pool_scale[l]),
                   w_out[l].astype(_BF16), row(ln1_g[l]), row(ln1_b[l]))
        k, v = _kv_proj(mem.reshape(B * N_MEM, D), w_k[l].astype(_BF16), w_v[l].astype(_BF16))
        x = _xattn(x, k.reshape(B, N_MEM, D), v.reshape(B, N_MEM, D),
                   w_q[l].astype(_BF16), w_o[l].astype(_BF16), row(ln2_g[l]), row(ln2_b[l]))
        x = _mlp(x.reshape(B * S, D), w_ff1[l].astype(_BF16), w_ff2[l].astype(_BF16),
                 row(ln3_g[l]), row(ln3_b[l])).reshape(B, S, D)
    return x
```

```python
import functools

import jax
import jax.numpy as jnp
from jax import lax
from jax.experimental import pallas as pl
from jax.experimental.pallas import tpu as pltpu

D_MODEL = 2048
POOL_WIDTH = 1024
LRU_WIDTH = 1024
POOL_WINDOWS = (2, 4, 8, 16)
POOL_GROUP = 256
LRU_HEADS = 8
LRU_HEAD_DIM = 128
CONV_WIDTH = 4
LRU_C = 8.0
N_MEM = 256
XATTN_HEADS = 4
XATTN_HEAD_DIM = 512
D_FF = 4 * D_MODEL
LN_EPS = 1e-5
DEPTH = 1
DEEPNORM_ALPHA = (2.0 * DEPTH) ** 0.25

POOL_HIST = 16
CONV_HIST = 8

MIXER_TILE = 256
XATTN_TILE = 512
MLP_TILE_M = 512
MLP_TILE_F = 1024
KV_TILE_N = 512
VMEM_LIMIT_BYTES = 56 * 1024 * 1024

_F32 = jnp.float32
_BF16 = jnp.bfloat16


def _layer_norm(v, g, b):
    mu = jnp.mean(v, axis=-1, keepdims=True)
    c = v - mu
    var = jnp.mean(jnp.square(c), axis=-1, keepdims=True)
    return c * lax.rsqrt(var + LN_EPS) * g + b


def _dot(a, b):
    return jnp.dot(a, b, preferred_element_type=_F32)


def _shift_rows(a, d, fill, row):
    n = a.shape[0]
    if d % 8 == 0:
        head = jnp.full((d, a.shape[1]), fill, a.dtype)
        return jnp.concatenate([head, a[: n - d]], axis=0)
    return jnp.where(row < d, fill, pltpu.roll(a, d, 0))


def _mixer_kernel(x_ref, w_in_ref, conv_w_ref, conv_b_ref, w_ax_ref, b_a_ref, b_x_ref, lam_ref,
                  w_pool_ref, b_pool_ref, pool_scale_ref, w_out_ref, g_ref, b_ref, *rest):
    n_cast = (len(rest) - 4) // 2
    cast_in, o_ref, cast_out = rest[:n_cast], rest[n_cast], rest[n_cast + 1:2 * n_cast + 1]
    pool_hist, conv_hist, h_carry = rest[2 * n_cast + 1:]
    T = x_ref.shape[1]
    s = pl.program_id(1)

    for src, dst in zip(cast_in, cast_out):
        dst[...] = src[...].astype(_BF16)

    @pl.when(s == 0)
    def _():
        pool_hist[...] = jnp.zeros_like(pool_hist)
        conv_hist[...] = jnp.zeros_like(conv_hist)
        h_carry[...] = jnp.zeros_like(h_carry)

    x = x_ref[0]
    proj = _dot(x.astype(_BF16), w_in_ref[...])
    u_pool = proj[:, :POOL_WIDTH]
    u_lru = proj[:, POOL_WIDTH:POOL_WIDTH + LRU_WIDTH]
    u_gate = proj[:, POOL_WIDTH + LRU_WIDTH:]

    row = lax.broadcasted_iota(jnp.int32, (T, 1), 0)
    t = s * T + row

    ext = jnp.concatenate([pool_hist[...], u_pool], axis=0)
    pool_hist[...] = u_pool[T - POOL_HIST:]
    s2 = ext + pltpu.roll(ext, 1, 0)
    s2r = s2[:, POOL_GROUP:]
    s4 = s2r + pltpu.roll(s2r, 2, 0)
    s4r = s4[:, POOL_GROUP:]
    s8 = s4r + pltpu.roll(s4r, 4, 0)
    s8r = s8[:, POOL_GROUP:]
    s16 = s8r + pltpu.roll(s8r, 8, 0)
    sums = (s2[POOL_HIST:, :POOL_GROUP], s4[POOL_HIST:, :POOL_GROUP],
            s8[POOL_HIST:, :POOL_GROUP], s16[POOL_HIST:])
    y_pool = []
    for g, w in enumerate(POOL_WINDOWS):
        cs = slice(g * POOL_GROUP, (g + 1) * POOL_GROUP)
        inv_cnt = 1.0 / jnp.minimum(t + 1, w).astype(_F32)
        mixed = sums[g] * inv_cnt - u_pool[:, cs]
        yg = _dot(mixed.astype(_BF16), w_pool_ref[g]) + b_pool_ref[:, cs]
        y_pool.append(yg * pool_scale_ref[:, cs])

    ext2 = jnp.concatenate([conv_hist[...], u_lru], axis=0)
    conv_hist[...] = u_lru[T - CONV_HIST:]
    xc = pltpu.roll(ext2, 3, 0)[CONV_HIST:] * conv_w_ref[0:1, :]
    xc = xc + pltpu.roll(ext2, 2, 0)[CONV_HIST:] * conv_w_ref[1:2, :]
    xc = xc + pltpu.roll(ext2, 1, 0)[CONV_HIST:] * conv_w_ref[2:3, :]
    xc = xc + u_lru * conv_w_ref[3:4, :]
    xc = xc + conv_b_ref[...]

    xcb = xc.astype(_BF16)
    r_parts, i_parts = [], []
    for h in range(LRU_HEADS):
        hs = slice(h * LRU_HEAD_DIM, (h + 1) * LRU_HEAD_DIM)
        pre = _dot(xcb[:, hs], w_ax_ref[h])
        r_parts.append(jax.nn.sigmoid(pre[:, :LRU_HEAD_DIM] + b_a_ref[:, hs]))
        i_parts.append(jax.nn.sigmoid(pre[:, LRU_HEAD_DIM:] + b_x_ref[:, hs]))
    r = jnp.concatenate(r_parts, axis=1)
    i = jnp.concatenate(i_parts, axis=1)

    log_a = (-LRU_C * r) * jax.nn.softplus(-lam_ref[...])
    a = jnp.exp(log_a)
    mult = jnp.sqrt(-jnp.tanh(log_a) * (a * a + 1.0))
    mult = jnp.where(t == 0, 1.0, mult)
    bv = mult * (i * xc)

    d = 1
    while d < T:
        a_s = _shift_rows(a, d, 1.0, row)
        b_s = _shift_rows(bv, d, 0.0, row)
        bv = a * b_s + bv
        a = a * a_s
        d *= 2
    hseq = bv + a * h_carry[0:1, :]
    h_carry[...] = jnp.broadcast_to(hseq[T - 1:T, :], h_carry.shape)
    y_lru = hseq * jax.nn.gelu(u_gate)

    z = jnp.concatenate(y_pool + [y_lru], axis=1).astype(_BF16)
    y = _dot(z, w_out_ref[...])
    o_ref[0] = _layer_norm(DEEPNORM_ALPHA * x + y, g_ref[...], b_ref[...])


def _const_spec(shape, single_buffer=False):
    nd = len(shape)
    kwargs = {"pipeline_mode": pl.Buffered(1)} if single_buffer else {}
    return pl.BlockSpec(shape, lambda *_: (0,) * nd, **kwargs)


def _mixer(x, w_in, conv_w, conv_b, w_ax, b_a, b_x, lam, w_pool, b_pool, pool_scale, w_out, g, b,
           cast_weights):
    B, S, D = x.shape
    T = MIXER_TILE
    n_seq = S // T
    n_steps = B * n_seq
    tile = pl.BlockSpec((1, T, D), lambda bi, si: (bi, si, 0))
    chunk_specs = []
    for w in cast_weights:
        assert w.shape[0] % n_steps == 0
        chunk_specs.append(pl.BlockSpec((w.shape[0] // n_steps, w.shape[1]),
                                        lambda bi, si: (bi * n_seq + si, 0)))
    return pl.pallas_call(
        _mixer_kernel,
        grid=(B, S // T),
        in_specs=[
            tile,
            _const_spec(w_in.shape, True),
            _const_spec(conv_w.shape), _const_spec(conv_b.shape),
            _const_spec(w_ax.shape, True),
            _const_spec(b_a.shape), _const_spec(b_x.shape), _const_spec(lam.shape),
            _const_spec(w_pool.shape, True),
            _const_spec(b_pool.shape), _const_spec(pool_scale.shape),
            _const_spec(w_out.shape, True),
            _const_spec(g.shape), _const_spec(b.shape),
        ] + chunk_specs,
        out_specs=[tile] + chunk_specs,
        out_shape=[jax.ShapeDtypeStruct(x.shape, _F32)]
        + [jax.ShapeDtypeStruct(w.shape, _BF16) for w in cast_weights],
        scratch_shapes=[
            pltpu.VMEM((POOL_HIST, POOL_WIDTH), _F32),
            pltpu.VMEM((CONV_HIST, LRU_WIDTH), _F32),
            pltpu.VMEM((8, LRU_WIDTH), _F32),
        ],
        compiler_params=pltpu.CompilerParams(
            dimension_semantics=("arbitrary", "arbitrary"),
            vmem_limit_bytes=VMEM_LIMIT_BYTES),
        name="mixer",
    )(x, w_in, conv_w, conv_b, w_ax, b_a, b_x, lam, w_pool, b_pool, pool_scale, w_out, g, b, *cast_weights)


def _kv_kernel(mem_ref, wk_ref, wv_ref, k_ref, v_ref):
    m = mem_ref[...].astype(_BF16)
    k_ref[...] = _dot(m, wk_ref[...]).astype(_BF16)
    v_ref[...] = _dot(m, wv_ref[...]).astype(_BF16)


def _kv_proj(mem2d, w_k, w_v):
    M, D = mem2d.shape
    tn = KV_TILE_N
    wspec = pl.BlockSpec((D, tn), lambda j: (0, j))
    ospec = pl.BlockSpec((M, tn), lambda j: (0, j))
    return pl.pallas_call(
        _kv_kernel,
        grid=(D // tn,),
        in_specs=[_const_spec(mem2d.shape), wspec, wspec],
        out_specs=[ospec, ospec],
        out_shape=[jax.ShapeDtypeStruct((M, D), _BF16)] * 2,
        compiler_params=pltpu.CompilerParams(
            dimension_semantics=("arbitrary",), vmem_limit_bytes=VMEM_LIMIT_BYTES),
        name="kv_proj",
    )(mem2d, w_k, w_v)


def _xattn_kernel(x_ref, k_ref, v_ref, wq_ref, wo_ref, g_ref, b_ref, o_ref):
    x = x_ref[0]
    q = _dot(x.astype(_BF16), wq_ref[...]).astype(_BF16)
    scale = XATTN_HEAD_DIM ** -0.5
    o_parts = []
    for h in range(XATTN_HEADS):
        hs = slice(h * XATTN_HEAD_DIM, (h + 1) * XATTN_HEAD_DIM)
        sc = lax.dot_general(q[:, hs], k_ref[0, :, hs], (((1,), (1,)), ((), ())),
                             preferred_element_type=_F32) * scale
        e = jnp.exp(sc - jnp.max(sc, axis=-1, keepdims=True))
        p = e / jnp.sum(e, axis=-1, keepdims=True)
        o_parts.append(_dot(p.astype(_BF16), v_ref[0, :, hs]))
    o = jnp.concatenate(o_parts, axis=1).astype(_BF16)
    y = _dot(o, wo_ref[...])
    o_ref[0] = _layer_norm(DEEPNORM_ALPHA * x + y, g_ref[...], b_ref[...])


def _xattn(x, k, v, w_q, w_o, g, b):
    B, S, D = x.shape
    T = XATTN_TILE
    tile = pl.BlockSpec((1, T, D), lambda bi, si: (bi, si, 0))
    kvspec = pl.BlockSpec((1, N_MEM, D), lambda bi, si: (bi, 0, 0))
    return pl.pallas_call(
        _xattn_kernel,
        grid=(B, S // T),
        in_specs=[tile, kvspec, kvspec,
                  _const_spec(w_q.shape, True), _const_spec(w_o.shape, True),
                  _const_spec(g.shape), _const_spec(b.shape)],
        out_specs=tile,
        out_shape=jax.ShapeDtypeStruct(x.shape, _F32),
        compiler_params=pltpu.CompilerParams(
            dimension_semantics=("arbitrary", "arbitrary"),
            vmem_limit_bytes=VMEM_LIMIT_BYTES),
        name="xattn",
    )(x, k, v, w_q, w_o, g, b)


def _mlp_kernel(x_ref, w1_ref, w2_ref, g_ref, b_ref, o_ref, xb_ref, acc_ref):
    j = pl.program_id(1)

    @pl.when(j == 0)
    def _():
        xb_ref[...] = x_ref[...].astype(_BF16)
        acc_ref[...] = jnp.zeros_like(acc_ref)

    h = jnp.square(jnp.maximum(_dot(xb_ref[...], w1_ref[...]), 0.0))
    acc_ref[...] += _dot(h.astype(_BF16), w2_ref[...])

    @pl.when(j == pl.num_programs(1) - 1)
    def _():
        o_ref[...] = _layer_norm(DEEPNORM_ALPHA * x_ref[...] + acc_ref[...], g_ref[...], b_ref[...])


def _mlp(x2d, w1, w2, g, b):
    M, D = x2d.shape
    F = w1.shape[1]
    tm, tf = MLP_TILE_M, MLP_TILE_F
    xspec = pl.BlockSpec((tm, D), lambda i, j: (i, 0))
    return pl.pallas_call(
        _mlp_kernel,
        grid=(M // tm, F // tf),
        in_specs=[xspec,
                  pl.BlockSpec((D, tf), lambda i, j: (0, j)),
                  pl.BlockSpec((tf, D), lambda i, j: (j, 0)),
                  _const_spec(g.shape), _const_spec(b.shape)],
        out_specs=xspec,
        out_shape=jax.ShapeDtypeStruct((M, D), _F32),
        scratch_shapes=[pltpu.VMEM((tm, D), _BF16), pltpu.VMEM((tm, D), _F32)],
        compiler_params=pltpu.CompilerParams(
            dimension_semantics=("arbitrary", "arbitrary"),
            vmem_limit_bytes=VMEM_LIMIT_BYTES),
        name="mlp",
    )(x2d, w1, w2, g, b)


def kernel(x, mem, w_in, conv_w, conv_b, w_a, b_a, w_x, b_x, lru_lambda, w_pool, b_pool, pool_scale,
           w_out, ln1_g, ln1_b, w_q, w_k, w_v, w_o, ln2_g, ln2_b, w_ff1, w_ff2, ln3_g, ln3_b):
    B, S, D = x.shape
    row = lambda p: p.reshape(1, -1)
    for l in range(DEPTH):
        w_ax = jnp.concatenate([w_a[l], w_x[l]], axis=-1).astype(_BF16)
        x, wq_b, wo_b, w1_b, w2_b = _mixer(
            x, w_in[l].astype(_BF16), conv_w[l], row(conv_b[l]), w_ax,
            row(b_a[l]), row(b_x[l]), row(lru_lambda[l]),
            w_pool[l].astype(_BF16), row(b_pool[l]), row(pool_scale[l]),
            w_out[l].astype(_BF16), row(ln1_g[l]), row(ln1_b[l]),
            (w_q[l], w_o[l], w_ff1[l], w_ff2[l]))
        k, v = _kv_proj(mem.reshape(B * N_MEM, D), w_k[l].astype(_BF16), w_v[l].astype(_BF16))
        x = _xattn(x, k.reshape(B, N_MEM, D), v.reshape(B, N_MEM, D),
                   wq_b, wo_b, row(ln2_g[l]), row(ln2_b[l]))
        x = _mlp(x.reshape(B * S, D), w1_b, w2_b, row(ln3_g[l]), row(ln3_b[l])).reshape(B, S, D)
    return x
```

```python
import jax
import jax.numpy as jnp
from jax import lax
from jax.experimental import pallas as pl
from jax.experimental.pallas import tpu as pltpu

D_MODEL = 2048
POOL_WIDTH = 1024
LRU_WIDTH = 1024
POOL_WINDOWS = (2, 4, 8, 16)
POOL_GROUP = 256
LRU_HEADS = 8
LRU_HEAD_DIM = 128
CONV_WIDTH = 4
LRU_C = 8.0
N_MEM = 256
XATTN_HEADS = 4
XATTN_HEAD_DIM = 512
D_FF = 4 * D_MODEL
LN_EPS = 1e-5
DEPTH = 1
DEEPNORM_ALPHA = (2.0 * DEPTH) ** 0.25

POOL_HIST = 16
SUBLANES = 8
LANES = 128

MIXER_TILE = 256
XATTN_TILE = 512
MLP_TILE_M = 512
MLP_TILE_F = 1024
KV_TILE_N = 512
VMEM_LIMIT_BYTES = 56 * 1024 * 1024

_F32 = jnp.float32
_BF16 = jnp.bfloat16


def _layer_norm(v, g, b):
    mu = jnp.mean(v, axis=-1, keepdims=True)
    c = v - mu
    var = jnp.mean(jnp.square(c), axis=-1, keepdims=True)
    return c * lax.rsqrt(var + LN_EPS) * g + b


def _dot(a, b):
    return jnp.dot(a, b, preferred_element_type=_F32)


def _store_chunked(dst_ref, slab0, val, lc, pitch):
    for c in range(val.shape[1] // LANES):
        for j in range(SUBLANES):
            dst_ref[slab0 + c, j * pitch:j * pitch + lc, :] = (
                val[j * lc:(j + 1) * lc, c * LANES:(c + 1) * LANES])


def _load_chunked(src_ref, slab0, n_slabs, lc, pitch):
    return jnp.concatenate(
        [jnp.concatenate([src_ref[slab0 + c, j * pitch:j * pitch + lc, :] for j in range(SUBLANES)], axis=0)
         for c in range(n_slabs)], axis=-1)


def _load_interleaved(src_ref, slab0, n_slabs, lc, pitch):
    cols = [jnp.stack([src_ref[slab0 + c, pl.ds(k, SUBLANES, stride=pitch), :] for k in range(lc)], axis=0)
            for c in range(n_slabs)]
    return jnp.concatenate(cols, axis=-1)


def _store_interleaved(dst_ref, slab0, val3, pitch):
    for c in range(val3.shape[2] // LANES):
        for k in range(val3.shape[0]):
            dst_ref[slab0 + c, pl.ds(k, SUBLANES, stride=pitch), :] = val3[k, :, c * LANES:(c + 1) * LANES]


def _history_rows(cur_tail, prev_tail):
    sub = lax.broadcasted_iota(jnp.int32, (1,) + cur_tail.shape[1:], 1)
    return jnp.where(sub == 0, pltpu.roll(prev_tail, 1, 1), pltpu.roll(cur_tail, 1, 1))


def _mixer_kernel(x_ref, w_in_ref, conv_w_ref, conv_b_ref, w_ax_ref, b_a_ref, b_x_ref, lam_ref,
                  w_pool_ref, b_pool_ref, pool_scale_ref, w_out_ref, g_ref, b_ref, *rest):
    n_cast = (len(rest) - 9) // 2
    cast_in, o_ref, cast_out = rest[:n_cast], rest[n_cast], rest[n_cast + 1:2 * n_cast + 1]
    up_ref, ul_ref, ug_ref, zp_ref, zl_ref, pool_hist, conv_hist, h_carry = rest[2 * n_cast + 1:]
    T = x_ref.shape[1]
    lc = T // SUBLANES
    pitch = lc + SUBLANES
    s = pl.program_id(1)

    for src, dst in zip(cast_in, cast_out):
        dst[...] = src[...].astype(_BF16)

    @pl.when(s == 0)
    def _():
        pool_hist[...] = jnp.zeros_like(pool_hist)
        conv_hist[...] = jnp.zeros_like(conv_hist)
        h_carry[...] = jnp.zeros_like(h_carry)

    x = x_ref[0]
    xb = x.astype(_BF16)
    n_pool, n_lru = POOL_WIDTH // LANES, LRU_WIDTH // LANES
    _store_chunked(up_ref, 0, _dot(xb, w_in_ref[:, :POOL_WIDTH]), lc, pitch)
    up = _load_interleaved(up_ref, 0, n_pool, lc, pitch)
    _store_chunked(ul_ref, 0, _dot(xb, w_in_ref[:, POOL_WIDTH:POOL_WIDTH + LRU_WIDTH]), lc, pitch)

    kk = lax.broadcasted_iota(jnp.int32, (POOL_HIST, SUBLANES, LANES), 0)
    jj = lax.broadcasted_iota(jnp.int32, (POOL_HIST, SUBLANES, LANES), 1)
    t_head = s * T + jj * lc + kk

    ext = jnp.concatenate([_history_rows(up[lc - POOL_HIST:], pool_hist[...]), up], axis=0)
    pool_hist[...] = up[lc - POOL_HIST:]
    s2 = ext[1:] + ext[:-1]
    s2r = s2[:, :, POOL_GROUP:]
    s4 = s2r[2:] + s2r[:-2]
    s4r = s4[:, :, POOL_GROUP:]
    s8 = s4r[4:] + s4r[:-4]
    s8r = s8[:, :, POOL_GROUP:]
    s16 = s8r[8:] + s8r[:-8]
    sums = (s2[15:, :, :POOL_GROUP], s4[13:, :, :POOL_GROUP], s8[9:, :, :POOL_GROUP], s16[1:])
    y_pool = []
    for g, w in enumerate(POOL_WINDOWS):
        cs = slice(g * POOL_GROUP, (g + 1) * POOL_GROUP)
        inv_head = 1.0 / jnp.minimum(t_head + 1, w).astype(_F32)
        inv_head = jnp.concatenate([inv_head] * (POOL_GROUP // LANES), axis=-1)
        mean = jnp.concatenate([sums[g][:POOL_HIST] * inv_head, sums[g][POOL_HIST:] * (1.0 / w)], axis=0)
        mixed = mean - up[:, :, cs]
        yg = _dot(mixed.reshape(T, POOL_GROUP).astype(_BF16), w_pool_ref[g]).reshape(lc, SUBLANES, POOL_GROUP)
        y_pool.append((yg + b_pool_ref[:, cs]) * pool_scale_ref[:, cs])
    _store_interleaved(zp_ref, 0, jnp.concatenate(y_pool, axis=-1), pitch)

    z_pool = _load_chunked(zp_ref, 0, n_pool, lc, pitch).astype(_BF16)
    n_hist = CONV_WIDTH - 1
    sub = lax.broadcasted_iota(jnp.int32, (SUBLANES, LRU_HEAD_DIM), 0)
    first = t_head[0:1] == 0
    gate_cols = 2 * LRU_HEAD_DIM
    out_cols = D_MODEL // LRU_HEADS
    y_top = []
    for h in range(LRU_HEADS):
        hs = slice(h * LRU_HEAD_DIM, (h + 1) * LRU_HEAD_DIM)
        ul = _load_interleaved(ul_ref, h, 1, lc, pitch)
        ext2 = jnp.concatenate([_history_rows(ul[lc - n_hist:], conv_hist[:, :, hs]), ul], axis=0)
        conv_hist[:, :, hs] = ul[lc - n_hist:]
        xc = ext2[0:lc] * conv_w_ref[0:1, hs]
        for k in range(1, CONV_WIDTH):
            xc = xc + ext2[k:k + lc] * conv_w_ref[k:k + 1, hs]
        xc = xc + conv_b_ref[:, hs]

        pre = _dot(xc.reshape(T, LRU_HEAD_DIM).astype(_BF16), w_ax_ref[h])
        pre = pre.reshape(lc, SUBLANES, 2 * LRU_HEAD_DIM)
        r = jax.nn.sigmoid(pre[:, :, :LRU_HEAD_DIM] + b_a_ref[:, hs])
        i = jax.nn.sigmoid(pre[:, :, LRU_HEAD_DIM:] + b_x_ref[:, hs])

        if h % 2 == 0:
            c0 = POOL_WIDTH + LRU_WIDTH + h * LRU_HEAD_DIM
            _store_chunked(ug_ref, h, _dot(xb, w_in_ref[:, c0:c0 + gate_cols]), lc, pitch)
        else:
            c0 = (h // 2) * out_cols
            y_top.append(_dot(z_pool, w_out_ref[:POOL_WIDTH, c0:c0 + out_cols]))

        log_a = (-LRU_C * r) * jax.nn.softplus(-lam_ref[:, hs])
        a = jnp.exp(log_a)
        mult = jnp.sqrt(-jnp.tanh(log_a) * (a * a + 1.0))
        mult = jnp.concatenate([jnp.where(first, 1.0, mult[0:1]), mult[1:]], axis=0)
        bv = mult * (i * xc)

        hl, ac = [bv[0]], [a[0]]
        for k in range(1, lc):
            hl.append(a[k] * hl[-1] + bv[k])
            ac.append(a[k] * ac[-1])
        c_a, c_b = ac[-1], hl[-1]
        for d in (1, 2, 4):
            a_s = jnp.where(sub < d, 1.0, pltpu.roll(c_a, d, 0))
            b_s = jnp.where(sub < d, 0.0, pltpu.roll(c_b, d, 0))
            c_b = c_a * b_s + c_b
            c_a = c_a * a_s
        h_prev = h_carry[:, hs]
        h_end = c_b + c_a * h_prev
        h_in = jnp.where(sub == 0, h_prev, pltpu.roll(h_end, 1, 0))
        h_carry[:, hs] = jnp.broadcast_to(h_end[SUBLANES - 1:SUBLANES, :], (SUBLANES, LRU_HEAD_DIM))
        hseq = jnp.stack(hl, axis=0) + jnp.stack(ac, axis=0) * h_in
        ug = _load_interleaved(ug_ref, h, 1, lc, pitch)
        _store_interleaved(zl_ref, h, hseq * jax.nn.gelu(ug), pitch)

    for q in range(LRU_HEADS // 2, LRU_HEADS):
        y_top.append(_dot(z_pool, w_out_ref[:POOL_WIDTH, q * out_cols:(q + 1) * out_cols]))
    z_lru = _load_chunked(zl_ref, 0, n_lru, lc, pitch).astype(_BF16)
    y = jnp.concatenate(y_top, axis=1) + _dot(z_lru, w_out_ref[POOL_WIDTH:, :])
    o_ref[0] = _layer_norm(DEEPNORM_ALPHA * x + y, g_ref[...], b_ref[...])


def _const_spec(shape, single_buffer=False):
    nd = len(shape)
    kwargs = {"pipeline_mode": pl.Buffered(1)} if single_buffer else {}
    return pl.BlockSpec(shape, lambda *_: (0,) * nd, **kwargs)


def _mixer(x, w_in, conv_w, conv_b, w_ax, b_a, b_x, lam, w_pool, b_pool, pool_scale, w_out, g, b,
           cast_weights):
    B, S, D = x.shape
    T = MIXER_TILE
    n_seq = S // T
    n_steps = B * n_seq
    assert T % SUBLANES == 0 and T // SUBLANES >= POOL_HIST and (T // SUBLANES) % (2 * SUBLANES) == 0
    rows = SUBLANES * (T // SUBLANES + SUBLANES)
    tile = pl.BlockSpec((1, T, D), lambda bi, si: (bi, si, 0))
    chunk_specs = []
    for w in cast_weights:
        assert w.shape[0] % n_steps == 0
        chunk_specs.append(pl.BlockSpec((w.shape[0] // n_steps, w.shape[1]),
                                        lambda bi, si: (bi * n_seq + si, 0)))
    return pl.pallas_call(
        _mixer_kernel,
        grid=(B, S // T),
        in_specs=[
            tile,
            _const_spec(w_in.shape, True),
            _const_spec(conv_w.shape), _const_spec(conv_b.shape),
            _const_spec(w_ax.shape, True),
            _const_spec(b_a.shape), _const_spec(b_x.shape), _const_spec(lam.shape),
            _const_spec(w_pool.shape, True),
            _const_spec(b_pool.shape), _const_spec(pool_scale.shape),
            _const_spec(w_out.shape, True),
            _const_spec(g.shape), _const_spec(b.shape),
        ] + chunk_specs,
        out_specs=[tile] + chunk_specs,
        out_shape=[jax.ShapeDtypeStruct(x.shape, _F32)]
        + [jax.ShapeDtypeStruct(w.shape, _BF16) for w in cast_weights],
        scratch_shapes=[
            pltpu.VMEM((POOL_WIDTH // LANES, rows, LANES), _F32),
            pltpu.VMEM((LRU_WIDTH // LANES, rows, LANES), _F32),
            pltpu.VMEM((LRU_WIDTH // LANES, rows, LANES), _F32),
            pltpu.VMEM((POOL_WIDTH // LANES, rows, LANES), _F32),
            pltpu.VMEM((LRU_WIDTH // LANES, rows, LANES), _F32),
            pltpu.VMEM((POOL_HIST, SUBLANES, POOL_WIDTH), _F32),
            pltpu.VMEM((CONV_WIDTH - 1, SUBLANES, LRU_WIDTH), _F32),
            pltpu.VMEM((SUBLANES, LRU_WIDTH), _F32),
        ],
        compiler_params=pltpu.CompilerParams(
            dimension_semantics=("arbitrary", "arbitrary"),
            vmem_limit_bytes=VMEM_LIMIT_BYTES),
        name="mixer",
    )(x, w_in, conv_w, conv_b, w_ax, b_a, b_x, lam, w_pool, b_pool, pool_scale, w_out, g, b, *cast_weights)


def _kv_kernel(mem_ref, wk_ref, wv_ref, k_ref, v_ref):
    m = mem_ref[...].astype(_BF16)
    k_ref[...] = _dot(m, wk_ref[...]).astype(_BF16)
    v_ref[...] = _dot(m, wv_ref[...]).astype(_BF16)


def _kv_proj(mem2d, w_k, w_v):
    M, D = mem2d.shape
    tn = KV_TILE_N
    wspec = pl.BlockSpec((D, tn), lambda j: (0, j))
    ospec = pl.BlockSpec((M, tn), lambda j: (0, j))
    return pl.pallas_call(
        _kv_kernel,
        grid=(D // tn,),
        in_specs=[_const_spec(mem2d.shape), wspec, wspec],
        out_specs=[ospec, ospec],
        out_shape=[jax.ShapeDtypeStruct((M, D), _BF16)] * 2,
        compiler_params=pltpu.CompilerParams(
            dimension_semantics=("arbitrary",), vmem_limit_bytes=VMEM_LIMIT_BYTES),
        name="kv_proj",
    )(mem2d, w_k, w_v)


def _xattn_kernel(x_ref, k_ref, v_ref, wq_ref, wo_ref, g_ref, b_ref, o_ref):
    x = x_ref[0]
    q = _dot(x.astype(_BF16), wq_ref[...]).astype(_BF16)
    scale = XATTN_HEAD_DIM ** -0.5
    o_parts = []
    for h in range(XATTN_HEADS):
        hs = slice(h * XATTN_HEAD_DIM, (h + 1) * XATTN_HEAD_DIM)
        sc = lax.dot_general(q[:, hs], k_ref[0, :, hs], (((1,), (1,)), ((), ())),
                             preferred_element_type=_F32) * scale
        e = jnp.exp(sc - jnp.max(sc, axis=-1, keepdims=True))
        p = e / jnp.sum(e, axis=-1, keepdims=True)
        o_parts.append(_dot(p.astype(_BF16), v_ref[0, :, hs]))
    o = jnp.concatenate(o_parts, axis=1).astype(_BF16)
    y = _dot(o, wo_ref[...])
    o_ref[0] = _layer_norm(DEEPNORM_ALPHA * x + y, g_ref[...], b_ref[...])


def _xattn(x, k, v, w_q, w_o, g, b):
    B, S, D = x.shape
    T = XATTN_TILE
    tile = pl.BlockSpec((1, T, D), lambda bi, si: (bi, si, 0))
    kvspec = pl.BlockSpec((1, N_MEM, D), lambda bi, si: (bi, 0, 0))
    return pl.pallas_call(
        _xattn_kernel,
        grid=(B, S // T),
        in_specs=[tile, kvspec, kvspec,
                  _const_spec(w_q.shape, True), _const_spec(w_o.shape, True),
                  _const_spec(g.shape), _const_spec(b.shape)],
        out_specs=tile,
        out_shape=jax.ShapeDtypeStruct(x.shape, _F32),
        compiler_params=pltpu.CompilerParams(
            dimension_semantics=("arbitrary", "arbitrary"),
            vmem_limit_bytes=VMEM_LIMIT_BYTES),
        name="xattn",
    )(x, k, v, w_q, w_o, g, b)


def _mlp_kernel(x_ref, w1_ref, w2_ref, g_ref, b_ref, o_ref, xb_ref, acc_ref):
    j = pl.program_id(1)

    @pl.when(j == 0)
    def _():
        xb_ref[...] = x_ref[...].astype(_BF16)
        acc_ref[...] = jnp.zeros_like(acc_ref)

    h = jnp.square(jnp.maximum(_dot(xb_ref[...], w1_ref[...]), 0.0))
    acc_ref[...] += _dot(h.astype(_BF16), w2_ref[...])

    @pl.when(j == pl.num_programs(1) - 1)
    def _():
        o_ref[...] = _layer_norm(DEEPNORM_ALPHA * x_ref[...] + acc_ref[...], g_ref[...], b_ref[...])


def _mlp(x2d, w1, w2, g, b):
    M, D = x2d.shape
    F = w1.shape[1]
    tm, tf = MLP_TILE_M, MLP_TILE_F
    xspec = pl.BlockSpec((tm, D), lambda i, j: (i, 0))
    return pl.pallas_call(
        _mlp_kernel,
        grid=(M // tm, F // tf),
        in_specs=[xspec,
                  pl.BlockSpec((D, tf), lambda i, j: (0, j)),
                  pl.BlockSpec((tf, D), lambda i, j: (j, 0)),
                  _const_spec(g.shape), _const_spec(b.shape)],
        out_specs=xspec,
        out_shape=jax.ShapeDtypeStruct((M, D), _F32),
        scratch_shapes=[pltpu.VMEM((tm, D), _BF16), pltpu.VMEM((tm, D), _F32)],
        compiler_params=pltpu.CompilerParams(
            dimension_semantics=("arbitrary", "arbitrary"),
            vmem_limit_bytes=VMEM_LIMIT_BYTES),
        name="mlp",
    )(x2d, w1, w2, g, b)


def kernel(x, mem, w_in, conv_w, conv_b, w_a, b_a, w_x, b_x, lru_lambda, w_pool, b_pool, pool_scale,
           w_out, ln1_g, ln1_b, w_q, w_k, w_v, w_o, ln2_g, ln2_b, w_ff1, w_ff2, ln3_g, ln3_b):
    B, S, D = x.shape
    row = lambda p: p.reshape(1, -1)
    for l in range(DEPTH):
        w_ax = jnp.concatenate([w_a[l], w_x[l]], axis=-1).astype(_BF16)
        x, wq_b, wo_b, w1_b, w2_b = _mixer(
            x, w_in[l].astype(_BF16), conv_w[l], row(conv_b[l]), w_ax,
            row(b_a[l]), row(b_x[l]), row(lru_lambda[l]),
            w_pool[l].astype(_BF16), row(b_pool[l]), row(pool_scale[l]),
            w_out[l].astype(_BF16), row(ln1_g[l]), row(ln1_b[l]),
            (w_q[l], w_o[l], w_ff1[l], w_ff2[l]))
        k, v = _kv_proj(mem.reshape(B * N_MEM, D), w_k[l].astype(_BF16), w_v[l].astype(_BF16))
        x = _xattn(x, k.reshape(B, N_MEM, D), v.reshape(B, N_MEM, D),
                   wq_b, wo_b, row(ln2_g[l]), row(ln2_b[l]))
        x = _mlp(x.reshape(B * S, D), w1_b, w2_b, row(ln3_g[l]), row(ln3_b[l])).reshape(B, S, D)
    return x
```

```python
import functools

import jax
import jax.numpy as jnp
from jax import lax
from jax.experimental import pallas as pl
from jax.experimental.pallas import tpu as pltpu

D_MODEL = 2048
POOL_WIDTH = 1024
LRU_WIDTH = 1024
POOL_WINDOWS = (2, 4, 8, 16)
POOL_GROUP = 256
LRU_HEADS = 8
LRU_HEAD_DIM = 128
CONV_WIDTH = 4
LRU_C = 8.0
N_MEM = 256
XATTN_HEADS = 4
XATTN_HEAD_DIM = 512
D_FF = 4 * D_MODEL
LN_EPS = 1e-5
DEPTH = 1
DEEPNORM_ALPHA = (2.0 * DEPTH) ** 0.25

POOL_HIST = 16
SUBLANES = 8
LANES = 128

MIXER_TILE = 256
XATTN_TILE = 512
XATTN_ANCHOR_COLS = 256
MLP_TILE_M = 512
MLP_TILE_F = 1024
KV_TILE_N = 512
VMEM_LIMIT_BYTES = 56 * 1024 * 1024

_F32 = jnp.float32
_BF16 = jnp.bfloat16


def _layer_norm(v, g, b):
    mu = jnp.mean(v, axis=-1, keepdims=True)
    c = v - mu
    var = jnp.mean(jnp.square(c), axis=-1, keepdims=True)
    return c * lax.rsqrt(var + LN_EPS) * g + b


def _dot(a, b):
    return jnp.dot(a, b, preferred_element_type=_F32)


def _ordering_zero(v):
    rows, cols = v.shape
    t = jnp.sum(v.reshape(rows // SUBLANES, SUBLANES, cols), axis=0)
    t = functools.reduce(lambda a, b: a + b, [t[:, c:c + LANES] for c in range(0, cols, LANES)])
    return jnp.minimum(jnp.abs(t), 0.0)


def _store_chunked(dst_ref, slab0, val, lc, pitch):
    for c in range(val.shape[1] // LANES):
        for j in range(SUBLANES):
            dst_ref[slab0 + c, j * pitch:j * pitch + lc, :] = (
                val[j * lc:(j + 1) * lc, c * LANES:(c + 1) * LANES])


def _load_chunked(src_ref, slab0, n_slabs, lc, pitch):
    return jnp.concatenate(
        [jnp.concatenate([src_ref[slab0 + c, j * pitch:j * pitch + lc, :] for j in range(SUBLANES)], axis=0)
         for c in range(n_slabs)], axis=-1)


def _load_interleaved(src_ref, slab0, n_slabs, lc, pitch):
    cols = [jnp.stack([src_ref[slab0 + c, pl.ds(k, SUBLANES, stride=pitch), :] for k in range(lc)], axis=0)
            for c in range(n_slabs)]
    return jnp.concatenate(cols, axis=-1)


def _store_interleaved(dst_ref, slab0, val3, pitch):
    for c in range(val3.shape[2] // LANES):
        for k in range(val3.shape[0]):
            dst_ref[slab0 + c, pl.ds(k, SUBLANES, stride=pitch), :] = val3[k, :, c * LANES:(c + 1) * LANES]


def _history_rows(cur_tail, prev_tail):
    sub = lax.broadcasted_iota(jnp.int32, (1,) + cur_tail.shape[1:], 1)
    return jnp.where(sub == 0, pltpu.roll(prev_tail, 1, 1), pltpu.roll(cur_tail, 1, 1))


def _mixer_kernel(x_ref, w_in_ref, conv_w_ref, conv_b_ref, w_ax_ref, b_a_ref, b_x_ref, lam_ref,
                  w_pool_ref, b_pool_ref, pool_scale_ref, w_out_ref, g_ref, b_ref, *rest):
    n_cast = (len(rest) - 9) // 2
    cast_in, o_ref, cast_out = rest[:n_cast], rest[n_cast], rest[n_cast + 1:2 * n_cast + 1]
    up_ref, ul_ref, ug_ref, zp_ref, zl_ref, pool_hist, conv_hist, h_carry = rest[2 * n_cast + 1:]
    T = x_ref.shape[1]
    lc = T // SUBLANES
    pitch = lc + SUBLANES
    s = pl.program_id(1)

    for src, dst in zip(cast_in, cast_out):
        dst[...] = src[...].astype(_BF16)

    @pl.when(s == 0)
    def _():
        pool_hist[...] = jnp.zeros_like(pool_hist)
        conv_hist[...] = jnp.zeros_like(conv_hist)
        h_carry[...] = jnp.zeros_like(h_carry)

    x = x_ref[0]
    xb = x.astype(_BF16)
    n_pool, n_lru = POOL_WIDTH // LANES, LRU_WIDTH // LANES
    _store_chunked(up_ref, 0, _dot(xb, w_in_ref[:, :POOL_WIDTH]), lc, pitch)
    up = _load_interleaved(up_ref, 0, n_pool, lc, pitch)
    _store_chunked(ul_ref, 0, _dot(xb, w_in_ref[:, POOL_WIDTH:POOL_WIDTH + LRU_WIDTH]), lc, pitch)

    kk = lax.broadcasted_iota(jnp.int32, (POOL_HIST, SUBLANES, LANES), 0)
    jj = lax.broadcasted_iota(jnp.int32, (POOL_HIST, SUBLANES, LANES), 1)
    t_head = s * T + jj * lc + kk

    ext = jnp.concatenate([_history_rows(up[lc - POOL_HIST:], pool_hist[...]), up], axis=0)
    pool_hist[...] = up[lc - POOL_HIST:]
    s2 = ext[1:] + ext[:-1]
    s2r = s2[:, :, POOL_GROUP:]
    s4 = s2r[2:] + s2r[:-2]
    s4r = s4[:, :, POOL_GROUP:]
    s8 = s4r[4:] + s4r[:-4]
    s8r = s8[:, :, POOL_GROUP:]
    s16 = s8r[8:] + s8r[:-8]
    sums = (s2[15:, :, :POOL_GROUP], s4[13:, :, :POOL_GROUP], s8[9:, :, :POOL_GROUP], s16[1:])
    y_pool = []
    for g, w in enumerate(POOL_WINDOWS):
        cs = slice(g * POOL_GROUP, (g + 1) * POOL_GROUP)
        inv_head = 1.0 / jnp.minimum(t_head + 1, w).astype(_F32)
        inv_head = jnp.concatenate([inv_head] * (POOL_GROUP // LANES), axis=-1)
        mean = jnp.concatenate([sums[g][:POOL_HIST] * inv_head, sums[g][POOL_HIST:] * (1.0 / w)], axis=0)
        mixed = mean - up[:, :, cs]
        yg = _dot(mixed.reshape(T, POOL_GROUP).astype(_BF16), w_pool_ref[g]).reshape(lc, SUBLANES, POOL_GROUP)
        y_pool.append((yg + b_pool_ref[:, cs]) * pool_scale_ref[:, cs])
    _store_interleaved(zp_ref, 0, jnp.concatenate(y_pool, axis=-1), pitch)

    z_pool = _load_chunked(zp_ref, 0, n_pool, lc, pitch).astype(_BF16)
    n_hist = CONV_WIDTH - 1
    sub = lax.broadcasted_iota(jnp.int32, (SUBLANES, LRU_HEAD_DIM), 0)
    first = t_head[0:1] == 0
    gate_cols = 2 * LRU_HEAD_DIM
    out_cols = D_MODEL // LRU_HEADS
    y_top = []
    for h in range(LRU_HEADS):
        hs = slice(h * LRU_HEAD_DIM, (h + 1) * LRU_HEAD_DIM)
        ul = _load_interleaved(ul_ref, h, 1, lc, pitch)
        ext2 = jnp.concatenate([_history_rows(ul[lc - n_hist:], conv_hist[:, :, hs]), ul], axis=0)
        conv_hist[:, :, hs] = ul[lc - n_hist:]
        xc = ext2[0:lc] * conv_w_ref[0:1, hs]
        for k in range(1, CONV_WIDTH):
            xc = xc + ext2[k:k + lc] * conv_w_ref[k:k + 1, hs]
        xc = xc + conv_b_ref[:, hs]

        pre = _dot(xc.reshape(T, LRU_HEAD_DIM).astype(_BF16), w_ax_ref[h])
        pre = pre.reshape(lc, SUBLANES, 2 * LRU_HEAD_DIM)
        r = jax.nn.sigmoid(pre[:, :, :LRU_HEAD_DIM] + b_a_ref[:, hs])
        i = jax.nn.sigmoid(pre[:, :, LRU_HEAD_DIM:] + b_x_ref[:, hs])

        if h % 2 == 0:
            c0 = POOL_WIDTH + LRU_WIDTH + h * LRU_HEAD_DIM
            _store_chunked(ug_ref, h, _dot(xb, w_in_ref[:, c0:c0 + gate_cols]), lc, pitch)
        else:
            c0 = (h // 2) * out_cols
            y_top.append(_dot(z_pool, w_out_ref[:POOL_WIDTH, c0:c0 + out_cols]))

        log_a = (-LRU_C * r) * jax.nn.softplus(-lam_ref[:, hs])
        a = jnp.exp(log_a)
        mult = jnp.sqrt(-jnp.tanh(log_a) * (a * a + 1.0))
        mult = jnp.concatenate([jnp.where(first, 1.0, mult[0:1]), mult[1:]], axis=0)
        bv = mult * (i * xc)

        hl, ac = [bv[0]], [a[0]]
        for k in range(1, lc):
            hl.append(a[k] * hl[-1] + bv[k])
            ac.append(a[k] * ac[-1])
        c_a, c_b = ac[-1], hl[-1]
        for d in (1, 2, 4):
            a_s = jnp.where(sub < d, 1.0, pltpu.roll(c_a, d, 0))
            b_s = jnp.where(sub < d, 0.0, pltpu.roll(c_b, d, 0))
            c_b = c_a * b_s + c_b
            c_a = c_a * a_s
        h_prev = h_carry[:, hs]
        h_end = c_b + c_a * h_prev
        h_in = jnp.where(sub == 0, h_prev, pltpu.roll(h_end, 1, 0))
        h_carry[:, hs] = jnp.broadcast_to(h_end[SUBLANES - 1:SUBLANES, :], (SUBLANES, LRU_HEAD_DIM))
        hseq = jnp.stack(hl, axis=0) + jnp.stack(ac, axis=0) * h_in
        ug = _load_interleaved(ug_ref, h, 1, lc, pitch)
        _store_interleaved(zl_ref, h, hseq * jax.nn.gelu(ug), pitch)

    for q in range(LRU_HEADS // 2, LRU_HEADS):
        y_top.append(_dot(z_pool, w_out_ref[:POOL_WIDTH, q * out_cols:(q + 1) * out_cols]))
    z_lru = _load_chunked(zl_ref, 0, n_lru, lc, pitch).astype(_BF16)
    y = jnp.concatenate(y_top, axis=1) + _dot(z_lru, w_out_ref[POOL_WIDTH:, :])
    o_ref[0] = _layer_norm(DEEPNORM_ALPHA * x + y, g_ref[...], b_ref[...])


def _const_spec(shape, single_buffer=False):
    nd = len(shape)
    kwargs = {"pipeline_mode": pl.Buffered(1)} if single_buffer else {}
    return pl.BlockSpec(shape, lambda *_: (0,) * nd, **kwargs)


def _mixer(x, w_in, conv_w, conv_b, w_ax, b_a, b_x, lam, w_pool, b_pool, pool_scale, w_out, g, b,
           cast_weights):
    B, S, D = x.shape
    T = MIXER_TILE
    n_seq = S // T
    n_steps = B * n_seq
    assert T % SUBLANES == 0 and T // SUBLANES >= POOL_HIST and (T // SUBLANES) % (2 * SUBLANES) == 0
    rows = SUBLANES * (T // SUBLANES + SUBLANES)
    tile = pl.BlockSpec((1, T, D), lambda bi, si: (bi, si, 0))
    chunk_specs = []
    for w in cast_weights:
        assert w.shape[0] % n_steps == 0
        chunk_specs.append(pl.BlockSpec((w.shape[0] // n_steps, w.shape[1]),
                                        lambda bi, si: (bi * n_seq + si, 0)))
    return pl.pallas_call(
        _mixer_kernel,
        grid=(B, S // T),
        in_specs=[
            tile,
            _const_spec(w_in.shape, True),
            _const_spec(conv_w.shape), _const_spec(conv_b.shape),
            _const_spec(w_ax.shape, True),
            _const_spec(b_a.shape), _const_spec(b_x.shape), _const_spec(lam.shape),
            _const_spec(w_pool.shape, True),
            _const_spec(b_pool.shape), _const_spec(pool_scale.shape),
            _const_spec(w_out.shape, True),
            _const_spec(g.shape), _const_spec(b.shape),
        ] + chunk_specs,
        out_specs=[tile] + chunk_specs,
        out_shape=[jax.ShapeDtypeStruct(x.shape, _F32)]
        + [jax.ShapeDtypeStruct(w.shape, _BF16) for w in cast_weights],
        scratch_shapes=[
            pltpu.VMEM((POOL_WIDTH // LANES, rows, LANES), _F32),
            pltpu.VMEM((LRU_WIDTH // LANES, rows, LANES), _F32),
            pltpu.VMEM((LRU_WIDTH // LANES, rows, LANES), _F32),
            pltpu.VMEM((POOL_WIDTH // LANES, rows, LANES), _F32),
            pltpu.VMEM((LRU_WIDTH // LANES, rows, LANES), _F32),
            pltpu.VMEM((POOL_HIST, SUBLANES, POOL_WIDTH), _F32),
            pltpu.VMEM((CONV_WIDTH - 1, SUBLANES, LRU_WIDTH), _F32),
            pltpu.VMEM((SUBLANES, LRU_WIDTH), _F32),
        ],
        compiler_params=pltpu.CompilerParams(
            dimension_semantics=("arbitrary", "arbitrary"),
            vmem_limit_bytes=VMEM_LIMIT_BYTES),
        name="mixer",
    )(x, w_in, conv_w, conv_b, w_ax, b_a, b_x, lam, w_pool, b_pool, pool_scale, w_out, g, b, *cast_weights)


def _kv_kernel(mem_ref, wk_ref, wv_ref, k_ref, v_ref):
    m = mem_ref[...].astype(_BF16)
    k_ref[...] = _dot(m, wk_ref[...]).astype(_BF16)
    v_ref[...] = _dot(m, wv_ref[...]).astype(_BF16)


def _kv_proj(mem2d, w_k, w_v):
    M, D = mem2d.shape
    tn = KV_TILE_N
    wspec = pl.BlockSpec((D, tn), lambda j: (0, j))
    ospec = pl.BlockSpec((M, tn), lambda j: (0, j))
    return pl.pallas_call(
        _kv_kernel,
        grid=(D // tn,),
        in_specs=[_const_spec(mem2d.shape), wspec, wspec],
        out_specs=[ospec, ospec],
        out_shape=[jax.ShapeDtypeStruct((M, D), _BF16)] * 2,
        compiler_params=pltpu.CompilerParams(
            dimension_semantics=("arbitrary",), vmem_limit_bytes=VMEM_LIMIT_BYTES),
        name="kv_proj",
    )(mem2d, w_k, w_v)


def _xattn_kernel(x_ref, k_ref, v_ref, wq_ref, wo_ref, g_ref, b_ref, o_ref, resid_ref):
    i = pl.program_id(0)
    n_tiles = pl.num_programs(0) - 1
    slot = i % 2

    @pl.when(i == 0)
    def _():
        resid_ref[1] = jnp.zeros(resid_ref.shape[1:], _F32)

    T, D = x_ref.shape[1:]
    n_chunks = D // XATTN_ANCHOR_COLS
    chunk = T // n_chunks

    def finish_previous():
        zeros = []
        for c in range(n_chunks):
            rows = slice(c * chunk, (c + 1) * chunk)
            out = _layer_norm(resid_ref[1 - slot, rows, :], g_ref[...], b_ref[...])
            o_ref[0, rows, :] = out
            zeros.append(_ordering_zero(out))
        return zeros

    @pl.when(i < n_tiles)
    def _():
        zeros = finish_previous()
        x = x_ref[0]
        q = _dot(x.astype(_BF16), wq_ref[...]).astype(_BF16)
        scale = XATTN_HEAD_DIM ** -0.5
        o_parts = []
        for h in range(XATTN_HEADS):
            hs = slice(h * XATTN_HEAD_DIM, (h + 1) * XATTN_HEAD_DIM)
            sc = lax.dot_general(q[:, hs], k_ref[0, :, hs], (((1,), (1,)), ((), ())),
                                 preferred_element_type=_F32) * scale
            e = jnp.exp(sc - jnp.max(sc, axis=-1, keepdims=True))
            p = e / jnp.sum(e, axis=-1, keepdims=True)
            o_parts.append(_dot(p.astype(_BF16), v_ref[0, :, hs]))
        o = jnp.concatenate(o_parts, axis=1).astype(_BF16)
        resid = DEEPNORM_ALPHA * x + _dot(o, wo_ref[...])
        for c in range(n_chunks):
            c0 = c * XATTN_ANCHOR_COLS
            resid_ref[slot, :, c0:c0 + LANES] = resid[:, c0:c0 + LANES] + jnp.tile(zeros[c], (T // SUBLANES, 1))
            resid_ref[slot, :, c0 + LANES:c0 + XATTN_ANCHOR_COLS] = resid[:, c0 + LANES:c0 + XATTN_ANCHOR_COLS]

    @pl.when(i == n_tiles)
    def _():
        finish_previous()


def _xattn(x, k, v, w_q, w_o, g, b):
    B, S, D = x.shape
    T = XATTN_TILE
    n_seq = S // T
    n_tiles = B * n_seq

    def cur(i):
        t = jnp.minimum(i, n_tiles - 1)
        return t // n_seq, t % n_seq

    def prev(i):
        t = jnp.maximum(i - 1, 0)
        return t // n_seq, t % n_seq

    kvspec = pl.BlockSpec((1, N_MEM, D), lambda i: (cur(i)[0], 0, 0))
    return pl.pallas_call(
        _xattn_kernel,
        grid=(n_tiles + 1,),
        in_specs=[pl.BlockSpec((1, T, D), lambda i: (*cur(i), 0)), kvspec, kvspec,
                  _const_spec(w_q.shape, True), _const_spec(w_o.shape, True),
                  _const_spec(g.shape), _const_spec(b.shape)],
        out_specs=pl.BlockSpec((1, T, D), lambda i: (*prev(i), 0)),
        out_shape=jax.ShapeDtypeStruct(x.shape, _F32),
        scratch_shapes=[pltpu.VMEM((2, T, D), _F32)],
        compiler_params=pltpu.CompilerParams(
            dimension_semantics=("arbitrary",), vmem_limit_bytes=VMEM_LIMIT_BYTES),
        name="xattn",
    )(x, k, v, w_q, w_o, g, b)


def _mlp_kernel(x_ref, w1_ref, w2_ref, g_ref, b_ref, o_ref, xb_ref, acc_ref, *, n_f):
    i, j = pl.program_id(0), pl.program_id(1)
    n_tiles = pl.num_programs(0) - 1
    slot = i % 2
    rows = x_ref.shape[0] // n_f

    @pl.when((i == 0) & (j == 0))
    def _():
        acc_ref[1] = jnp.zeros(acc_ref.shape[1:], _F32)

    def finish_previous_rows():
        r0 = pl.multiple_of(j * rows, rows)
        o_ref[pl.ds(r0, rows), :] = _layer_norm(acc_ref[1 - slot, pl.ds(r0, rows), :], g_ref[...], b_ref[...])

    @pl.when(i < n_tiles)
    def _():
        @pl.when(j == 0)
        def _():
            xb_ref[...] = x_ref[...].astype(_BF16)
            acc_ref[slot] = DEEPNORM_ALPHA * x_ref[...]

        finish_previous_rows()
        h = jnp.square(jnp.maximum(_dot(xb_ref[...], w1_ref[...]), 0.0))
        acc_ref[slot] += _dot(h.astype(_BF16), w2_ref[...])

    @pl.when(i == n_tiles)
    def _():
        finish_previous_rows()


def _mlp(x2d, w1, w2, g, b):
    M, D = x2d.shape
    F = w1.shape[1]
    tm, tf = MLP_TILE_M, MLP_TILE_F
    n_tiles, n_f = M // tm, F // tf
    assert tm % (n_f * SUBLANES) == 0

    def wblk(i, j):
        return jnp.where(i == n_tiles, n_f - 1, j)

    return pl.pallas_call(
        functools.partial(_mlp_kernel, n_f=n_f),
        grid=(n_tiles + 1, n_f),
        in_specs=[pl.BlockSpec((tm, D), lambda i, j: (jnp.minimum(i, n_tiles - 1), 0)),
                  pl.BlockSpec((D, tf), lambda i, j: (0, wblk(i, j))),
                  pl.BlockSpec((tf, D), lambda i, j: (wblk(i, j), 0)),
                  _const_spec(g.shape), _const_spec(b.shape)],
        out_specs=pl.BlockSpec((tm, D), lambda i, j: (jnp.maximum(i - 1, 0), 0)),
        out_shape=jax.ShapeDtypeStruct((M, D), _F32),
        scratch_shapes=[pltpu.VMEM((tm, D), _BF16), pltpu.VMEM((2, tm, D), _F32)],
        compiler_params=pltpu.CompilerParams(
            dimension_semantics=("arbitrary", "arbitrary"),
            vmem_limit_bytes=VMEM_LIMIT_BYTES),
        name="mlp",
    )(x2d, w1, w2, g, b)


def kernel(x, mem, w_in, conv_w, conv_b, w_a, b_a, w_x, b_x, lru_lambda, w_pool, b_pool, pool_scale,
           w_out, ln1_g, ln1_b, w_q, w_k, w_v, w_o, ln2_g, ln2_b, w_ff1, w_ff2, ln3_g, ln3_b):
    B, S, D = x.shape
    row = lambda p: p.reshape(1, -1)
    for l in range(DEPTH):
        w_ax = jnp.concatenate([w_a[l], w_x[l]], axis=-1).astype(_BF16)
        x, wq_b, wo_b, w1_b, w2_b = _mixer(
            x, w_in[l].astype(_BF16), conv_w[l], row(conv_b[l]), w_ax,
            row(b_a[l]), row(b_x[l]), row(lru_lambda[l]),
            w_pool[l].astype(_BF16), row(b_pool[l]), row(pool_scale[l]),
            w_out[l].astype(_BF16), row(ln1_g[l]), row(ln1_b[l]),
            (w_q[l], w_o[l], w_ff1[l], w_ff2[l]))
        k, v = _kv_proj(mem.reshape(B * N_MEM, D), w_k[l].astype(_BF16), w_v[l].astype(_BF16))
        x = _xattn(x, k.reshape(B, N_MEM, D), v.reshape(B, N_MEM, D),
                   wq_b, wo_b, row(ln2_g[l]), row(ln2_b[l]))
        x = _mlp(x.reshape(B * S, D), w1_b, w2_b, row(ln3_g[l]), row(ln3_b[l])).reshape(B, S, D)
    return x
```

```python
import functools

import jax
import jax.numpy as jnp
from jax import lax
from jax.experimental import pallas as pl
from jax.experimental.pallas import tpu as pltpu

D_MODEL = 2048
POOL_WIDTH = 1024
LRU_WIDTH = 1024
POOL_WINDOWS = (2, 4, 8, 16)
POOL_GROUP = 256
LRU_HEADS = 8
LRU_HEAD_DIM = 128
CONV_WIDTH = 4
LRU_C = 8.0
N_MEM = 256
XATTN_HEADS = 4
XATTN_HEAD_DIM = 512
D_FF = 4 * D_MODEL
LN_EPS = 1e-5
DEPTH = 1
DEEPNORM_ALPHA = (2.0 * DEPTH) ** 0.25

POOL_HIST = 16
SUBLANES = 8
LANES = 128

MIXER_TILE = 256
XATTN_TILE = 512
ANCHOR_COLS = 256
MLP_TILE_M = 512
MLP_TILE_F = 1024
KV_TILE_N = 256
VMEM_LIMIT_BYTES = 56 * 1024 * 1024

_F32 = jnp.float32
_BF16 = jnp.bfloat16


def _layer_norm(v, g, b):
    mu = jnp.mean(v, axis=-1, keepdims=True)
    c = v - mu
    var = jnp.mean(jnp.square(c), axis=-1, keepdims=True)
    return c * lax.rsqrt(var + LN_EPS) * g + b


def _dot(a, b):
    return jnp.dot(a, b, preferred_element_type=_F32)


def _ordering_zero(v):
    rows, cols = v.shape
    t = jnp.sum(v.reshape(rows // SUBLANES, SUBLANES, cols), axis=0)
    t = functools.reduce(lambda a, b: a + b, [t[:, c:c + LANES] for c in range(0, cols, LANES)])
    return jnp.minimum(jnp.abs(t), 0.0)


def _normalize_previous(resid_ref, slot, g_ref, b_ref, o_ref):
    T, D = resid_ref.shape[1:]
    chunk = T // (D // ANCHOR_COLS)
    zeros = []
    for r0 in range(0, T, chunk):
        out = _layer_norm(resid_ref[1 - slot, r0:r0 + chunk, :], g_ref[...], b_ref[...])
        o_ref[0, r0:r0 + chunk, :] = out
        zeros.append(_ordering_zero(out))
    return zeros


def _store_anchored(resid_ref, slot, resid, zeros):
    T = resid.shape[0]
    for c, zero in enumerate(zeros):
        c0 = c * ANCHOR_COLS
        resid_ref[slot, :, c0:c0 + LANES] = resid[:, c0:c0 + LANES] + jnp.tile(zero, (T // SUBLANES, 1))
        resid_ref[slot, :, c0 + LANES:c0 + ANCHOR_COLS] = resid[:, c0 + LANES:c0 + ANCHOR_COLS]


def _store_chunked(dst_ref, slab0, val, lc, pitch):
    for c in range(val.shape[1] // LANES):
        for j in range(SUBLANES):
            dst_ref[slab0 + c, j * pitch:j * pitch + lc, :] = (
                val[j * lc:(j + 1) * lc, c * LANES:(c + 1) * LANES])


def _load_chunked(src_ref, slab0, n_slabs, lc, pitch):
    return jnp.concatenate(
        [jnp.concatenate([src_ref[slab0 + c, j * pitch:j * pitch + lc, :] for j in range(SUBLANES)], axis=0)
         for c in range(n_slabs)], axis=-1)


def _load_interleaved(src_ref, slab0, n_slabs, lc, pitch):
    cols = [jnp.stack([src_ref[slab0 + c, pl.ds(k, SUBLANES, stride=pitch), :] for k in range(lc)], axis=0)
            for c in range(n_slabs)]
    return jnp.concatenate(cols, axis=-1)


def _store_interleaved(dst_ref, slab0, val3, pitch):
    for c in range(val3.shape[2] // LANES):
        for k in range(val3.shape[0]):
            dst_ref[slab0 + c, pl.ds(k, SUBLANES, stride=pitch), :] = val3[k, :, c * LANES:(c + 1) * LANES]


def _history_rows(cur_tail, prev_tail):
    sub = lax.broadcasted_iota(jnp.int32, (1,) + cur_tail.shape[1:], 1)
    return jnp.where(sub == 0, pltpu.roll(prev_tail, 1, 1), pltpu.roll(cur_tail, 1, 1))


def _mixer_kernel(x_ref, w_in_ref, conv_w_ref, conv_b_ref, w_ax_ref, b_a_ref, b_x_ref, lam_ref,
                  w_pool_ref, b_pool_ref, pool_scale_ref, w_out_ref, g_ref, b_ref, *rest, n_seq):
    n_cast = (len(rest) - 10) // 2
    cast_in, o_ref, cast_out = rest[:n_cast], rest[n_cast], rest[n_cast + 1:2 * n_cast + 1]
    up_ref, ul_ref, ug_ref, zp_ref, zl_ref, pool_hist, conv_hist, h_carry, resid_ref = rest[2 * n_cast + 1:]
    T = x_ref.shape[1]
    lc = T // SUBLANES
    pitch = lc + SUBLANES
    i = pl.program_id(0)
    n_tiles = pl.num_programs(0) - 1
    s = jnp.minimum(i, n_tiles - 1) % n_seq
    slot = i % 2

    @pl.when(i == 0)
    def _():
        resid_ref[1] = jnp.zeros(resid_ref.shape[1:], _F32)

    def finish_previous():
        return _normalize_previous(resid_ref, slot, g_ref, b_ref, o_ref)

    @pl.when(i == n_tiles)
    def _():
        finish_previous()

    @pl.when(i < n_tiles)
    def _():
        _mixer_tile(s, slot, finish_previous, x_ref, w_in_ref, conv_w_ref, conv_b_ref, w_ax_ref, b_a_ref, b_x_ref,
                    lam_ref, w_pool_ref, b_pool_ref, pool_scale_ref, w_out_ref, cast_in, cast_out,
                    up_ref, ul_ref, ug_ref, zp_ref, zl_ref, pool_hist, conv_hist, h_carry, resid_ref, lc, pitch)


def _mixer_tile(s, slot, finish_previous, x_ref, w_in_ref, conv_w_ref, conv_b_ref, w_ax_ref, b_a_ref, b_x_ref,
                lam_ref, w_pool_ref, b_pool_ref, pool_scale_ref, w_out_ref, cast_in, cast_out,
                up_ref, ul_ref, ug_ref, zp_ref, zl_ref, pool_hist, conv_hist, h_carry, resid_ref, lc, pitch):
    T = x_ref.shape[1]

    @pl.when(s == 0)
    def _():
        pool_hist[...] = jnp.zeros_like(pool_hist)
        conv_hist[...] = jnp.zeros_like(conv_hist)
        h_carry[...] = jnp.zeros_like(h_carry)

    zeros = finish_previous()
    for src, dst in zip(cast_in, cast_out):
        dst[...] = src[...].astype(_BF16)

    x = x_ref[0]
    xb = x.astype(_BF16)
    n_pool, n_lru = POOL_WIDTH // LANES, LRU_WIDTH // LANES
    _store_chunked(up_ref, 0, _dot(xb, w_in_ref[:, :POOL_WIDTH]), lc, pitch)
    up = _load_interleaved(up_ref, 0, n_pool, lc, pitch)
    _store_chunked(ul_ref, 0, _dot(xb, w_in_ref[:, POOL_WIDTH:POOL_WIDTH + LRU_WIDTH]), lc, pitch)

    kk = lax.broadcasted_iota(jnp.int32, (POOL_HIST, SUBLANES, LANES), 0)
    jj = lax.broadcasted_iota(jnp.int32, (POOL_HIST, SUBLANES, LANES), 1)
    t_head = s * T + jj * lc + kk

    ext = jnp.concatenate([_history_rows(up[lc - POOL_HIST:], pool_hist[...]), up], axis=0)
    pool_hist[...] = up[lc - POOL_HIST:]
    s2 = ext[1:] + ext[:-1]
    s2r = s2[:, :, POOL_GROUP:]
    s4 = s2r[2:] + s2r[:-2]
    s4r = s4[:, :, POOL_GROUP:]
    s8 = s4r[4:] + s4r[:-4]
    s8r = s8[:, :, POOL_GROUP:]
    s16 = s8r[8:] + s8r[:-8]
    sums = (s2[15:, :, :POOL_GROUP], s4[13:, :, :POOL_GROUP], s8[9:, :, :POOL_GROUP], s16[1:])
    y_pool = []
    for g, w in enumerate(POOL_WINDOWS):
        cs = slice(g * POOL_GROUP, (g + 1) * POOL_GROUP)
        inv_head = 1.0 / jnp.minimum(t_head + 1, w).astype(_F32)
        inv_head = jnp.concatenate([inv_head] * (POOL_GROUP // LANES), axis=-1)
        mean = jnp.concatenate([sums[g][:POOL_HIST] * inv_head, sums[g][POOL_HIST:] * (1.0 / w)], axis=0)
        mixed = mean - up[:, :, cs]
        yg = _dot(mixed.reshape(T, POOL_GROUP).astype(_BF16), w_pool_ref[g]).reshape(lc, SUBLANES, POOL_GROUP)
        y_pool.append((yg + b_pool_ref[:, cs]) * pool_scale_ref[:, cs])
    _store_interleaved(zp_ref, 0, jnp.concatenate(y_pool, axis=-1), pitch)

    z_pool = _load_chunked(zp_ref, 0, n_pool, lc, pitch).astype(_BF16)
    n_hist = CONV_WIDTH - 1
    sub = lax.broadcasted_iota(jnp.int32, (SUBLANES, LRU_HEAD_DIM), 0)
    first = t_head[0:1] == 0
    gate_cols = 2 * LRU_HEAD_DIM
    out_cols = D_MODEL // LRU_HEADS
    y_top = []
    for h in range(LRU_HEADS):
        hs = slice(h * LRU_HEAD_DIM, (h + 1) * LRU_HEAD_DIM)
        ul = _load_interleaved(ul_ref, h, 1, lc, pitch)
        ext2 = jnp.concatenate([_history_rows(ul[lc - n_hist:], conv_hist[:, :, hs]), ul], axis=0)
        conv_hist[:, :, hs] = ul[lc - n_hist:]
        xc = ext2[0:lc] * conv_w_ref[0:1, hs]
        for k in range(1, CONV_WIDTH):
            xc = xc + ext2[k:k + lc] * conv_w_ref[k:k + 1, hs]
        xc = xc + conv_b_ref[:, hs]

        pre = _dot(xc.reshape(T, LRU_HEAD_DIM).astype(_BF16), w_ax_ref[h])
        pre = pre.reshape(lc, SUBLANES, 2 * LRU_HEAD_DIM)
        r = jax.nn.sigmoid(pre[:, :, :LRU_HEAD_DIM] + b_a_ref[:, hs])
        i = jax.nn.sigmoid(pre[:, :, LRU_HEAD_DIM:] + b_x_ref[:, hs])

        if h % 2 == 0:
            c0 = POOL_WIDTH + LRU_WIDTH + h * LRU_HEAD_DIM
            _store_chunked(ug_ref, h, _dot(xb, w_in_ref[:, c0:c0 + gate_cols]), lc, pitch)
        else:
            c0 = (h // 2) * out_cols
            y_top.append(_dot(z_pool, w_out_ref[:POOL_WIDTH, c0:c0 + out_cols]))

        log_a = (-LRU_C * r) * jax.nn.softplus(-lam_ref[:, hs])
        a = jnp.exp(log_a)
        mult = jnp.sqrt(-jnp.tanh(log_a) * (a * a + 1.0))
        mult = jnp.concatenate([jnp.where(first, 1.0, mult[0:1]), mult[1:]], axis=0)
        bv = mult * (i * xc)

        hl, ac = [bv[0]], [a[0]]
        for k in range(1, lc):
            hl.append(a[k] * hl[-1] + bv[k])
            ac.append(a[k] * ac[-1])
        c_a, c_b = ac[-1], hl[-1]
        for d in (1, 2, 4):
            a_s = jnp.where(sub < d, 1.0, pltpu.roll(c_a, d, 0))
            b_s = jnp.where(sub < d, 0.0, pltpu.roll(c_b, d, 0))
            c_b = c_a * b_s + c_b
            c_a = c_a * a_s
        h_prev = h_carry[:, hs]
        h_end = c_b + c_a * h_prev
        h_in = jnp.where(sub == 0, h_prev, pltpu.roll(h_end, 1, 0))
        h_carry[:, hs] = jnp.broadcast_to(h_end[SUBLANES - 1:SUBLANES, :], (SUBLANES, LRU_HEAD_DIM))
        hseq = jnp.stack(hl, axis=0) + jnp.stack(ac, axis=0) * h_in
        ug = _load_interleaved(ug_ref, h, 1, lc, pitch)
        _store_interleaved(zl_ref, h, hseq * jax.nn.gelu(ug), pitch)

    for q in range(LRU_HEADS // 2, LRU_HEADS):
        y_top.append(_dot(z_pool, w_out_ref[:POOL_WIDTH, q * out_cols:(q + 1) * out_cols]))
    z_lru = _load_chunked(zl_ref, 0, n_lru, lc, pitch).astype(_BF16)
    y = jnp.concatenate(y_top, axis=1) + _dot(z_lru, w_out_ref[POOL_WIDTH:, :])
    _store_anchored(resid_ref, slot, DEEPNORM_ALPHA * x + y, zeros)


def _const_spec(shape, single_buffer=False):
    nd = len(shape)
    kwargs = {"pipeline_mode": pl.Buffered(1)} if single_buffer else {}
    return pl.BlockSpec(shape, lambda *_: (0,) * nd, **kwargs)


def _mixer(x, w_in, conv_w, conv_b, w_ax, b_a, b_x, lam, w_pool, b_pool, pool_scale, w_out, g, b,
           cast_weights):
    B, S, D = x.shape
    T = MIXER_TILE
    n_seq = S // T
    n_steps = B * n_seq
    assert T % SUBLANES == 0 and T // SUBLANES >= POOL_HIST and (T // SUBLANES) % (2 * SUBLANES) == 0
    rows = SUBLANES * (T // SUBLANES + SUBLANES)

    def cur(i):
        t = jnp.minimum(i, n_steps - 1)
        return t // n_seq, t % n_seq

    def prev(i):
        t = jnp.maximum(i - 1, 0)
        return t // n_seq, t % n_seq

    tile = pl.BlockSpec((1, T, D), lambda i: (*cur(i), 0))
    chunk_specs = []
    for w in cast_weights:
        assert w.shape[0] % n_steps == 0
        chunk_specs.append(pl.BlockSpec((w.shape[0] // n_steps, w.shape[1]),
                                        lambda i: (jnp.minimum(i, n_steps - 1), 0)))
    return pl.pallas_call(
        functools.partial(_mixer_kernel, n_seq=n_seq),
        grid=(n_steps + 1,),
        in_specs=[
            tile,
            _const_spec(w_in.shape, True),
            _const_spec(conv_w.shape), _const_spec(conv_b.shape),
            _const_spec(w_ax.shape, True),
            _const_spec(b_a.shape), _const_spec(b_x.shape), _const_spec(lam.shape),
            _const_spec(w_pool.shape, True),
            _const_spec(b_pool.shape), _const_spec(pool_scale.shape),
            _const_spec(w_out.shape, True),
            _const_spec(g.shape), _const_spec(b.shape),
        ] + chunk_specs,
        out_specs=[pl.BlockSpec((1, T, D), lambda i: (*prev(i), 0))] + chunk_specs,
        out_shape=[jax.ShapeDtypeStruct(x.shape, _F32)]
        + [jax.ShapeDtypeStruct(w.shape, _BF16) for w in cast_weights],
        scratch_shapes=[
            pltpu.VMEM((POOL_WIDTH // LANES, rows, LANES), _F32),
            pltpu.VMEM((LRU_WIDTH // LANES, rows, LANES), _F32),
            pltpu.VMEM((LRU_WIDTH // LANES, rows, LANES), _F32),
            pltpu.VMEM((POOL_WIDTH // LANES, rows, LANES), _F32),
            pltpu.VMEM((LRU_WIDTH // LANES, rows, LANES), _F32),
            pltpu.VMEM((POOL_HIST, SUBLANES, POOL_WIDTH), _F32),
            pltpu.VMEM((CONV_WIDTH - 1, SUBLANES, LRU_WIDTH), _F32),
            pltpu.VMEM((SUBLANES, LRU_WIDTH), _F32),
            pltpu.VMEM((2, T, D), _F32),
        ],
        compiler_params=pltpu.CompilerParams(
            dimension_semantics=("arbitrary",), vmem_limit_bytes=VMEM_LIMIT_BYTES),
        name="mixer",
    )(x, w_in, conv_w, conv_b, w_ax, b_a, b_x, lam, w_pool, b_pool, pool_scale, w_out, g, b, *cast_weights)


def _kv_kernel(mem_ref, wk_ref, wv_ref, *rest):
    n_cast = (len(rest) - 2) // 2
    cast_in, k_ref, v_ref, cast_out = rest[:n_cast], rest[n_cast], rest[n_cast + 1], rest[n_cast + 2:]
    for src, dst in zip(cast_in, cast_out):
        dst[...] = src[...].astype(_BF16)
    m = mem_ref[...].astype(_BF16)
    k_ref[...] = _dot(m, wk_ref[...].astype(_BF16)).astype(_BF16)
    v_ref[...] = _dot(m, wv_ref[...].astype(_BF16)).astype(_BF16)


def _kv_proj(mem2d, w_k, w_v, cast_weights):
    M, D = mem2d.shape
    tn = KV_TILE_N
    n_steps = D // tn
    wspec = pl.BlockSpec((D, tn), lambda j: (0, j))
    ospec = pl.BlockSpec((M, tn), lambda j: (0, j))
    chunk_specs = []
    for w in cast_weights:
        assert w.shape[0] % n_steps == 0
        chunk_specs.append(pl.BlockSpec((w.shape[0] // n_steps, w.shape[1]), lambda j: (j, 0)))
    return pl.pallas_call(
        _kv_kernel,
        grid=(n_steps,),
        in_specs=[_const_spec(mem2d.shape, True), wspec, wspec] + chunk_specs,
        out_specs=[ospec, ospec] + chunk_specs,
        out_shape=[jax.ShapeDtypeStruct((M, D), _BF16)] * 2
        + [jax.ShapeDtypeStruct(w.shape, _BF16) for w in cast_weights],
        compiler_params=pltpu.CompilerParams(
            dimension_semantics=("arbitrary",), vmem_limit_bytes=VMEM_LIMIT_BYTES),
        name="kv_proj",
    )(mem2d, w_k, w_v, *cast_weights)


def _xattn_kernel(x_ref, k_ref, v_ref, wq_ref, wo_ref, g_ref, b_ref, o_ref, resid_ref):
    i = pl.program_id(0)
    n_tiles = pl.num_programs(0) - 1
    slot = i % 2

    @pl.when(i == 0)
    def _():
        resid_ref[1] = jnp.zeros(resid_ref.shape[1:], _F32)

    def finish_previous():
        return _normalize_previous(resid_ref, slot, g_ref, b_ref, o_ref)

    @pl.when(i < n_tiles)
    def _():
        zeros = finish_previous()
        x = x_ref[0]
        q = _dot(x.astype(_BF16), wq_ref[...]).astype(_BF16)
        scale = XATTN_HEAD_DIM ** -0.5
        o_parts = []
        for h in range(XATTN_HEADS):
            hs = slice(h * XATTN_HEAD_DIM, (h + 1) * XATTN_HEAD_DIM)
            sc = lax.dot_general(q[:, hs], k_ref[0, :, hs], (((1,), (1,)), ((), ())),
                                 preferred_element_type=_F32) * scale
            e = jnp.exp(sc - jnp.max(sc, axis=-1, keepdims=True))
            p = e / jnp.sum(e, axis=-1, keepdims=True)
            o_parts.append(_dot(p.astype(_BF16), v_ref[0, :, hs]))
        o = jnp.concatenate(o_parts, axis=1).astype(_BF16)
        resid = DEEPNORM_ALPHA * x + _dot(o, wo_ref[...])
        _store_anchored(resid_ref, slot, resid, zeros)

    @pl.when(i == n_tiles)
    def _():
        finish_previous()


def _xattn(x, k, v, w_q, w_o, g, b):
    B, S, D = x.shape
    T = XATTN_TILE
    n_seq = S // T
    n_tiles = B * n_seq

    def cur(i):
        t = jnp.minimum(i, n_tiles - 1)
        return t // n_seq, t % n_seq

    def prev(i):
        t = jnp.maximum(i - 1, 0)
        return t // n_seq, t % n_seq

    kvspec = pl.BlockSpec((1, N_MEM, D), lambda i: (cur(i)[0], 0, 0))
    return pl.pallas_call(
        _xattn_kernel,
        grid=(n_tiles + 1,),
        in_specs=[pl.BlockSpec((1, T, D), lambda i: (*cur(i), 0)), kvspec, kvspec,
                  _const_spec(w_q.shape, True), _const_spec(w_o.shape, True),
                  _const_spec(g.shape), _const_spec(b.shape)],
        out_specs=pl.BlockSpec((1, T, D), lambda i: (*prev(i), 0)),
        out_shape=jax.ShapeDtypeStruct(x.shape, _F32),
        scratch_shapes=[pltpu.VMEM((2, T, D), _F32)],
        compiler_params=pltpu.CompilerParams(
            dimension_semantics=("arbitrary",), vmem_limit_bytes=VMEM_LIMIT_BYTES),
        name="xattn",
    )(x, k, v, w_q, w_o, g, b)


def _mlp_kernel(x_ref, w1_ref, w2_ref, g_ref, b_ref, o_ref, xb_ref, acc_ref, *, n_f):
    i, j = pl.program_id(0), pl.program_id(1)
    n_tiles = pl.num_programs(0) - 1
    slot = i % 2
    rows = x_ref.shape[0] // n_f

    @pl.when((i == 0) & (j == 0))
    def _():
        acc_ref[1] = jnp.zeros(acc_ref.shape[1:], _F32)

    def finish_previous_rows():
        r0 = pl.multiple_of(j * rows, rows)
        o_ref[pl.ds(r0, rows), :] = _layer_norm(acc_ref[1 - slot, pl.ds(r0, rows), :], g_ref[...], b_ref[...])

    @pl.when(i < n_tiles)
    def _():
        @pl.when(j == 0)
        def _():
            xb_ref[...] = x_ref[...].astype(_BF16)
            acc_ref[slot] = DEEPNORM_ALPHA * x_ref[...]

        finish_previous_rows()
        h = jnp.square(jnp.maximum(_dot(xb_ref[...], w1_ref[...]), 0.0))
        acc_ref[slot] += _dot(h.astype(_BF16), w2_ref[...])

    @pl.when(i == n_tiles)
    def _():
        finish_previous_rows()


def _mlp(x2d, w1, w2, g, b):
    M, D = x2d.shape
    F = w1.shape[1]
    tm, tf = MLP_TILE_M, MLP_TILE_F
    n_tiles, n_f = M // tm, F // tf
    assert tm % (n_f * SUBLANES) == 0

    def wblk(i, j):
        return jnp.where(i == n_tiles, n_f - 1, j)

    return pl.pallas_call(
        functools.partial(_mlp_kernel, n_f=n_f),
        grid=(n_tiles + 1, n_f),
        in_specs=[pl.BlockSpec((tm, D), lambda i, j: (jnp.minimum(i, n_tiles - 1), 0)),
                  pl.BlockSpec((D, tf), lambda i, j: (0, wblk(i, j))),
                  pl.BlockSpec((tf, D), lambda i, j: (wblk(i, j), 0)),
                  _const_spec(g.shape), _const_spec(b.shape)],
        out_specs=pl.BlockSpec((tm, D), lambda i, j: (jnp.maximum(i - 1, 0), 0)),
        out_shape=jax.ShapeDtypeStruct((M, D), _F32),
        scratch_shapes=[pltpu.VMEM((tm, D), _BF16), pltpu.VMEM((2, tm, D), _F32)],
        compiler_params=pltpu.CompilerParams(
            dimension_semantics=("arbitrary", "arbitrary"),
            vmem_limit_bytes=VMEM_LIMIT_BYTES),
        name="mlp",
    )(x2d, w1, w2, g, b)


def kernel(x, mem, w_in, conv_w, conv_b, w_a, b_a, w_x, b_x, lru_lambda, w_pool, b_pool, pool_scale,
           w_out, ln1_g, ln1_b, w_q, w_k, w_v, w_o, ln2_g, ln2_b, w_ff1, w_ff2, ln3_g, ln3_b):
    B, S, D = x.shape
    row = lambda p: p.reshape(1, -1)
    for l in range(DEPTH):
        w_ax = jnp.concatenate([w_a[l], w_x[l]], axis=-1).astype(_BF16)
        k, v, w_in_b, w_out_b = _kv_proj(mem.reshape(B * N_MEM, D), w_k[l], w_v[l], (w_in[l], w_out[l]))
        x, wq_b, wo_b, w1_b, w2_b = _mixer(
            x, w_in_b, conv_w[l], row(conv_b[l]), w_ax,
            row(b_a[l]), row(b_x[l]), row(lru_lambda[l]),
            w_pool[l].astype(_BF16), row(b_pool[l]), row(pool_scale[l]),
            w_out_b, row(ln1_g[l]), row(ln1_b[l]),
            (w_q[l], w_o[l], w_ff1[l], w_ff2[l]))
        x = _xattn(x, k.reshape(B, N_MEM, D), v.reshape(B, N_MEM, D),
                   wq_b, wo_b, row(ln2_g[l]), row(ln2_b[l]))
        x = _mlp(x.reshape(B * S, D), w1_b, w2_b, row(ln3_g[l]), row(ln3_b[l])).reshape(B, S, D)
    return x
```

```python
import functools

import jax
import jax.numpy as jnp
from jax import lax
from jax.experimental import pallas as pl
from jax.experimental.pallas import tpu as pltpu

D_MODEL = 2048
POOL_WIDTH = 1024
LRU_WIDTH = 1024
POOL_WINDOWS = (2, 4, 8, 16)
POOL_GROUP = 256
LRU_HEADS = 8
LRU_HEAD_DIM = 128
CONV_WIDTH = 4
LRU_C = 8.0
N_MEM = 256
XATTN_HEADS = 4
XATTN_HEAD_DIM = 512
D_FF = 4 * D_MODEL
LN_EPS = 1e-5
DEPTH = 1
DEEPNORM_ALPHA = (2.0 * DEPTH) ** 0.25

POOL_HIST = 16
SUBLANES = 8
LANES = 128

MIXER_TILE = 256
XATTN_TILE = 512
ANCHOR_COLS = 256
MLP_TILE_M = 512
MLP_TILE_F = 1024
KV_TILE_N = 256
VMEM_LIMIT_BYTES = 56 * 1024 * 1024

_F32 = jnp.float32
_BF16 = jnp.bfloat16


def _layer_norm(v, g, b):
    mu = jnp.mean(v, axis=-1, keepdims=True)
    c = v - mu
    var = jnp.mean(jnp.square(c), axis=-1, keepdims=True)
    return c * lax.rsqrt(var + LN_EPS) * g + b


def _dot(a, b):
    return jnp.dot(a, b, preferred_element_type=_F32)


def _pack_rows(v):
    return pltpu.bitcast(v.astype(_BF16), jnp.uint32)


def _unpack_rows(w):
    return pltpu.bitcast(w, _BF16)


def _ordering_zero(v):
    rows, cols = v.shape
    t = jnp.sum(v.reshape(rows // SUBLANES, SUBLANES, cols), axis=0)
    t = functools.reduce(lambda a, b: a + b, [t[:, c:c + LANES] for c in range(0, cols, LANES)])
    return jnp.minimum(jnp.abs(t), 0.0)


def _normalize_previous(resid_ref, slot, g_ref, b_ref, o_ref):
    T, D = resid_ref.shape[1:]
    chunk = T // (D // ANCHOR_COLS)
    zeros = []
    for r0 in range(0, T, chunk):
        out = _layer_norm(resid_ref[1 - slot, r0:r0 + chunk, :], g_ref[...], b_ref[...])
        o_ref[0, r0:r0 + chunk, :] = out
        zeros.append(_ordering_zero(out))
    return zeros


def _store_anchored(resid_ref, slot, resid, zeros):
    T = resid.shape[0]
    for c, zero in enumerate(zeros):
        c0 = c * ANCHOR_COLS
        resid_ref[slot, :, c0:c0 + LANES] = resid[:, c0:c0 + LANES] + jnp.tile(zero, (T // SUBLANES, 1))
        resid_ref[slot, :, c0 + LANES:c0 + ANCHOR_COLS] = resid[:, c0 + LANES:c0 + ANCHOR_COLS]


def _store_chunked(dst_ref, slab0, val, lc, pitch):
    for c in range(val.shape[1] // LANES):
        for j in range(SUBLANES):
            dst_ref[slab0 + c, j * pitch:j * pitch + lc, :] = (
                val[j * lc:(j + 1) * lc, c * LANES:(c + 1) * LANES])


def _load_chunked(src_ref, slab0, n_slabs, lc, pitch):
    return jnp.concatenate(
        [jnp.concatenate([src_ref[slab0 + c, j * pitch:j * pitch + lc, :] for j in range(SUBLANES)], axis=0)
         for c in range(n_slabs)], axis=-1)


def _load_interleaved(src_ref, slab0, n_slabs, lc, pitch):
    cols = [jnp.stack([src_ref[slab0 + c, pl.ds(k, SUBLANES, stride=pitch), :] for k in range(lc)], axis=0)
            for c in range(n_slabs)]
    return jnp.concatenate(cols, axis=-1)


def _store_interleaved(dst_ref, slab0, val3, pitch):
    for c in range(val3.shape[2] // LANES):
        for k in range(val3.shape[0]):
            dst_ref[slab0 + c, pl.ds(k, SUBLANES, stride=pitch), :] = val3[k, :, c * LANES:(c + 1) * LANES]


def _history_rows(cur_tail, prev_tail):
    sub = lax.broadcasted_iota(jnp.int32, (1,) + cur_tail.shape[1:], 1)
    return jnp.where(sub == 0, pltpu.roll(prev_tail, 1, 1), pltpu.roll(cur_tail, 1, 1))


def _mixer_kernel(x_ref, w_in_ref, conv_w_ref, conv_b_ref, w_ax_ref, b_a_ref, b_x_ref, lam_ref,
                  w_pool_ref, b_pool_ref, pool_scale_ref, w_out_ref, g_ref, b_ref, *rest, n_seq):
    n_cast = (len(rest) - 10) // 2
    cast_in, o_ref, cast_out = rest[:n_cast], rest[n_cast], rest[n_cast + 1:2 * n_cast + 1]
    up_ref, ul_ref, ug_ref, zp_ref, zl_ref, pool_hist, conv_hist, h_carry, resid_ref = rest[2 * n_cast + 1:]
    T = x_ref.shape[1]
    lc = T // SUBLANES
    pitch = lc + SUBLANES
    i = pl.program_id(0)
    n_tiles = pl.num_programs(0) - 1
    s = jnp.minimum(i, n_tiles - 1) % n_seq
    slot = i % 2

    @pl.when(i == 0)
    def _():
        resid_ref[1] = jnp.zeros(resid_ref.shape[1:], _F32)

    def finish_previous():
        return _normalize_previous(resid_ref, slot, g_ref, b_ref, o_ref)

    @pl.when(i == n_tiles)
    def _():
        finish_previous()

    @pl.when(i < n_tiles)
    def _():
        _mixer_tile(s, slot, finish_previous, x_ref, w_in_ref, conv_w_ref, conv_b_ref, w_ax_ref, b_a_ref, b_x_ref,
                    lam_ref, w_pool_ref, b_pool_ref, pool_scale_ref, w_out_ref, cast_in, cast_out,
                    up_ref, ul_ref, ug_ref, zp_ref, zl_ref, pool_hist, conv_hist, h_carry, resid_ref, lc, pitch)


def _mixer_tile(s, slot, finish_previous, x_ref, w_in_ref, conv_w_ref, conv_b_ref, w_ax_ref, b_a_ref, b_x_ref,
                lam_ref, w_pool_ref, b_pool_ref, pool_scale_ref, w_out_ref, cast_in, cast_out,
                up_ref, ul_ref, ug_ref, zp_ref, zl_ref, pool_hist, conv_hist, h_carry, resid_ref, lc, pitch):
    T = x_ref.shape[1]

    @pl.when(s == 0)
    def _():
        pool_hist[...] = jnp.zeros_like(pool_hist)
        conv_hist[...] = jnp.zeros_like(conv_hist)
        h_carry[...] = jnp.zeros_like(h_carry)

    zeros = finish_previous()
    for src, dst in zip(cast_in, cast_out):
        dst[...] = _pack_rows(src[...])

    x = x_ref[0]
    xb = x.astype(_BF16)
    n_pool, n_lru = POOL_WIDTH // LANES, LRU_WIDTH // LANES
    _store_chunked(up_ref, 0, _dot(xb, _unpack_rows(w_in_ref[:, :POOL_WIDTH])), lc, pitch)
    up = _load_interleaved(up_ref, 0, n_pool, lc, pitch)
    _store_chunked(ul_ref, 0, _dot(xb, _unpack_rows(w_in_ref[:, POOL_WIDTH:POOL_WIDTH + LRU_WIDTH])), lc, pitch)

    kk = lax.broadcasted_iota(jnp.int32, (POOL_HIST, SUBLANES, LANES), 0)
    jj = lax.broadcasted_iota(jnp.int32, (POOL_HIST, SUBLANES, LANES), 1)
    t_head = s * T + jj * lc + kk

    ext = jnp.concatenate([_history_rows(up[lc - POOL_HIST:], pool_hist[...]), up], axis=0)
    pool_hist[...] = up[lc - POOL_HIST:]
    s2 = ext[1:] + ext[:-1]
    s2r = s2[:, :, POOL_GROUP:]
    s4 = s2r[2:] + s2r[:-2]
    s4r = s4[:, :, POOL_GROUP:]
    s8 = s4r[4:] + s4r[:-4]
    s8r = s8[:, :, POOL_GROUP:]
    s16 = s8r[8:] + s8r[:-8]
    sums = (s2[15:, :, :POOL_GROUP], s4[13:, :, :POOL_GROUP], s8[9:, :, :POOL_GROUP], s16[1:])
    y_pool = []
    for g, w in enumerate(POOL_WINDOWS):
        cs = slice(g * POOL_GROUP, (g + 1) * POOL_GROUP)
        inv_head = 1.0 / jnp.minimum(t_head + 1, w).astype(_F32)
        inv_head = jnp.concatenate([inv_head] * (POOL_GROUP // LANES), axis=-1)
        mean = jnp.concatenate([sums[g][:POOL_HIST] * inv_head, sums[g][POOL_HIST:] * (1.0 / w)], axis=0)
        mixed = mean - up[:, :, cs]
        yg = _dot(mixed.reshape(T, POOL_GROUP).astype(_BF16), w_pool_ref[g]).reshape(lc, SUBLANES, POOL_GROUP)
        y_pool.append((yg + b_pool_ref[:, cs]) * pool_scale_ref[:, cs])
    _store_interleaved(zp_ref, 0, jnp.concatenate(y_pool, axis=-1), pitch)

    z_pool = _load_chunked(zp_ref, 0, n_pool, lc, pitch).astype(_BF16)
    n_hist = CONV_WIDTH - 1
    sub = lax.broadcasted_iota(jnp.int32, (SUBLANES, LRU_HEAD_DIM), 0)
    first = t_head[0:1] == 0
    gate_cols = 2 * LRU_HEAD_DIM
    out_cols = D_MODEL // LRU_HEADS
    y_top = []
    for h in range(LRU_HEADS):
        hs = slice(h * LRU_HEAD_DIM, (h + 1) * LRU_HEAD_DIM)
        ul = _load_interleaved(ul_ref, h, 1, lc, pitch)
        ext2 = jnp.concatenate([_history_rows(ul[lc - n_hist:], conv_hist[:, :, hs]), ul], axis=0)
        conv_hist[:, :, hs] = ul[lc - n_hist:]
        xc = ext2[0:lc] * conv_w_ref[0:1, hs]
        for k in range(1, CONV_WIDTH):
            xc = xc + ext2[k:k + lc] * conv_w_ref[k:k + 1, hs]
        xc = xc + conv_b_ref[:, hs]

        pre = _dot(xc.reshape(T, LRU_HEAD_DIM).astype(_BF16), w_ax_ref[h])
        pre = pre.reshape(lc, SUBLANES, 2 * LRU_HEAD_DIM)
        r = jax.nn.sigmoid(pre[:, :, :LRU_HEAD_DIM] + b_a_ref[:, hs])
        i = jax.nn.sigmoid(pre[:, :, LRU_HEAD_DIM:] + b_x_ref[:, hs])

        if h % 2 == 0:
            c0 = POOL_WIDTH + LRU_WIDTH + h * LRU_HEAD_DIM
            _store_chunked(ug_ref, h, _dot(xb, _unpack_rows(w_in_ref[:, c0:c0 + gate_cols])), lc, pitch)
        else:
            c0 = (h // 2) * out_cols
            y_top.append(_dot(z_pool, _unpack_rows(w_out_ref[:POOL_WIDTH // 2, c0:c0 + out_cols])))

        log_a = (-LRU_C * r) * jax.nn.softplus(-lam_ref[:, hs])
        a = jnp.exp(log_a)
        u = -jnp.tanh(log_a) * (a * a + 1.0)
        mult = jnp.where(u == 0.0, 0.0, u * lax.rsqrt(u))
        mult = jnp.concatenate([jnp.where(first, 1.0, mult[0:1]), mult[1:]], axis=0)
        bv = mult * (i * xc)

        hl, ac = [bv[0]], [a[0]]
        for k in range(1, lc):
            hl.append(a[k] * hl[-1] + bv[k])
            ac.append(a[k] * ac[-1])
        c_a, c_b = ac[-1], hl[-1]
        for d in (1, 2, 4):
            a_s = jnp.where(sub < d, 1.0, pltpu.roll(c_a, d, 0))
            b_s = jnp.where(sub < d, 0.0, pltpu.roll(c_b, d, 0))
            c_b = c_a * b_s + c_b
            c_a = c_a * a_s
        h_prev = h_carry[:, hs]
        h_end = c_b + c_a * h_prev
        h_in = jnp.where(sub == 0, h_prev, pltpu.roll(h_end, 1, 0))
        h_carry[:, hs] = jnp.broadcast_to(h_end[SUBLANES - 1:SUBLANES, :], (SUBLANES, LRU_HEAD_DIM))
        hseq = jnp.stack(hl, axis=0) + jnp.stack(ac, axis=0) * h_in
        ug = _load_interleaved(ug_ref, h, 1, lc, pitch)
        _store_interleaved(zl_ref, h, hseq * jax.nn.gelu(ug), pitch)

    for q in range(LRU_HEADS // 2, LRU_HEADS):
        y_top.append(_dot(z_pool, _unpack_rows(w_out_ref[:POOL_WIDTH // 2, q * out_cols:(q + 1) * out_cols])))
    z_lru = _load_chunked(zl_ref, 0, n_lru, lc, pitch).astype(_BF16)
    y = jnp.concatenate(y_top, axis=1) + _dot(z_lru, _unpack_rows(w_out_ref[POOL_WIDTH // 2:, :]))
    _store_anchored(resid_ref, slot, DEEPNORM_ALPHA * x + y, zeros)


def _const_spec(shape, single_buffer=False):
    nd = len(shape)
    kwargs = {"pipeline_mode": pl.Buffered(1)} if single_buffer else {}
    return pl.BlockSpec(shape, lambda *_: (0,) * nd, **kwargs)


def _mixer(x, w_in, conv_w, conv_b, w_ax, b_a, b_x, lam, w_pool, b_pool, pool_scale, w_out, g, b,
           cast_weights):
    B, S, D = x.shape
    T = MIXER_TILE
    n_seq = S // T
    n_steps = B * n_seq
    assert T % SUBLANES == 0 and T // SUBLANES >= POOL_HIST and (T // SUBLANES) % (2 * SUBLANES) == 0
    rows = SUBLANES * (T // SUBLANES + SUBLANES)

    def cur(i):
        t = jnp.minimum(i, n_steps - 1)
        return t // n_seq, t % n_seq

    def prev(i):
        t = jnp.maximum(i - 1, 0)
        return t // n_seq, t % n_seq

    tile = pl.BlockSpec((1, T, D), lambda i: (*cur(i), 0))
    chunk_in, chunk_out = [], []
    for w in cast_weights:
        assert w.shape[0] % (2 * SUBLANES * n_steps) == 0
        rows_w = w.shape[0] // n_steps
        chunk_in.append(pl.BlockSpec((rows_w, w.shape[1]), lambda i: (jnp.minimum(i, n_steps - 1), 0)))
        chunk_out.append(pl.BlockSpec((rows_w // 2, w.shape[1]), lambda i: (jnp.minimum(i, n_steps - 1), 0)))
    return pl.pallas_call(
        functools.partial(_mixer_kernel, n_seq=n_seq),
        grid=(n_steps + 1,),
        in_specs=[
            tile,
            _const_spec(w_in.shape, True),
            _const_spec(conv_w.shape), _const_spec(conv_b.shape),
            _const_spec(w_ax.shape, True),
            _const_spec(b_a.shape), _const_spec(b_x.shape), _const_spec(lam.shape),
            _const_spec(w_pool.shape, True),
            _const_spec(b_pool.shape), _const_spec(pool_scale.shape),
            _const_spec(w_out.shape, True),
            _const_spec(g.shape), _const_spec(b.shape),
        ] + chunk_in,
        out_specs=[pl.BlockSpec((1, T, D), lambda i: (*prev(i), 0))] + chunk_out,
        out_shape=[jax.ShapeDtypeStruct(x.shape, _F32)]
        + [jax.ShapeDtypeStruct((w.shape[0] // 2, w.shape[1]), jnp.uint32) for w in cast_weights],
        scratch_shapes=[
            pltpu.VMEM((POOL_WIDTH // LANES, rows, LANES), _F32),
            pltpu.VMEM((LRU_WIDTH // LANES, rows, LANES), _F32),
            pltpu.VMEM((LRU_WIDTH // LANES, rows, LANES), _F32),
            pltpu.VMEM((POOL_WIDTH // LANES, rows, LANES), _F32),
            pltpu.VMEM((LRU_WIDTH // LANES, rows, LANES), _F32),
            pltpu.VMEM((POOL_HIST, SUBLANES, POOL_WIDTH), _F32),
            pltpu.VMEM((CONV_WIDTH - 1, SUBLANES, LRU_WIDTH), _F32),
            pltpu.VMEM((SUBLANES, LRU_WIDTH), _F32),
            pltpu.VMEM((2, T, D), _F32),
        ],
        compiler_params=pltpu.CompilerParams(
            dimension_semantics=("arbitrary",), vmem_limit_bytes=VMEM_LIMIT_BYTES),
        name="mixer",
    )(x, w_in, conv_w, conv_b, w_ax, b_a, b_x, lam, w_pool, b_pool, pool_scale, w_out, g, b, *cast_weights)


def _kv_kernel(mem_ref, wk_ref, wv_ref, *rest):
    n_cast = (len(rest) - 2) // 2
    cast_in, k_ref, v_ref, cast_out = rest[:n_cast], rest[n_cast], rest[n_cast + 1], rest[n_cast + 2:]
    for src, dst in zip(cast_in, cast_out):
        dst[...] = _pack_rows(src[...])
    m = mem_ref[...].astype(_BF16)
    k_ref[...] = _pack_rows(_dot(m, wk_ref[...].astype(_BF16)))
    v_ref[...] = _pack_rows(_dot(m, wv_ref[...].astype(_BF16)))


def _kv_proj(mem2d, w_k, w_v, cast_weights):
    M, D = mem2d.shape
    tn = KV_TILE_N
    n_steps = D // tn
    wspec = pl.BlockSpec((D, tn), lambda j: (0, j))
    ospec = pl.BlockSpec((M // 2, tn), lambda j: (0, j))
    chunk_in, chunk_out = [], []
    for w in cast_weights:
        assert w.shape[0] % (2 * SUBLANES * n_steps) == 0
        rows_w = w.shape[0] // n_steps
        chunk_in.append(pl.BlockSpec((rows_w, w.shape[1]), lambda j: (j, 0)))
        chunk_out.append(pl.BlockSpec((rows_w // 2, w.shape[1]), lambda j: (j, 0)))
    return pl.pallas_call(
        _kv_kernel,
        grid=(n_steps,),
        in_specs=[_const_spec(mem2d.shape, True), wspec, wspec] + chunk_in,
        out_specs=[ospec, ospec] + chunk_out,
        out_shape=[jax.ShapeDtypeStruct((M // 2, D), jnp.uint32)] * 2
        + [jax.ShapeDtypeStruct((w.shape[0] // 2, w.shape[1]), jnp.uint32) for w in cast_weights],
        compiler_params=pltpu.CompilerParams(
            dimension_semantics=("arbitrary",), vmem_limit_bytes=VMEM_LIMIT_BYTES),
        name="kv_proj",
    )(mem2d, w_k, w_v, *cast_weights)


def _xattn_kernel(x_ref, k_ref, v_ref, wq_ref, wo_ref, g_ref, b_ref, o_ref, resid_ref):
    i = pl.program_id(0)
    n_tiles = pl.num_programs(0) - 1
    slot = i % 2

    @pl.when(i == 0)
    def _():
        resid_ref[1] = jnp.zeros(resid_ref.shape[1:], _F32)

    def finish_previous():
        return _normalize_previous(resid_ref, slot, g_ref, b_ref, o_ref)

    @pl.when(i < n_tiles)
    def _():
        zeros = finish_previous()
        x = x_ref[0]
        q = _dot(x.astype(_BF16), _unpack_rows(wq_ref[...])).astype(_BF16)
        scale = XATTN_HEAD_DIM ** -0.5
        o_parts = []
        for h in range(XATTN_HEADS):
            hs = slice(h * XATTN_HEAD_DIM, (h + 1) * XATTN_HEAD_DIM)
            sc = lax.dot_general(q[:, hs], _unpack_rows(k_ref[0, :, hs]), (((1,), (1,)), ((), ())),
                                 preferred_element_type=_F32) * scale
            e = jnp.exp(sc - jnp.max(sc, axis=-1, keepdims=True))
            p = e / jnp.sum(e, axis=-1, keepdims=True)
            o_parts.append(_dot(p.astype(_BF16), _unpack_rows(v_ref[0, :, hs])))
        o = jnp.concatenate(o_parts, axis=1).astype(_BF16)
        resid = DEEPNORM_ALPHA * x + _dot(o, _unpack_rows(wo_ref[...]))
        _store_anchored(resid_ref, slot, resid, zeros)

    @pl.when(i == n_tiles)
    def _():
        finish_previous()


def _xattn(x, k, v, w_q, w_o, g, b):
    B, S, D = x.shape
    T = XATTN_TILE
    n_seq = S // T
    n_tiles = B * n_seq

    def cur(i):
        t = jnp.minimum(i, n_tiles - 1)
        return t // n_seq, t % n_seq

    def prev(i):
        t = jnp.maximum(i - 1, 0)
        return t // n_seq, t % n_seq

    kvspec = pl.BlockSpec((1, N_MEM // 2, D), lambda i: (cur(i)[0], 0, 0))
    return pl.pallas_call(
        _xattn_kernel,
        grid=(n_tiles + 1,),
        in_specs=[pl.BlockSpec((1, T, D), lambda i: (*cur(i), 0)), kvspec, kvspec,
                  _const_spec(w_q.shape, True), _const_spec(w_o.shape, True),
                  _const_spec(g.shape), _const_spec(b.shape)],
        out_specs=pl.BlockSpec((1, T, D), lambda i: (*prev(i), 0)),
        out_shape=jax.ShapeDtypeStruct(x.shape, _F32),
        scratch_shapes=[pltpu.VMEM((2, T, D), _F32)],
        compiler_params=pltpu.CompilerParams(
            dimension_semantics=("arbitrary",), vmem_limit_bytes=VMEM_LIMIT_BYTES),
        name="xattn",
    )(x, k, v, w_q, w_o, g, b)


def _mlp_kernel(x_ref, w1_ref, w2_ref, g_ref, b_ref, o_ref, xb_ref, acc_ref, *, n_f):
    i, j = pl.program_id(0), pl.program_id(1)
    n_tiles = pl.num_programs(0) - 1
    slot = i % 2
    rows = x_ref.shape[0] // n_f

    @pl.when((i == 0) & (j == 0))
    def _():
        acc_ref[1] = jnp.zeros(acc_ref.shape[1:], _F32)

    def finish_previous_rows():
        r0 = pl.multiple_of(j * rows, rows)
        o_ref[pl.ds(r0, rows), :] = _layer_norm(acc_ref[1 - slot, pl.ds(r0, rows), :], g_ref[...], b_ref[...])

    @pl.when(i < n_tiles)
    def _():
        @pl.when(j == 0)
        def _():
            xb_ref[...] = x_ref[...].astype(_BF16)
            acc_ref[slot] = DEEPNORM_ALPHA * x_ref[...]

        finish_previous_rows()
        h = jnp.square(jnp.maximum(_dot(xb_ref[...], _unpack_rows(w1_ref[...])), 0.0))
        acc_ref[slot] += _dot(h.astype(_BF16), _unpack_rows(w2_ref[...]))

    @pl.when(i == n_tiles)
    def _():
        finish_previous_rows()


def _mlp(x2d, w1, w2, g, b):
    M, D = x2d.shape
    F = w1.shape[1]
    tm, tf = MLP_TILE_M, MLP_TILE_F
    n_tiles, n_f = M // tm, F // tf
    assert tm % (n_f * SUBLANES) == 0

    def wblk(i, j):
        return jnp.where(i == n_tiles, n_f - 1, j)

    return pl.pallas_call(
        functools.partial(_mlp_kernel, n_f=n_f),
        grid=(n_tiles + 1, n_f),
        in_specs=[pl.BlockSpec((tm, D), lambda i, j: (jnp.minimum(i, n_tiles - 1), 0)),
                  pl.BlockSpec((D // 2, tf), lambda i, j: (0, wblk(i, j))),
                  pl.BlockSpec((tf // 2, D), lambda i, j: (wblk(i, j), 0)),
                  _const_spec(g.shape), _const_spec(b.shape)],
        out_specs=pl.BlockSpec((tm, D), lambda i, j: (jnp.maximum(i - 1, 0), 0)),
        out_shape=jax.ShapeDtypeStruct((M, D), _F32),
        scratch_shapes=[pltpu.VMEM((tm, D), _BF16), pltpu.VMEM((2, tm, D), _F32)],
        compiler_params=pltpu.CompilerParams(
            dimension_semantics=("arbitrary", "arbitrary"),
            vmem_limit_bytes=VMEM_LIMIT_BYTES),
        name="mlp",
    )(x2d, w1, w2, g, b)


def kernel(x, mem, w_in, conv_w, conv_b, w_a, b_a, w_x, b_x, lru_lambda, w_pool, b_pool, pool_scale,
           w_out, ln1_g, ln1_b, w_q, w_k, w_v, w_o, ln2_g, ln2_b, w_ff1, w_ff2, ln3_g, ln3_b):
    B, S, D = x.shape
    row = lambda p: p.reshape(1, -1)
    for l in range(DEPTH):
        w_ax = jnp.concatenate([w_a[l], w_x[l]], axis=-1).astype(_BF16)
        k, v, w_in_b, w_out_b = _kv_proj(mem.reshape(B * N_MEM, D), w_k[l], w_v[l], (w_in[l], w_out[l]))
        x, wq_b, wo_b, w1_b, w2_b = _mixer(
            x, w_in_b, conv_w[l], row(conv_b[l]), w_ax,
            row(b_a[l]), row(b_x[l]), row(lru_lambda[l]),
            w_pool[l].astype(_BF16), row(b_pool[l]), row(pool_scale[l]),
            w_out_b, row(ln1_g[l]), row(ln1_b[l]),
            (w_q[l], w_o[l], w_ff1[l], w_ff2[l]))
        x = _xattn(x, k.reshape(B, N_MEM // 2, D), v.reshape(B, N_MEM // 2, D),
                   wq_b, wo_b, row(ln2_g[l]), row(ln2_b[l]))
        x = _mlp(x.reshape(B * S, D), w1_b, w2_b, row(ln3_g[l]), row(ln3_b[l])).reshape(B, S, D)
    return x
```

```python
import functools

import jax
import jax.numpy as jnp
from jax import lax
from jax.experimental import pallas as pl
from jax.experimental.pallas import tpu as pltpu

D_MODEL = 2048
POOL_WIDTH = 1024
LRU_WIDTH = 1024
POOL_WINDOWS = (2, 4, 8, 16)
POOL_GROUP = 256
LRU_HEADS = 8
LRU_HEAD_DIM = 128
CONV_WIDTH = 4
LRU_C = 8.0
N_MEM = 256
XATTN_HEADS = 4
XATTN_HEAD_DIM = 512
D_FF = 4 * D_MODEL
LN_EPS = 1e-5
DEPTH = 1
DEEPNORM_ALPHA = (2.0 * DEPTH) ** 0.25

POOL_HIST = 16
SUBLANES = 8
LANES = 128

MIXER_TILE = 256
XATTN_TILE = 512
ANCHOR_COLS = 256
MLP_TILE_M = 512
MLP_TILE_F = 1024
KV_TILE_N = 256
VMEM_LIMIT_BYTES = 56 * 1024 * 1024

_F32 = jnp.float32
_BF16 = jnp.bfloat16


def _layer_norm(v, g, b):
    mu = jnp.mean(v, axis=-1, keepdims=True)
    c = v - mu
    var = jnp.mean(jnp.square(c), axis=-1, keepdims=True)
    return c * lax.rsqrt(var + LN_EPS) * g + b


def _dot(a, b):
    return jnp.dot(a, b, preferred_element_type=_F32)


def _pack_rows(v):
    return pltpu.bitcast(v.astype(_BF16), jnp.uint32)


def _unpack_rows(w):
    return pltpu.bitcast(w, _BF16)


def _ordering_zero(v):
    rows, cols = v.shape
    t = jnp.sum(v.reshape(rows // SUBLANES, SUBLANES, cols), axis=0)
    t = functools.reduce(lambda a, b: a + b, [t[:, c:c + LANES] for c in range(0, cols, LANES)])
    return jnp.minimum(jnp.abs(t), 0.0)


def _normalize_previous(resid_ref, slot, g_ref, b_ref, o_ref):
    T, D = resid_ref.shape[1:]
    chunk = T // (D // ANCHOR_COLS)
    zeros = []
    for r0 in range(0, T, chunk):
        out = _layer_norm(resid_ref[1 - slot, r0:r0 + chunk, :], g_ref[...], b_ref[...])
        o_ref[0, r0:r0 + chunk, :] = out
        zeros.append(_ordering_zero(out))
    return zeros


def _store_anchored(resid_ref, slot, resid, zeros):
    T = resid.shape[0]
    for c, zero in enumerate(zeros):
        c0 = c * ANCHOR_COLS
        resid_ref[slot, :, c0:c0 + LANES] = resid[:, c0:c0 + LANES] + jnp.tile(zero, (T // SUBLANES, 1))
        resid_ref[slot, :, c0 + LANES:c0 + ANCHOR_COLS] = resid[:, c0 + LANES:c0 + ANCHOR_COLS]


def _tile_zeros(zeros, rows, cols):
    width = cols // len(zeros)
    return jnp.concatenate([jnp.tile(z, (rows // SUBLANES, width // LANES)) for z in zeros], axis=1)


def _store_chunked(dst_ref, slab0, val, lc, pitch):
    for c in range(val.shape[1] // LANES):
        for j in range(SUBLANES):
            dst_ref[slab0 + c, j * pitch:j * pitch + lc, :] = (
                val[j * lc:(j + 1) * lc, c * LANES:(c + 1) * LANES])


def _load_chunked(src_ref, slab0, n_slabs, lc, pitch):
    return jnp.concatenate(
        [jnp.concatenate([src_ref[slab0 + c, j * pitch:j * pitch + lc, :] for j in range(SUBLANES)], axis=0)
         for c in range(n_slabs)], axis=-1)


def _load_interleaved(src_ref, slab0, n_slabs, lc, pitch):
    cols = [jnp.stack([src_ref[slab0 + c, pl.ds(k, SUBLANES, stride=pitch), :] for k in range(lc)], axis=0)
            for c in range(n_slabs)]
    return jnp.concatenate(cols, axis=-1)


def _store_interleaved(dst_ref, slab0, val3, pitch):
    for c in range(val3.shape[2] // LANES):
        for k in range(val3.shape[0]):
            dst_ref[slab0 + c, pl.ds(k, SUBLANES, stride=pitch), :] = val3[k, :, c * LANES:(c + 1) * LANES]


def _history_rows(cur_tail, prev_tail):
    sub = lax.broadcasted_iota(jnp.int32, (1,) + cur_tail.shape[1:], 1)
    return jnp.where(sub == 0, pltpu.roll(prev_tail, 1, 1), pltpu.roll(cur_tail, 1, 1))


def _mixer_kernel(x_ref, w_in_ref, conv_w_ref, conv_b_ref, w_ax_ref, b_a_ref, b_x_ref, lam_ref,
                  w_pool_ref, b_pool_ref, pool_scale_ref, w_out_ref, g_ref, b_ref, *rest, n_seq):
    n_cast = (len(rest) - 10) // 2
    cast_in, o_ref, cast_out = rest[:n_cast], rest[n_cast], rest[n_cast + 1:2 * n_cast + 1]
    up_ref, ul_ref, ug_ref, zp_ref, zl_ref, pool_hist, conv_hist, h_carry, resid_ref = rest[2 * n_cast + 1:]
    T = x_ref.shape[1]
    lc = T // SUBLANES
    pitch = lc + SUBLANES
    i = pl.program_id(0)
    n_tiles = pl.num_programs(0) - 1
    s = jnp.minimum(i, n_tiles - 1) % n_seq
    slot = i % 2

    @pl.when(i == 0)
    def _():
        resid_ref[1] = jnp.zeros(resid_ref.shape[1:], _F32)

    def finish_previous():
        return _normalize_previous(resid_ref, slot, g_ref, b_ref, o_ref)

    @pl.when(i == n_tiles)
    def _():
        finish_previous()

    @pl.when(i < n_tiles)
    def _():
        _mixer_tile(s, slot, finish_previous, x_ref, w_in_ref, conv_w_ref, conv_b_ref, w_ax_ref, b_a_ref, b_x_ref,
                    lam_ref, w_pool_ref, b_pool_ref, pool_scale_ref, w_out_ref, cast_in, cast_out,
                    up_ref, ul_ref, ug_ref, zp_ref, zl_ref, pool_hist, conv_hist, h_carry, resid_ref, lc, pitch)


def _mixer_tile(s, slot, finish_previous, x_ref, w_in_ref, conv_w_ref, conv_b_ref, w_ax_ref, b_a_ref, b_x_ref,
                lam_ref, w_pool_ref, b_pool_ref, pool_scale_ref, w_out_ref, cast_in, cast_out,
                up_ref, ul_ref, ug_ref, zp_ref, zl_ref, pool_hist, conv_hist, h_carry, resid_ref, lc, pitch):
    T = x_ref.shape[1]

    @pl.when(s == 0)
    def _():
        pool_hist[...] = jnp.zeros_like(pool_hist)
        conv_hist[...] = jnp.zeros_like(conv_hist)
        h_carry[...] = jnp.zeros_like(h_carry)

    zeros = finish_previous()
    for src, dst in zip(cast_in, cast_out):
        dst[...] = _pack_rows(src[...])

    x = x_ref[0]
    xb = x.astype(_BF16)
    n_pool, n_lru = POOL_WIDTH // LANES, LRU_WIDTH // LANES
    half = len(zeros) // 2
    u_pool = _dot(xb, _unpack_rows(w_in_ref[:, :POOL_WIDTH])) + _tile_zeros(zeros[:half], T, POOL_WIDTH)
    _store_chunked(up_ref, 0, u_pool, lc, pitch)
    up = _load_interleaved(up_ref, 0, n_pool, lc, pitch)
    u_lru = (_dot(xb, _unpack_rows(w_in_ref[:, POOL_WIDTH:POOL_WIDTH + LRU_WIDTH]))
             + _tile_zeros(zeros[half:], T, LRU_WIDTH))
    _store_chunked(ul_ref, 0, u_lru, lc, pitch)

    kk = lax.broadcasted_iota(jnp.int32, (POOL_HIST, SUBLANES, LANES), 0)
    jj = lax.broadcasted_iota(jnp.int32, (POOL_HIST, SUBLANES, LANES), 1)
    t_head = s * T + jj * lc + kk

    ext = jnp.concatenate([_history_rows(up[lc - POOL_HIST:], pool_hist[...]), up], axis=0)
    pool_hist[...] = up[lc - POOL_HIST:]
    s2 = ext[1:] + ext[:-1]
    s2r = s2[:, :, POOL_GROUP:]
    s4 = s2r[2:] + s2r[:-2]
    s4r = s4[:, :, POOL_GROUP:]
    s8 = s4r[4:] + s4r[:-4]
    s8r = s8[:, :, POOL_GROUP:]
    s16 = s8r[8:] + s8r[:-8]
    sums = (s2[15:, :, :POOL_GROUP], s4[13:, :, :POOL_GROUP], s8[9:, :, :POOL_GROUP], s16[1:])
    y_pool = []
    for g, w in enumerate(POOL_WINDOWS):
        cs = slice(g * POOL_GROUP, (g + 1) * POOL_GROUP)
        inv_head = 1.0 / jnp.minimum(t_head + 1, w).astype(_F32)
        inv_head = jnp.concatenate([inv_head] * (POOL_GROUP // LANES), axis=-1)
        mean = jnp.concatenate([sums[g][:POOL_HIST] * inv_head, sums[g][POOL_HIST:] * (1.0 / w)], axis=0)
        mixed = mean - up[:, :, cs]
        yg = _dot(mixed.reshape(T, POOL_GROUP).astype(_BF16), w_pool_ref[g]).reshape(lc, SUBLANES, POOL_GROUP)
        y_pool.append((yg + b_pool_ref[:, cs]) * pool_scale_ref[:, cs])
    _store_interleaved(zp_ref, 0, jnp.concatenate(y_pool, axis=-1), pitch)

    z_pool = _load_chunked(zp_ref, 0, n_pool, lc, pitch).astype(_BF16)
    n_hist = CONV_WIDTH - 1
    sub = lax.broadcasted_iota(jnp.int32, (SUBLANES, LRU_HEAD_DIM), 0)
    first = t_head[0:1] == 0
    gate_cols = 2 * LRU_HEAD_DIM
    out_cols = D_MODEL // LRU_HEADS
    y_top = []
    for h in range(LRU_HEADS):
        hs = slice(h * LRU_HEAD_DIM, (h + 1) * LRU_HEAD_DIM)
        ul = _load_interleaved(ul_ref, h, 1, lc, pitch)
        ext2 = jnp.concatenate([_history_rows(ul[lc - n_hist:], conv_hist[:, :, hs]), ul], axis=0)
        conv_hist[:, :, hs] = ul[lc - n_hist:]
        xc = ext2[0:lc] * conv_w_ref[0:1, hs]
        for k in range(1, CONV_WIDTH):
            xc = xc + ext2[k:k + lc] * conv_w_ref[k:k + 1, hs]
        xc = xc + conv_b_ref[:, hs]

        pre = _dot(xc.reshape(T, LRU_HEAD_DIM).astype(_BF16), w_ax_ref[h])
        pre = pre.reshape(lc, SUBLANES, 2 * LRU_HEAD_DIM)
        r = jax.nn.sigmoid(pre[:, :, :LRU_HEAD_DIM] + b_a_ref[:, hs])
        i = jax.nn.sigmoid(pre[:, :, LRU_HEAD_DIM:] + b_x_ref[:, hs])

        if h % 2 == 0:
            c0 = POOL_WIDTH + LRU_WIDTH + h * LRU_HEAD_DIM
            _store_chunked(ug_ref, h, _dot(xb, _unpack_rows(w_in_ref[:, c0:c0 + gate_cols])), lc, pitch)
        else:
            c0 = (h // 2) * out_cols
            y_top.append(_dot(z_pool, _unpack_rows(w_out_ref[:POOL_WIDTH // 2, c0:c0 + out_cols])))

        log_a = (-LRU_C * r) * jax.nn.softplus(-lam_ref[:, hs])
        a = jnp.exp(log_a)
        u = -jnp.tanh(log_a) * (a * a + 1.0)
        mult = jnp.where(u == 0.0, 0.0, u * lax.rsqrt(u))
        mult = jnp.concatenate([jnp.where(first, 1.0, mult[0:1]), mult[1:]], axis=0)
        bv = mult * (i * xc)

        hl, ac = [bv[0]], [a[0]]
        for k in range(1, lc):
            hl.append(a[k] * hl[-1] + bv[k])
            ac.append(a[k] * ac[-1])
        c_a, c_b = ac[-1], hl[-1]
        for d in (1, 2, 4):
            a_s = jnp.where(sub < d, 1.0, pltpu.roll(c_a, d, 0))
            b_s = jnp.where(sub < d, 0.0, pltpu.roll(c_b, d, 0))
            c_b = c_a * b_s + c_b
            c_a = c_a * a_s
        h_prev = h_carry[:, hs]
        h_end = c_b + c_a * h_prev
        h_in = jnp.where(sub == 0, h_prev, pltpu.roll(h_end, 1, 0))
        h_carry[:, hs] = jnp.broadcast_to(h_end[SUBLANES - 1:SUBLANES, :], (SUBLANES, LRU_HEAD_DIM))
        hseq = jnp.stack(hl, axis=0) + jnp.stack(ac, axis=0) * h_in
        ug = _load_interleaved(ug_ref, h, 1, lc, pitch)
        _store_interleaved(zl_ref, h, hseq * jax.nn.gelu(ug), pitch)

    for q in range(LRU_HEADS // 2, LRU_HEADS):
        y_top.append(_dot(z_pool, _unpack_rows(w_out_ref[:POOL_WIDTH // 2, q * out_cols:(q + 1) * out_cols])))
    z_lru = _load_chunked(zl_ref, 0, n_lru, lc, pitch).astype(_BF16)
    y = jnp.concatenate(y_top, axis=1) + _dot(z_lru, _unpack_rows(w_out_ref[POOL_WIDTH // 2:, :]))
    resid_ref[slot] = DEEPNORM_ALPHA * x + y


def _const_spec(shape, single_buffer=False):
    nd = len(shape)
    kwargs = {"pipeline_mode": pl.Buffered(1)} if single_buffer else {}
    return pl.BlockSpec(shape, lambda *_: (0,) * nd, **kwargs)


def _mixer(x, w_in, conv_w, conv_b, w_ax, b_a, b_x, lam, w_pool, b_pool, pool_scale, w_out, g, b,
           cast_weights):
    B, S, D = x.shape
    T = MIXER_TILE
    n_seq = S // T
    n_steps = B * n_seq
    assert T % SUBLANES == 0 and T // SUBLANES >= POOL_HIST and (T // SUBLANES) % (2 * SUBLANES) == 0
    rows = SUBLANES * (T // SUBLANES + SUBLANES)

    def cur(i):
        t = jnp.minimum(i, n_steps - 1)
        return t // n_seq, t % n_seq

    def prev(i):
        t = jnp.maximum(i - 1, 0)
        return t // n_seq, t % n_seq

    tile = pl.BlockSpec((1, T, D), lambda i: (*cur(i), 0))
    chunk_in, chunk_out = [], []
    for w in cast_weights:
        assert w.shape[0] % (2 * SUBLANES * n_steps) == 0
        rows_w = w.shape[0] // n_steps
        chunk_in.append(pl.BlockSpec((rows_w, w.shape[1]), lambda i: (jnp.minimum(i, n_steps - 1), 0)))
        chunk_out.append(pl.BlockSpec((rows_w // 2, w.shape[1]), lambda i: (jnp.minimum(i, n_steps - 1), 0)))
    return pl.pallas_call(
        functools.partial(_mixer_kernel, n_seq=n_seq),
        grid=(n_steps + 1,),
        in_specs=[
            tile,
            _const_spec(w_in.shape, True),
            _const_spec(conv_w.shape), _const_spec(conv_b.shape),
            _const_spec(w_ax.shape, True),
            _const_spec(b_a.shape), _const_spec(b_x.shape), _const_spec(lam.shape),
            _const_spec(w_pool.shape, True),
            _const_spec(b_pool.shape), _const_spec(pool_scale.shape),
            _const_spec(w_out.shape, True),
            _const_spec(g.shape), _const_spec(b.shape),
        ] + chunk_in,
        out_specs=[pl.BlockSpec((1, T, D), lambda i: (*prev(i), 0))] + chunk_out,
        out_shape=[jax.ShapeDtypeStruct(x.shape, _F32)]
        + [jax.ShapeDtypeStruct((w.shape[0] // 2, w.shape[1]), jnp.uint32) for w in cast_weights],
        scratch_shapes=[
            pltpu.VMEM((POOL_WIDTH // LANES, rows, LANES), _F32),
            pltpu.VMEM((LRU_WIDTH // LANES, rows, LANES), _F32),
            pltpu.VMEM((LRU_WIDTH // LANES, rows, LANES), _F32),
            pltpu.VMEM((POOL_WIDTH // LANES, rows, LANES), _F32),
            pltpu.VMEM((LRU_WIDTH // LANES, rows, LANES), _F32),
            pltpu.VMEM((POOL_HIST, SUBLANES, POOL_WIDTH), _F32),
            pltpu.VMEM((CONV_WIDTH - 1, SUBLANES, LRU_WIDTH), _F32),
            pltpu.VMEM((SUBLANES, LRU_WIDTH), _F32),
            pltpu.VMEM((2, T, D), _F32),
        ],
        compiler_params=pltpu.CompilerParams(
            dimension_semantics=("arbitrary",), vmem_limit_bytes=VMEM_LIMIT_BYTES),
        name="mixer",
    )(x, w_in, conv_w, conv_b, w_ax, b_a, b_x, lam, w_pool, b_pool, pool_scale, w_out, g, b, *cast_weights)


def _kv_kernel(mem_ref, wk_ref, wv_ref, *rest):
    n_cast = (len(rest) - 2) // 2
    cast_in, k_ref, v_ref, cast_out = rest[:n_cast], rest[n_cast], rest[n_cast + 1], rest[n_cast + 2:]
    for src, dst in zip(cast_in, cast_out):
        dst[...] = _pack_rows(src[...])
    m = mem_ref[...].astype(_BF16)
    k_ref[...] = _pack_rows(_dot(m, wk_ref[...].astype(_BF16)))
    v_ref[...] = _pack_rows(_dot(m, wv_ref[...].astype(_BF16)))


def _kv_proj(mem2d, w_k, w_v, cast_weights):
    M, D = mem2d.shape
    tn = KV_TILE_N
    n_steps = D // tn
    wspec = pl.BlockSpec((D, tn), lambda j: (0, j))
    ospec = pl.BlockSpec((M // 2, tn), lambda j: (0, j))
    chunk_in, chunk_out = [], []
    for w in cast_weights:
        assert w.shape[0] % (2 * SUBLANES * n_steps) == 0
        rows_w = w.shape[0] // n_steps
        chunk_in.append(pl.BlockSpec((rows_w, w.shape[1]), lambda j: (j, 0)))
        chunk_out.append(pl.BlockSpec((rows_w // 2, w.shape[1]), lambda j: (j, 0)))
    return pl.pallas_call(
        _kv_kernel,
        grid=(n_steps,),
        in_specs=[_const_spec(mem2d.shape, True), wspec, wspec] + chunk_in,
        out_specs=[ospec, ospec] + chunk_out,
        out_shape=[jax.ShapeDtypeStruct((M // 2, D), jnp.uint32)] * 2
        + [jax.ShapeDtypeStruct((w.shape[0] // 2, w.shape[1]), jnp.uint32) for w in cast_weights],
        compiler_params=pltpu.CompilerParams(
            dimension_semantics=("arbitrary",), vmem_limit_bytes=VMEM_LIMIT_BYTES),
        name="kv_proj",
    )(mem2d, w_k, w_v, *cast_weights)


def _xattn_kernel(x_ref, k_ref, v_ref, wq_ref, wo_ref, g_ref, b_ref, o_ref, resid_ref):
    i = pl.program_id(0)
    n_tiles = pl.num_programs(0) - 1
    slot = i % 2

    @pl.when(i == 0)
    def _():
        resid_ref[1] = jnp.zeros(resid_ref.shape[1:], _F32)

    def finish_previous():
        return _normalize_previous(resid_ref, slot, g_ref, b_ref, o_ref)

    @pl.when(i < n_tiles)
    def _():
        zeros = finish_previous()
        x = x_ref[0]
        q = _dot(x.astype(_BF16), _unpack_rows(wq_ref[...])).astype(_BF16)
        scale = XATTN_HEAD_DIM ** -0.5
        o_parts = []
        for h in range(XATTN_HEADS):
            hs = slice(h * XATTN_HEAD_DIM, (h + 1) * XATTN_HEAD_DIM)
            sc = lax.dot_general(q[:, hs], _unpack_rows(k_ref[0, :, hs]), (((1,), (1,)), ((), ())),
                                 preferred_element_type=_F32) * scale
            e = jnp.exp(sc - jnp.max(sc, axis=-1, keepdims=True))
            p = e / jnp.sum(e, axis=-1, keepdims=True)
            o_parts.append(_dot(p.astype(_BF16), _unpack_rows(v_ref[0, :, hs])))
        o = jnp.concatenate(o_parts, axis=1).astype(_BF16)
        resid = DEEPNORM_ALPHA * x + _dot(o, _unpack_rows(wo_ref[...]))
        _store_anchored(resid_ref, slot, resid, zeros)

    @pl.when(i == n_tiles)
    def _():
        finish_previous()


def _xattn(x, k, v, w_q, w_o, g, b):
    B, S, D = x.shape
    T = XATTN_TILE
    n_seq = S // T
    n_tiles = B * n_seq

    def cur(i):
        t = jnp.minimum(i, n_tiles - 1)
        return t // n_seq, t % n_seq

    def prev(i):
        t = jnp.maximum(i - 1, 0)
        return t // n_seq, t % n_seq

    kvspec = pl.BlockSpec((1, N_MEM // 2, D), lambda i: (cur(i)[0], 0, 0))
    return pl.pallas_call(
        _xattn_kernel,
        grid=(n_tiles + 1,),
        in_specs=[pl.BlockSpec((1, T, D), lambda i: (*cur(i), 0)), kvspec, kvspec,
                  _const_spec(w_q.shape, True), _const_spec(w_o.shape, True),
                  _const_spec(g.shape), _const_spec(b.shape)],
        out_specs=pl.BlockSpec((1, T, D), lambda i: (*prev(i), 0)),
        out_shape=jax.ShapeDtypeStruct(x.shape, _F32),
        scratch_shapes=[pltpu.VMEM((2, T, D), _F32)],
        compiler_params=pltpu.CompilerParams(
            dimension_semantics=("arbitrary",), vmem_limit_bytes=VMEM_LIMIT_BYTES),
        name="xattn",
    )(x, k, v, w_q, w_o, g, b)


def _mlp_kernel(x_ref, w1_ref, w2_ref, g_ref, b_ref, o_ref, xb_ref, acc_ref, *, n_f):
    i, j = pl.program_id(0), pl.program_id(1)
    n_tiles = pl.num_programs(0) - 1
    slot = i % 2
    rows = x_ref.shape[0] // n_f

    @pl.when((i == 0) & (j == 0))
    def _():
        acc_ref[1] = jnp.zeros(acc_ref.shape[1:], _F32)

    def finish_previous_rows():
        r0 = pl.multiple_of(j * rows, rows)
        o_ref[pl.ds(r0, rows), :] = _layer_norm(acc_ref[1 - slot, pl.ds(r0, rows), :], g_ref[...], b_ref[...])

    @pl.when(i < n_tiles)
    def _():
        @pl.when(j == 0)
        def _():
            xb_ref[...] = x_ref[...].astype(_BF16)
            acc_ref[slot] = DEEPNORM_ALPHA * x_ref[...]

        finish_previous_rows()
        h = jnp.square(jnp.maximum(_dot(xb_ref[...], _unpack_rows(w1_ref[...])), 0.0))
        acc_ref[slot] += _dot(h.astype(_BF16), _unpack_rows(w2_ref[...]))

    @pl.when(i == n_tiles)
    def _():
        finish_previous_rows()


def _mlp(x2d, w1, w2, g, b):
    M, D = x2d.shape
    F = w1.shape[1]
    tm, tf = MLP_TILE_M, MLP_TILE_F
    n_tiles, n_f = M // tm, F // tf
    assert tm % (n_f * SUBLANES) == 0

    def wblk(i, j):
        return jnp.where(i == n_tiles, n_f - 1, j)

    return pl.pallas_call(
        functools.partial(_mlp_kernel, n_f=n_f),
        grid=(n_tiles + 1, n_f),
        in_specs=[pl.BlockSpec((tm, D), lambda i, j: (jnp.minimum(i, n_tiles - 1), 0)),
                  pl.BlockSpec((D // 2, tf), lambda i, j: (0, wblk(i, j))),
                  pl.BlockSpec((tf // 2, D), lambda i, j: (wblk(i, j), 0)),
                  _const_spec(g.shape), _const_spec(b.shape)],
        out_specs=pl.BlockSpec((tm, D), lambda i, j: (jnp.maximum(i - 1, 0), 0)),
        out_shape=jax.ShapeDtypeStruct((M, D), _F32),
        scratch_shapes=[pltpu.VMEM((tm, D), _BF16), pltpu.VMEM((2, tm, D), _F32)],
        compiler_params=pltpu.CompilerParams(
            dimension_semantics=("arbitrary", "arbitrary"),
            vmem_limit_bytes=VMEM_LIMIT_BYTES),
        name="mlp",
    )(x2d, w1, w2, g, b)


def kernel(x, mem, w_in, conv_w, conv_b, w_a, b_a, w_x, b_x, lru_lambda, w_pool, b_pool, pool_scale,
           w_out, ln1_g, ln1_b, w_q, w_k, w_v, w_o, ln2_g, ln2_b, w_ff1, w_ff2, ln3_g, ln3_b):
    B, S, D = x.shape
    row = lambda p: p.reshape(1, -1)
    for l in range(DEPTH):
        w_ax = jnp.concatenate([w_a[l], w_x[l]], axis=-1).astype(_BF16)
        k, v, w_in_b, w_out_b = _kv_proj(mem.reshape(B * N_MEM, D), w_k[l], w_v[l], (w_in[l], w_out[l]))
        x, wq_b, wo_b, w1_b, w2_b = _mixer(
            x, w_in_b, conv_w[l], row(conv_b[l]), w_ax,
            row(b_a[l]), row(b_x[l]), row(lru_lambda[l]),
            w_pool[l].astype(_BF16), row(b_pool[l]), row(pool_scale[l]),
            w_out_b, row(ln1_g[l]), row(ln1_b[l]),
            (w_q[l], w_o[l], w_ff1[l], w_ff2[l]))
        x = _xattn(x, k.reshape(B, N_MEM // 2, D), v.reshape(B, N_MEM // 2, D),
                   wq_b, wo_b, row(ln2_g[l]), row(ln2_b[l]))
        x = _mlp(x.reshape(B * S, D), w1_b, w2_b, row(ln3_g[l]), row(ln3_b[l])).reshape(B, S, D)
    return x
```

```python
import functools

import jax
import jax.numpy as jnp
from jax import lax
from jax.experimental import pallas as pl
from jax.experimental.pallas import tpu as pltpu

D_MODEL = 2048
POOL_WIDTH = 1024
LRU_WIDTH = 1024
POOL_WINDOWS = (2, 4, 8, 16)
POOL_GROUP = 256
LRU_HEADS = 8
LRU_HEAD_DIM = 128
CONV_WIDTH = 4
LRU_C = 8.0
N_MEM = 256
XATTN_HEADS = 4
XATTN_HEAD_DIM = 512
D_FF = 4 * D_MODEL
LN_EPS = 1e-5
DEPTH = 1
DEEPNORM_ALPHA = (2.0 * DEPTH) ** 0.25

POOL_HIST = 16
SUBLANES = 8
LANES = 128

MIXER_TILE = 256
XATTN_TILE = 512
ANCHOR_COLS = 256
MLP_TILE_M = 512
MLP_TILE_F = 1024
KV_TILE_N = 256
VMEM_LIMIT_BYTES = 56 * 1024 * 1024

_F32 = jnp.float32
_BF16 = jnp.bfloat16


def _layer_norm(v, g, b):
    mu = jnp.mean(v, axis=-1, keepdims=True)
    c = v - mu
    var = jnp.mean(jnp.square(c), axis=-1, keepdims=True)
    return c * lax.rsqrt(var + LN_EPS) * g + b


def _dot(a, b):
    return jnp.dot(a, b, preferred_element_type=_F32)


def _pack_rows(v):
    return pltpu.bitcast(v.astype(_BF16), jnp.uint32)


def _unpack_rows(w):
    return pltpu.bitcast(w, _BF16)


def _ordering_zero(v):
    rows, cols = v.shape
    t = jnp.sum(v.reshape(rows // SUBLANES, SUBLANES, cols), axis=0)
    t = functools.reduce(lambda a, b: a + b, [t[:, c:c + LANES] for c in range(0, cols, LANES)])
    return jnp.minimum(jnp.abs(t), 0.0)


def _normalize_previous(resid_ref, slot, g_ref, b_ref, o_ref):
    T, D = resid_ref.shape[1:]
    chunk = T // (D // ANCHOR_COLS)
    zeros = []
    for r0 in range(0, T, chunk):
        out = _layer_norm(resid_ref[1 - slot, r0:r0 + chunk, :], g_ref[...], b_ref[...])
        o_ref[0, r0:r0 + chunk, :] = out
        zeros.append(_ordering_zero(out))
    return zeros


def _store_anchored(resid_ref, slot, resid, zeros):
    T = resid.shape[0]
    for c, zero in enumerate(zeros):
        c0 = c * ANCHOR_COLS
        resid_ref[slot, :, c0:c0 + LANES] = resid[:, c0:c0 + LANES] + jnp.tile(zero, (T // SUBLANES, 1))
        resid_ref[slot, :, c0 + LANES:c0 + ANCHOR_COLS] = resid[:, c0 + LANES:c0 + ANCHOR_COLS]


def _tile_zeros(zeros, rows, cols):
    width = cols // len(zeros)
    return jnp.concatenate([jnp.tile(z, (rows // SUBLANES, width // LANES)) for z in zeros], axis=1)


def _store_chunked(dst_ref, slab0, val, lc, pitch):
    for c in range(val.shape[1] // LANES):
        for j in range(SUBLANES):
            dst_ref[slab0 + c, j * pitch:j * pitch + lc, :] = (
                val[j * lc:(j + 1) * lc, c * LANES:(c + 1) * LANES])


def _load_chunked(src_ref, slab0, n_slabs, lc, pitch):
    return jnp.concatenate(
        [jnp.concatenate([src_ref[slab0 + c, j * pitch:j * pitch + lc, :] for j in range(SUBLANES)], axis=0)
         for c in range(n_slabs)], axis=-1)


def _load_interleaved(src_ref, slab0, n_slabs, lc, pitch):
    cols = [jnp.stack([src_ref[slab0 + c, pl.ds(k, SUBLANES, stride=pitch), :] for k in range(lc)], axis=0)
            for c in range(n_slabs)]
    return jnp.concatenate(cols, axis=-1)


def _store_interleaved(dst_ref, slab0, val3, pitch):
    for c in range(val3.shape[2] // LANES):
        for k in range(val3.shape[0]):
            dst_ref[slab0 + c, pl.ds(k, SUBLANES, stride=pitch), :] = val3[k, :, c * LANES:(c + 1) * LANES]


def _history_rows(cur_tail, prev_tail):
    sub = lax.broadcasted_iota(jnp.int32, (1,) + cur_tail.shape[1:], 1)
    return jnp.where(sub == 0, pltpu.roll(prev_tail, 1, 1), pltpu.roll(cur_tail, 1, 1))


def _mixer_kernel(x_ref, w_in_ref, conv_w_ref, conv_b_ref, w_ax_ref, b_a_ref, b_x_ref, lam_ref,
                  w_pool_ref, b_pool_ref, pool_scale_ref, w_out_ref, g_ref, b_ref, *rest, n_seq):
    n_cast = (len(rest) - 10) // 2
    cast_in, o_ref, cast_out = rest[:n_cast], rest[n_cast], rest[n_cast + 1:2 * n_cast + 1]
    up_ref, ul_ref, ug_ref, zp_ref, zl_ref, pool_hist, conv_hist, h_carry, resid_ref = rest[2 * n_cast + 1:]
    T = x_ref.shape[1]
    lc = T // SUBLANES
    pitch = lc + SUBLANES
    i = pl.program_id(0)
    n_tiles = pl.num_programs(0) - 1
    s = jnp.minimum(i, n_tiles - 1) % n_seq
    slot = i % 2

    @pl.when(i == 0)
    def _():
        resid_ref[1] = jnp.zeros(resid_ref.shape[1:], _F32)

    def finish_previous():
        return _normalize_previous(resid_ref, slot, g_ref, b_ref, o_ref)

    @pl.when(i == n_tiles)
    def _():
        finish_previous()

    @pl.when(i < n_tiles)
    def _():
        _mixer_tile(s, slot, finish_previous, x_ref, w_in_ref, conv_w_ref, conv_b_ref, w_ax_ref, b_a_ref, b_x_ref,
                    lam_ref, w_pool_ref, b_pool_ref, pool_scale_ref, w_out_ref, cast_in, cast_out,
                    up_ref, ul_ref, ug_ref, zp_ref, zl_ref, pool_hist, conv_hist, h_carry, resid_ref, lc, pitch)


def _mixer_tile(s, slot, finish_previous, x_ref, w_in_ref, conv_w_ref, conv_b_ref, w_ax_ref, b_a_ref, b_x_ref,
                lam_ref, w_pool_ref, b_pool_ref, pool_scale_ref, w_out_ref, cast_in, cast_out,
                up_ref, ul_ref, ug_ref, zp_ref, zl_ref, pool_hist, conv_hist, h_carry, resid_ref, lc, pitch):
    T = x_ref.shape[1]

    @pl.when(s == 0)
    def _():
        pool_hist[...] = jnp.zeros_like(pool_hist)
        conv_hist[...] = jnp.zeros_like(conv_hist)
        h_carry[...] = jnp.zeros_like(h_carry)

    zeros = finish_previous()
    for src, dst in zip(cast_in, cast_out):
        dst[...] = _pack_rows(src[...])

    x = x_ref[0]
    xb = x.astype(_BF16)
    n_pool, n_lru = POOL_WIDTH // LANES, LRU_WIDTH // LANES
    half = len(zeros) // 2
    u_pool = _dot(xb, _unpack_rows(w_in_ref[:, :POOL_WIDTH])) + _tile_zeros(zeros[:half], T, POOL_WIDTH)
    _store_chunked(up_ref, 0, u_pool, lc, pitch)
    up = _load_interleaved(up_ref, 0, n_pool, lc, pitch)
    u_lru = (_dot(xb, _unpack_rows(w_in_ref[:, POOL_WIDTH:POOL_WIDTH + LRU_WIDTH]))
             + _tile_zeros(zeros[half:], T, LRU_WIDTH))
    _store_chunked(ul_ref, 0, u_lru, lc, pitch)

    kk = lax.broadcasted_iota(jnp.int32, (POOL_HIST, SUBLANES, LANES), 0)
    jj = lax.broadcasted_iota(jnp.int32, (POOL_HIST, SUBLANES, LANES), 1)
    t_head = s * T + jj * lc + kk

    ext = jnp.concatenate([_history_rows(up[lc - POOL_HIST:], pool_hist[...]), up], axis=0)
    pool_hist[...] = up[lc - POOL_HIST:]
    s2 = ext[1:] + ext[:-1]
    s2r = s2[:, :, POOL_GROUP:]
    s4 = s2r[2:] + s2r[:-2]
    s4r = s4[:, :, POOL_GROUP:]
    s8 = s4r[4:] + s4r[:-4]
    s8r = s8[:, :, POOL_GROUP:]
    s16 = s8r[8:] + s8r[:-8]
    sums = (s2[15:, :, :POOL_GROUP], s4[13:, :, :POOL_GROUP], s8[9:, :, :POOL_GROUP], s16[1:])
    y_pool = []
    for g, w in enumerate(POOL_WINDOWS):
        cs = slice(g * POOL_GROUP, (g + 1) * POOL_GROUP)
        inv_head = 1.0 / jnp.minimum(t_head + 1, w).astype(_F32)
        inv_head = jnp.concatenate([inv_head] * (POOL_GROUP // LANES), axis=-1)
        mean = jnp.concatenate([sums[g][:POOL_HIST] * inv_head, sums[g][POOL_HIST:] * (1.0 / w)], axis=0)
        mixed = mean - up[:, :, cs]
        yg = _dot(mixed.reshape(T, POOL_GROUP).astype(_BF16), w_pool_ref[g]).reshape(lc, SUBLANES, POOL_GROUP)
        y_pool.append((yg + b_pool_ref[:, cs]) * pool_scale_ref[:, cs])
    _store_interleaved(zp_ref, 0, jnp.concatenate(y_pool, axis=-1), pitch)

    z_pool = _load_chunked(zp_ref, 0, n_pool, lc, pitch).astype(_BF16)
    n_hist = CONV_WIDTH - 1
    sub = lax.broadcasted_iota(jnp.int32, (SUBLANES, LRU_HEAD_DIM), 0)
    first = t_head[0:1] == 0
    gate_cols = 2 * LRU_HEAD_DIM
    out_cols = D_MODEL // LRU_HEADS
    y_top = []
    for h in range(LRU_HEADS):
        hs = slice(h * LRU_HEAD_DIM, (h + 1) * LRU_HEAD_DIM)
        ul = _load_interleaved(ul_ref, h, 1, lc, pitch)
        ext2 = jnp.concatenate([_history_rows(ul[lc - n_hist:], conv_hist[:, :, hs]), ul], axis=0)
        conv_hist[:, :, hs] = ul[lc - n_hist:]
        xc = ext2[0:lc] * conv_w_ref[0:1, hs]
        for k in range(1, CONV_WIDTH):
            xc = xc + ext2[k:k + lc] * conv_w_ref[k:k + 1, hs]
        xc = xc + conv_b_ref[:, hs]

        pre = _dot(xc.reshape(T, LRU_HEAD_DIM).astype(_BF16), w_ax_ref[h])
        pre = pre.reshape(lc, SUBLANES, 2 * LRU_HEAD_DIM)
        r = jax.nn.sigmoid(pre[:, :, :LRU_HEAD_DIM] + b_a_ref[:, hs])
        i = jax.nn.sigmoid(pre[:, :, LRU_HEAD_DIM:] + b_x_ref[:, hs])

        if h % 2 == 0:
            c0 = POOL_WIDTH + LRU_WIDTH + h * LRU_HEAD_DIM
            _store_chunked(ug_ref, h, _dot(xb, _unpack_rows(w_in_ref[:, c0:c0 + gate_cols])), lc, pitch)
        else:
            c0 = (h // 2) * out_cols
            y_top.append(_dot(z_pool, _unpack_rows(w_out_ref[:POOL_WIDTH // 2, c0:c0 + out_cols])))

        log_a = (-LRU_C * r) * jax.nn.softplus(-lam_ref[:, hs])
        a = jnp.exp(log_a)
        u = -jnp.tanh(log_a) * (a * a + 1.0)
        mult = jnp.where(u == 0.0, 0.0, u * lax.rsqrt(u))
        mult = jnp.concatenate([jnp.where(first, 1.0, mult[0:1]), mult[1:]], axis=0)
        bv = mult * (i * xc)

        hl, ac = [bv[0]], [a[0]]
        for k in range(1, lc):
            hl.append(a[k] * hl[-1] + bv[k])
            ac.append(a[k] * ac[-1])
        c_a, c_b = ac[-1], hl[-1]
        for d in (1, 2, 4):
            a_s = jnp.where(sub < d, 1.0, pltpu.roll(c_a, d, 0))
            b_s = jnp.where(sub < d, 0.0, pltpu.roll(c_b, d, 0))
            c_b = c_a * b_s + c_b
            c_a = c_a * a_s
        h_prev = h_carry[:, hs]
        h_end = c_b + c_a * h_prev
        h_in = jnp.where(sub == 0, h_prev, pltpu.roll(h_end, 1, 0))
        h_carry[:, hs] = jnp.broadcast_to(h_end[SUBLANES - 1:SUBLANES, :], (SUBLANES, LRU_HEAD_DIM))
        hseq = jnp.stack(hl, axis=0) + jnp.stack(ac, axis=0) * h_in
        ug = _load_interleaved(ug_ref, h, 1, lc, pitch)
        _store_interleaved(zl_ref, h, hseq * jax.nn.gelu(ug), pitch)

    for q in range(LRU_HEADS // 2, LRU_HEADS):
        y_top.append(_dot(z_pool, _unpack_rows(w_out_ref[:POOL_WIDTH // 2, q * out_cols:(q + 1) * out_cols])))
    z_lru = _load_chunked(zl_ref, 0, n_lru, lc, pitch).astype(_BF16)
    y = jnp.concatenate(y_top, axis=1) + _dot(z_lru, _unpack_rows(w_out_ref[POOL_WIDTH // 2:, :]))
    resid_ref[slot] = DEEPNORM_ALPHA * x + y


def _const_spec(shape, single_buffer=False):
    nd = len(shape)
    kwargs = {"pipeline_mode": pl.Buffered(1)} if single_buffer else {}
    return pl.BlockSpec(shape, lambda *_: (0,) * nd, **kwargs)


def _mixer(x, w_in, conv_w, conv_b, w_ax, b_a, b_x, lam, w_pool, b_pool, pool_scale, w_out, g, b,
           cast_weights):
    B, S, D = x.shape
    T = MIXER_TILE
    n_seq = S // T
    n_steps = B * n_seq
    assert T % SUBLANES == 0 and T // SUBLANES >= POOL_HIST and (T // SUBLANES) % (2 * SUBLANES) == 0
    rows = SUBLANES * (T // SUBLANES + SUBLANES)

    def cur(i):
        t = jnp.minimum(i, n_steps - 1)
        return t // n_seq, t % n_seq

    def prev(i):
        t = jnp.maximum(i - 1, 0)
        return t // n_seq, t % n_seq

    tile = pl.BlockSpec((1, T, D), lambda i: (*cur(i), 0))
    chunk_in, chunk_out = [], []
    for w in cast_weights:
        assert w.shape[0] % (2 * SUBLANES * n_steps) == 0
        rows_w = w.shape[0] // n_steps
        chunk_in.append(pl.BlockSpec((rows_w, w.shape[1]), lambda i: (jnp.minimum(i, n_steps - 1), 0)))
        chunk_out.append(pl.BlockSpec((rows_w // 2, w.shape[1]), lambda i: (jnp.minimum(i, n_steps - 1), 0)))
    return pl.pallas_call(
        functools.partial(_mixer_kernel, n_seq=n_seq),
        grid=(n_steps + 1,),
        in_specs=[
            tile,
            _const_spec(w_in.shape, True),
            _const_spec(conv_w.shape), _const_spec(conv_b.shape),
            _const_spec(w_ax.shape, True),
            _const_spec(b_a.shape), _const_spec(b_x.shape), _const_spec(lam.shape),
            _const_spec(w_pool.shape, True),
            _const_spec(b_pool.shape), _const_spec(pool_scale.shape),
            _const_spec(w_out.shape, True),
            _const_spec(g.shape), _const_spec(b.shape),
        ] + chunk_in,
        out_specs=[pl.BlockSpec((1, T, D), lambda i: (*prev(i), 0))] + chunk_out,
        out_shape=[jax.ShapeDtypeStruct(x.shape, _F32)]
        + [jax.ShapeDtypeStruct((w.shape[0] // 2, w.shape[1]), jnp.uint32) for w in cast_weights],
        scratch_shapes=[
            pltpu.VMEM((POOL_WIDTH // LANES, rows, LANES), _F32),
            pltpu.VMEM((LRU_WIDTH // LANES, rows, LANES), _F32),
            pltpu.VMEM((LRU_WIDTH // LANES, rows, LANES), _F32),
            pltpu.VMEM((POOL_WIDTH // LANES, rows, LANES), _F32),
            pltpu.VMEM((LRU_WIDTH // LANES, rows, LANES), _F32),
            pltpu.VMEM((POOL_HIST, SUBLANES, POOL_WIDTH), _F32),
            pltpu.VMEM((CONV_WIDTH - 1, SUBLANES, LRU_WIDTH), _F32),
            pltpu.VMEM((SUBLANES, LRU_WIDTH), _F32),
            pltpu.VMEM((2, T, D), _F32),
        ],
        compiler_params=pltpu.CompilerParams(
            dimension_semantics=("arbitrary",), vmem_limit_bytes=VMEM_LIMIT_BYTES),
        name="mixer",
    )(x, w_in, conv_w, conv_b, w_ax, b_a, b_x, lam, w_pool, b_pool, pool_scale, w_out, g, b, *cast_weights)


def _kv_kernel(mem_ref, wk_ref, wv_ref, *rest):
    n_cast = (len(rest) - 2) // 2
    cast_in, k_ref, v_ref, cast_out = rest[:n_cast], rest[n_cast], rest[n_cast + 1], rest[n_cast + 2:]
    for src, dst in zip(cast_in, cast_out):
        dst[...] = _pack_rows(src[...])
    m = mem_ref[...].astype(_BF16)
    k_ref[...] = _pack_rows(_dot(m, wk_ref[...].astype(_BF16)))
    v_ref[...] = _pack_rows(_dot(m, wv_ref[...].astype(_BF16)))


def _kv_proj(mem2d, w_k, w_v, cast_weights):
    M, D = mem2d.shape
    tn = KV_TILE_N
    n_steps = D // tn
    wspec = pl.BlockSpec((D, tn), lambda j: (0, j))
    ospec = pl.BlockSpec((M // 2, tn), lambda j: (0, j))
    chunk_in, chunk_out = [], []
    for w in cast_weights:
        assert w.shape[0] % (2 * SUBLANES * n_steps) == 0
        rows_w = w.shape[0] // n_steps
        chunk_in.append(pl.BlockSpec((rows_w, w.shape[1]), lambda j: (j, 0)))
        chunk_out.append(pl.BlockSpec((rows_w // 2, w.shape[1]), lambda j: (j, 0)))
    return pl.pallas_call(
        _kv_kernel,
        grid=(n_steps,),
        in_specs=[_const_spec(mem2d.shape, True), wspec, wspec] + chunk_in,
        out_specs=[ospec, ospec] + chunk_out,
        out_shape=[jax.ShapeDtypeStruct((M // 2, D), jnp.uint32)] * 2
        + [jax.ShapeDtypeStruct((w.shape[0] // 2, w.shape[1]), jnp.uint32) for w in cast_weights],
        compiler_params=pltpu.CompilerParams(
            dimension_semantics=("arbitrary",), vmem_limit_bytes=VMEM_LIMIT_BYTES),
        name="kv_proj",
    )(mem2d, w_k, w_v, *cast_weights)


def _xattn_kernel(x_ref, k_ref, v_ref, wq_ref, wo_ref, g_ref, b_ref, o_ref, resid_ref):
    i = pl.program_id(0)
    n_tiles = pl.num_programs(0) - 1
    slot = i % 2

    @pl.when(i == 0)
    def _():
        resid_ref[1] = jnp.zeros(resid_ref.shape[1:], _F32)

    def finish_previous():
        return _normalize_previous(resid_ref, slot, g_ref, b_ref, o_ref)

    @pl.when(i < n_tiles)
    def _():
        zeros = finish_previous()
        x = x_ref[0]
        xb = x.astype(_BF16)
        T = x.shape[0]
        scale = XATTN_HEAD_DIM ** -0.5
        heads = [slice(h * XATTN_HEAD_DIM, (h + 1) * XATTN_HEAD_DIM) for h in range(XATTN_HEADS)]

        blocks_per_head = XATTN_HEAD_DIM // ANCHOR_COLS
        anchors = [[] for _ in range(XATTN_HEADS * blocks_per_head)]
        for c, z in enumerate(zeros):
            anchors[min(c + 1, len(anchors) - 1)].append(z)

        def query(h):
            q = _dot(xb, _unpack_rows(wq_ref[:, heads[h]]))
            cols = []
            for blk in range(blocks_per_head):
                qb = q[:, blk * ANCHOR_COLS:(blk + 1) * ANCHOR_COLS]
                for z in anchors[h * blocks_per_head + blk]:
                    qb = qb + jnp.tile(z, (T // SUBLANES, ANCHOR_COLS // LANES))
                cols.append(qb)
            return jnp.concatenate(cols, axis=1).astype(_BF16)

        def probs(q, h):
            sc = lax.dot_general(q, _unpack_rows(k_ref[0, :, heads[h]]), (((1,), (1,)), ((), ())),
                                 preferred_element_type=_F32) * scale
            e = jnp.exp(sc - jnp.max(sc, axis=-1, keepdims=True))
            return (e * (1.0 / jnp.sum(e, axis=-1, keepdims=True))).astype(_BF16)

        def attend(p, h):
            return _dot(p, _unpack_rows(v_ref[0, :, heads[h]])).astype(_BF16)

        half_rows = wo_ref.shape[0] // 2
        q0 = query(0)
        q1 = query(1)
        p0 = probs(q0, 0)
        q2 = query(2)
        o0 = attend(p0, 0)
        p1 = probs(q1, 1)
        q3 = query(3)
        o1 = attend(p1, 1)
        p2 = probs(q2, 2)
        y = _dot(jnp.concatenate([o0, o1], axis=1), _unpack_rows(wo_ref[:half_rows, :]))
        o2 = attend(p2, 2)
        p3 = probs(q3, 3)
        o3 = attend(p3, 3)
        y = y + _dot(jnp.concatenate([o2, o3], axis=1), _unpack_rows(wo_ref[half_rows:, :]))
        resid_ref[slot] = DEEPNORM_ALPHA * x + y

    @pl.when(i == n_tiles)
    def _():
        finish_previous()


def _xattn(x, k, v, w_q, w_o, g, b):
    B, S, D = x.shape
    T = XATTN_TILE
    n_seq = S // T
    n_tiles = B * n_seq

    def cur(i):
        t = jnp.minimum(i, n_tiles - 1)
        return t // n_seq, t % n_seq

    def prev(i):
        t = jnp.maximum(i - 1, 0)
        return t // n_seq, t % n_seq

    kvspec = pl.BlockSpec((1, N_MEM // 2, D), lambda i: (cur(i)[0], 0, 0))
    return pl.pallas_call(
        _xattn_kernel,
        grid=(n_tiles + 1,),
        in_specs=[pl.BlockSpec((1, T, D), lambda i: (*cur(i), 0)), kvspec, kvspec,
                  _const_spec(w_q.shape, True), _const_spec(w_o.shape, True),
                  _const_spec(g.shape), _const_spec(b.shape)],
        out_specs=pl.BlockSpec((1, T, D), lambda i: (*prev(i), 0)),
        out_shape=jax.ShapeDtypeStruct(x.shape, _F32),
        scratch_shapes=[pltpu.VMEM((2, T, D), _F32)],
        compiler_params=pltpu.CompilerParams(
            dimension_semantics=("arbitrary",), vmem_limit_bytes=VMEM_LIMIT_BYTES),
        name="xattn",
    )(x, k, v, w_q, w_o, g, b)


def _mlp_kernel(x_ref, w1_ref, w2_ref, g_ref, b_ref, o_ref, xb_ref, acc_ref, *, n_f):
    i, j = pl.program_id(0), pl.program_id(1)
    n_tiles = pl.num_programs(0) - 1
    slot = i % 2
    rows = x_ref.shape[0] // n_f

    @pl.when((i == 0) & (j == 0))
    def _():
        acc_ref[1] = jnp.zeros(acc_ref.shape[1:], _F32)

    def finish_previous_rows():
        r0 = pl.multiple_of(j * rows, rows)
        o_ref[pl.ds(r0, rows), :] = _layer_norm(acc_ref[1 - slot, pl.ds(r0, rows), :], g_ref[...], b_ref[...])

    @pl.when(i < n_tiles)
    def _():
        @pl.when(j == 0)
        def _():
            xb_ref[...] = x_ref[...].astype(_BF16)
            acc_ref[slot] = DEEPNORM_ALPHA * x_ref[...]

        finish_previous_rows()
        h = jnp.square(jnp.maximum(_dot(xb_ref[...], _unpack_rows(w1_ref[...])), 0.0))
        acc_ref[slot] += _dot(h.astype(_BF16), _unpack_rows(w2_ref[...]))

    @pl.when(i == n_tiles)
    def _():
        finish_previous_rows()


def _mlp(x2d, w1, w2, g, b):
    M, D = x2d.shape
    F = w1.shape[1]
    tm, tf = MLP_TILE_M, MLP_TILE_F
    n_tiles, n_f = M // tm, F // tf
    assert tm % (n_f * SUBLANES) == 0

    def wblk(i, j):
        return jnp.where(i == n_tiles, n_f - 1, j)

    return pl.pallas_call(
        functools.partial(_mlp_kernel, n_f=n_f),
        grid=(n_tiles + 1, n_f),
        in_specs=[pl.BlockSpec((tm, D), lambda i, j: (jnp.minimum(i, n_tiles - 1), 0)),
                  pl.BlockSpec((D // 2, tf), lambda i, j: (0, wblk(i, j))),
                  pl.BlockSpec((tf // 2, D), lambda i, j: (wblk(i, j), 0)),
                  _const_spec(g.shape), _const_spec(b.shape)],
        out_specs=pl.BlockSpec((tm, D), lambda i, j: (jnp.maximum(i - 1, 0), 0)),
        out_shape=jax.ShapeDtypeStruct((M, D), _F32),
        scratch_shapes=[pltpu.VMEM((tm, D), _BF16), pltpu.VMEM((2, tm, D), _F32)],
        compiler_params=pltpu.CompilerParams(
            dimension_semantics=("arbitrary", "arbitrary"),
            vmem_limit_bytes=VMEM_LIMIT_BYTES),
        name="mlp",
    )(x2d, w1, w2, g, b)


def kernel(x, mem, w_in, conv_w, conv_b, w_a, b_a, w_x, b_x, lru_lambda, w_pool, b_pool, pool_scale,
           w_out, ln1_g, ln1_b, w_q, w_k, w_v, w_o, ln2_g, ln2_b, w_ff1, w_ff2, ln3_g, ln3_b):
    B, S, D = x.shape
    row = lambda p: p.reshape(1, -1)
    for l in range(DEPTH):
        w_ax = jnp.concatenate([w_a[l], w_x[l]], axis=-1).astype(_BF16)
        k, v, w_in_b, w_out_b = _kv_proj(mem.reshape(B * N_MEM, D), w_k[l], w_v[l], (w_in[l], w_out[l]))
        x, wq_b, wo_b, w1_b, w2_b = _mixer(
            x, w_in_b, conv_w[l], row(conv_b[l]), w_ax,
            row(b_a[l]), row(b_x[l]), row(lru_lambda[l]),
            w_pool[l].astype(_BF16), row(b_pool[l]), row(pool_scale[l]),
            w_out_b, row(ln1_g[l]), row(ln1_b[l]),
            (w_q[l], w_o[l], w_ff1[l], w_ff2[l]))
        x = _xattn(x, k.reshape(B, N_MEM // 2, D), v.reshape(B, N_MEM // 2, D),
                   wq_b, wo_b, row(ln2_g[l]), row(ln2_b[l]))
        x = _mlp(x.reshape(B * S, D), w1_b, w2_b, row(ln3_g[l]), row(ln3_b[l])).reshape(B, S, D)
    return x
```

```python
import functools

import jax
import jax.numpy as jnp
from jax import lax
from jax.experimental import pallas as pl
from jax.experimental.pallas import tpu as pltpu

D_MODEL = 2048
POOL_WIDTH = 1024
LRU_WIDTH = 1024
POOL_WINDOWS = (2, 4, 8, 16)
POOL_GROUP = 256
LRU_HEADS = 8
LRU_HEAD_DIM = 128
CONV_WIDTH = 4
LRU_C = 8.0
N_MEM = 256
XATTN_HEADS = 4
XATTN_HEAD_DIM = 512
D_FF = 4 * D_MODEL
LN_EPS = 1e-5
DEPTH = 1
DEEPNORM_ALPHA = (2.0 * DEPTH) ** 0.25

POOL_HIST = 16
SUBLANES = 8
LANES = 128

MIXER_TILE = 256
XATTN_TILE = 512
ANCHOR_COLS = 256
MLP_TILE_M = 512
MLP_TILE_F = 1024
KV_TILE_N = 256
VMEM_LIMIT_BYTES = 56 * 1024 * 1024

_F32 = jnp.float32
_BF16 = jnp.bfloat16


def _layer_norm(v, g, b):
    mu = jnp.mean(v, axis=-1, keepdims=True)
    c = v - mu
    var = jnp.mean(jnp.square(c), axis=-1, keepdims=True)
    return c * lax.rsqrt(var + LN_EPS) * g + b


def _dot(a, b):
    return jnp.dot(a, b, preferred_element_type=_F32)


def _pack_rows(v):
    return pltpu.bitcast(v.astype(_BF16), jnp.uint32)


def _unpack_rows(w):
    return pltpu.bitcast(w, _BF16)


def _ordering_zero(v):
    rows, cols = v.shape
    t = jnp.sum(v.reshape(rows // SUBLANES, SUBLANES, cols), axis=0)
    t = functools.reduce(lambda a, b: a + b, [t[:, c:c + LANES] for c in range(0, cols, LANES)])
    return jnp.minimum(jnp.abs(t), 0.0)


def _normalize_previous(resid_ref, slot, g_ref, b_ref, o_ref):
    T, D = resid_ref.shape[1:]
    chunk = T // (D // ANCHOR_COLS)
    zeros = []
    for r0 in range(0, T, chunk):
        out = _layer_norm(resid_ref[1 - slot, r0:r0 + chunk, :], g_ref[...], b_ref[...])
        o_ref[0, r0:r0 + chunk, :] = out
        zeros.append(_ordering_zero(out))
    return zeros


def _store_anchored(resid_ref, slot, resid, zeros):
    T = resid.shape[0]
    for c, zero in enumerate(zeros):
        c0 = c * ANCHOR_COLS
        resid_ref[slot, :, c0:c0 + LANES] = resid[:, c0:c0 + LANES] + jnp.tile(zero, (T // SUBLANES, 1))
        resid_ref[slot, :, c0 + LANES:c0 + ANCHOR_COLS] = resid[:, c0 + LANES:c0 + ANCHOR_COLS]


def _tile_zeros(zeros, rows, cols):
    width = cols // len(zeros)
    return jnp.concatenate([jnp.tile(z, (rows // SUBLANES, width // LANES)) for z in zeros], axis=1)


def _store_chunked(dst_ref, slab0, val, lc, pitch):
    for c in range(val.shape[1] // LANES):
        for j in range(SUBLANES):
            dst_ref[slab0 + c, j * pitch:j * pitch + lc, :] = (
                val[j * lc:(j + 1) * lc, c * LANES:(c + 1) * LANES])


def _load_chunked(src_ref, slab0, n_slabs, lc, pitch):
    return jnp.concatenate(
        [jnp.concatenate([src_ref[slab0 + c, j * pitch:j * pitch + lc, :] for j in range(SUBLANES)], axis=0)
         for c in range(n_slabs)], axis=-1)


def _load_interleaved(src_ref, slab0, n_slabs, lc, pitch):
    cols = [jnp.stack([src_ref[slab0 + c, pl.ds(k, SUBLANES, stride=pitch), :] for k in range(lc)], axis=0)
            for c in range(n_slabs)]
    return jnp.concatenate(cols, axis=-1)


def _store_interleaved(dst_ref, slab0, val3, pitch):
    for c in range(val3.shape[2] // LANES):
        for k in range(val3.shape[0]):
            dst_ref[slab0 + c, pl.ds(k, SUBLANES, stride=pitch), :] = val3[k, :, c * LANES:(c + 1) * LANES]


def _history_rows(cur_tail, prev_tail):
    sub = lax.broadcasted_iota(jnp.int32, (1,) + cur_tail.shape[1:], 1)
    return jnp.where(sub == 0, pltpu.roll(prev_tail, 1, 1), pltpu.roll(cur_tail, 1, 1))


def _mixer_kernel(x_ref, w_in_ref, conv_w_ref, conv_b_ref, w_ax_ref, b_a_ref, b_x_ref, lam_ref,
                  w_pool_ref, b_pool_ref, pool_scale_ref, w_out_ref, g_ref, b_ref, *rest, n_seq):
    n_cast = (len(rest) - 9) // 2
    cast_in, o_ref, cast_out = rest[:n_cast], rest[n_cast], rest[n_cast + 1:2 * n_cast + 1]
    up_ref, ul_ref, zp_ref, zl_ref, pool_hist, conv_hist, h_carry, resid_ref = rest[2 * n_cast + 1:]
    T = x_ref.shape[1]
    lc = T // SUBLANES
    pitch = lc + SUBLANES
    i = pl.program_id(0)
    n_tiles = pl.num_programs(0) - 1
    s = jnp.minimum(i, n_tiles - 1) % n_seq
    slot = i % 2

    @pl.when(i == 0)
    def _():
        resid_ref[1] = jnp.zeros(resid_ref.shape[1:], _F32)

    def finish_previous():
        return _normalize_previous(resid_ref, slot, g_ref, b_ref, o_ref)

    @pl.when(i == n_tiles)
    def _():
        finish_previous()

    @pl.when(i < n_tiles)
    def _():
        _mixer_tile(s, slot, finish_previous, x_ref, w_in_ref, conv_w_ref, conv_b_ref, w_ax_ref, b_a_ref, b_x_ref,
                    lam_ref, w_pool_ref, b_pool_ref, pool_scale_ref, w_out_ref, cast_in, cast_out,
                    up_ref, ul_ref, zp_ref, zl_ref, pool_hist, conv_hist, h_carry, resid_ref, lc, pitch)


def _mixer_tile(s, slot, finish_previous, x_ref, w_in_ref, conv_w_ref, conv_b_ref, w_ax_ref, b_a_ref, b_x_ref,
                lam_ref, w_pool_ref, b_pool_ref, pool_scale_ref, w_out_ref, cast_in, cast_out,
                up_ref, ul_ref, zp_ref, zl_ref, pool_hist, conv_hist, h_carry, resid_ref, lc, pitch):
    T = x_ref.shape[1]

    @pl.when(s == 0)
    def _():
        pool_hist[...] = jnp.zeros_like(pool_hist)
        conv_hist[...] = jnp.zeros_like(conv_hist)
        h_carry[...] = jnp.zeros_like(h_carry)

    zeros = finish_previous()
    for src, dst in zip(cast_in, cast_out):
        dst[...] = _pack_rows(src[...])

    x = x_ref[0]
    xb = x.astype(_BF16)
    n_pool, n_lru = POOL_WIDTH // LANES, LRU_WIDTH // LANES
    half = len(zeros) // 2
    u_pool = _dot(xb, _unpack_rows(w_in_ref[:, :POOL_WIDTH])) + _tile_zeros(zeros[:half], T, POOL_WIDTH)
    _store_chunked(up_ref, 0, u_pool, lc, pitch)
    up = _load_interleaved(up_ref, 0, n_pool, lc, pitch)
    u_lru = (_dot(xb, _unpack_rows(w_in_ref[:, POOL_WIDTH:POOL_WIDTH + LRU_WIDTH]))
             + _tile_zeros(zeros[half:], T, LRU_WIDTH))
    _store_chunked(ul_ref, 0, u_lru, lc, pitch)

    kk = lax.broadcasted_iota(jnp.int32, (POOL_HIST, SUBLANES, LANES), 0)
    jj = lax.broadcasted_iota(jnp.int32, (POOL_HIST, SUBLANES, LANES), 1)
    t_head = s * T + jj * lc + kk

    ext = jnp.concatenate([_history_rows(up[lc - POOL_HIST:], pool_hist[...]), up], axis=0)
    pool_hist[...] = up[lc - POOL_HIST:]
    s2 = ext[1:] + ext[:-1]
    s2r = s2[:, :, POOL_GROUP:]
    s4 = s2r[2:] + s2r[:-2]
    s4r = s4[:, :, POOL_GROUP:]
    s8 = s4r[4:] + s4r[:-4]
    s8r = s8[:, :, POOL_GROUP:]
    s16 = s8r[8:] + s8r[:-8]
    sums = (s2[15:, :, :POOL_GROUP], s4[13:, :, :POOL_GROUP], s8[9:, :, :POOL_GROUP], s16[1:])
    y_pool = []
    for g, w in enumerate(POOL_WINDOWS):
        cs = slice(g * POOL_GROUP, (g + 1) * POOL_GROUP)
        inv_head = 1.0 / jnp.minimum(t_head + 1, w).astype(_F32)
        inv_head = jnp.concatenate([inv_head] * (POOL_GROUP // LANES), axis=-1)
        mean = jnp.concatenate([sums[g][:POOL_HIST] * inv_head, sums[g][POOL_HIST:] * (1.0 / w)], axis=0)
        mixed = mean - up[:, :, cs]
        yg = _dot(mixed.reshape(T, POOL_GROUP).astype(_BF16), w_pool_ref[g]).reshape(lc, SUBLANES, POOL_GROUP)
        y_pool.append((yg + b_pool_ref[:, cs]) * pool_scale_ref[:, cs])
    _store_interleaved(zp_ref, 0, jnp.concatenate(y_pool, axis=-1), pitch)

    z_pool = _load_chunked(zp_ref, 0, n_pool, lc, pitch).astype(_BF16)
    n_hist = CONV_WIDTH - 1
    sub = lax.broadcasted_iota(jnp.int32, (SUBLANES, LRU_HEAD_DIM), 0)
    first = t_head[0:1] == 0
    gate_cols = 2 * LRU_HEAD_DIM
    out_cols = D_MODEL // LRU_HEADS
    y_top, z_lru = [], []
    for h in range(LRU_HEADS):
        hs = slice(h * LRU_HEAD_DIM, (h + 1) * LRU_HEAD_DIM)
        ul = _load_interleaved(ul_ref, h, 1, lc, pitch)
        ext2 = jnp.concatenate([_history_rows(ul[lc - n_hist:], conv_hist[:, :, hs]), ul], axis=0)
        conv_hist[:, :, hs] = ul[lc - n_hist:]
        xc = ext2[0:lc] * conv_w_ref[0:1, hs]
        for k in range(1, CONV_WIDTH):
            xc = xc + ext2[k:k + lc] * conv_w_ref[k:k + 1, hs]
        xc = xc + conv_b_ref[:, hs]

        pre = _dot(xc.reshape(T, LRU_HEAD_DIM).astype(_BF16), w_ax_ref[h])
        pre = pre.reshape(lc, SUBLANES, 2 * LRU_HEAD_DIM)
        r = jax.nn.sigmoid(pre[:, :, :LRU_HEAD_DIM] + b_a_ref[:, hs])
        i = jax.nn.sigmoid(pre[:, :, LRU_HEAD_DIM:] + b_x_ref[:, hs])

        if h % 2 == 0:
            c0 = POOL_WIDTH + LRU_WIDTH + h * LRU_HEAD_DIM
            u_gate = _dot(xb, _unpack_rows(w_in_ref[:, c0:c0 + gate_cols]))
        else:
            c0 = (h // 2) * out_cols
            y_top.append(_dot(z_pool, _unpack_rows(w_out_ref[:POOL_WIDTH // 2, c0:c0 + out_cols])))

        log_a = (-LRU_C * r) * jax.nn.softplus(-lam_ref[:, hs])
        a = jnp.exp(log_a)
        u = -jnp.tanh(log_a) * (a * a + 1.0)
        mult = jnp.where(u == 0.0, 0.0, u * lax.rsqrt(u))
        mult = jnp.concatenate([jnp.where(first, 1.0, mult[0:1]), mult[1:]], axis=0)
        bv = mult * (i * xc)

        hl, ac = [bv[0]], [a[0]]
        for k in range(1, lc):
            hl.append(a[k] * hl[-1] + bv[k])
            ac.append(a[k] * ac[-1])
        c_a, c_b = ac[-1], hl[-1]
        for d in (1, 2, 4):
            a_s = jnp.where(sub < d, 1.0, pltpu.roll(c_a, d, 0))
            b_s = jnp.where(sub < d, 0.0, pltpu.roll(c_b, d, 0))
            c_b = c_a * b_s + c_b
            c_a = c_a * a_s
        h_prev = h_carry[:, hs]
        h_end = c_b + c_a * h_prev
        h_in = jnp.where(sub == 0, h_prev, pltpu.roll(h_end, 1, 0))
        h_carry[:, hs] = jnp.broadcast_to(h_end[SUBLANES - 1:SUBLANES, :], (SUBLANES, LRU_HEAD_DIM))
        hseq = jnp.stack(hl, axis=0) + jnp.stack(ac, axis=0) * h_in
        _store_interleaved(zl_ref, h, hseq, pitch)
        gate = jax.nn.gelu(u_gate[:, (h % 2) * LRU_HEAD_DIM:(h % 2 + 1) * LRU_HEAD_DIM])
        z_lru.append((_load_chunked(zl_ref, h, 1, lc, pitch) * gate).astype(_BF16))

    for q in range(LRU_HEADS // 2, LRU_HEADS):
        y_top.append(_dot(z_pool, _unpack_rows(w_out_ref[:POOL_WIDTH // 2, q * out_cols:(q + 1) * out_cols])))
    y = (jnp.concatenate(y_top, axis=1)
         + _dot(jnp.concatenate(z_lru, axis=1), _unpack_rows(w_out_ref[POOL_WIDTH // 2:, :])))
    resid_ref[slot] = DEEPNORM_ALPHA * x + y


def _const_spec(shape, single_buffer=False):
    nd = len(shape)
    kwargs = {"pipeline_mode": pl.Buffered(1)} if single_buffer else {}
    return pl.BlockSpec(shape, lambda *_: (0,) * nd, **kwargs)


def _mixer(x, w_in, conv_w, conv_b, w_ax, b_a, b_x, lam, w_pool, b_pool, pool_scale, w_out, g, b,
           cast_weights):
    B, S, D = x.shape
    T = MIXER_TILE
    n_seq = S // T
    n_steps = B * n_seq
    assert T % SUBLANES == 0 and T // SUBLANES >= POOL_HIST and (T // SUBLANES) % (2 * SUBLANES) == 0
    rows = SUBLANES * (T // SUBLANES + SUBLANES)

    def cur(i):
        t = jnp.minimum(i, n_steps - 1)
        return t // n_seq, t % n_seq

    def prev(i):
        t = jnp.maximum(i - 1, 0)
        return t // n_seq, t % n_seq

    tile = pl.BlockSpec((1, T, D), lambda i: (*cur(i), 0))
    chunk_in, chunk_out = [], []
    for w in cast_weights:
        assert w.shape[0] % (2 * SUBLANES * n_steps) == 0
        rows_w = w.shape[0] // n_steps
        chunk_in.append(pl.BlockSpec((rows_w, w.shape[1]), lambda i: (jnp.minimum(i, n_steps - 1), 0)))
        chunk_out.append(pl.BlockSpec((rows_w // 2, w.shape[1]), lambda i: (jnp.minimum(i, n_steps - 1), 0)))
    return pl.pallas_call(
        functools.partial(_mixer_kernel, n_seq=n_seq),
        grid=(n_steps + 1,),
        in_specs=[
            tile,
            _const_spec(w_in.shape, True),
            _const_spec(conv_w.shape), _const_spec(conv_b.shape),
            _const_spec(w_ax.shape, True),
            _const_spec(b_a.shape), _const_spec(b_x.shape), _const_spec(lam.shape),
            _const_spec(w_pool.shape, True),
            _const_spec(b_pool.shape), _const_spec(pool_scale.shape),
            _const_spec(w_out.shape, True),
            _const_spec(g.shape), _const_spec(b.shape),
        ] + chunk_in,
        out_specs=[pl.BlockSpec((1, T, D), lambda i: (*prev(i), 0))] + chunk_out,
        out_shape=[jax.ShapeDtypeStruct(x.shape, _F32)]
        + [jax.ShapeDtypeStruct((w.shape[0] // 2, w.shape[1]), jnp.uint32) for w in cast_weights],
        scratch_shapes=[
            pltpu.VMEM((POOL_WIDTH // LANES, rows, LANES), _F32),
            pltpu.VMEM((LRU_WIDTH // LANES, rows, LANES), _F32),
            pltpu.VMEM((POOL_WIDTH // LANES, rows, LANES), _F32),
            pltpu.VMEM((LRU_WIDTH // LANES, rows, LANES), _F32),
            pltpu.VMEM((POOL_HIST, SUBLANES, POOL_WIDTH), _F32),
            pltpu.VMEM((CONV_WIDTH - 1, SUBLANES, LRU_WIDTH), _F32),
            pltpu.VMEM((SUBLANES, LRU_WIDTH), _F32),
            pltpu.VMEM((2, T, D), _F32),
        ],
        compiler_params=pltpu.CompilerParams(
            dimension_semantics=("arbitrary",), vmem_limit_bytes=VMEM_LIMIT_BYTES),
        name="mixer",
    )(x, w_in, conv_w, conv_b, w_ax, b_a, b_x, lam, w_pool, b_pool, pool_scale, w_out, g, b, *cast_weights)


def _kv_kernel(mem_ref, wk_ref, wv_ref, *rest):
    n_cast = (len(rest) - 2) // 2
    cast_in, k_ref, v_ref, cast_out = rest[:n_cast], rest[n_cast], rest[n_cast + 1], rest[n_cast + 2:]
    for src, dst in zip(cast_in, cast_out):
        dst[...] = _pack_rows(src[...])
    m = mem_ref[...].astype(_BF16)
    k_ref[...] = _pack_rows(_dot(m, wk_ref[...].astype(_BF16)))
    v_ref[...] = _pack_rows(_dot(m, wv_ref[...].astype(_BF16)))


def _kv_proj(mem2d, w_k, w_v, cast_weights):
    M, D = mem2d.shape
    tn = KV_TILE_N
    n_steps = D // tn
    wspec = pl.BlockSpec((D, tn), lambda j: (0, j))
    ospec = pl.BlockSpec((M // 2, tn), lambda j: (0, j))
    chunk_in, chunk_out = [], []
    for w in cast_weights:
        assert w.shape[0] % (2 * SUBLANES * n_steps) == 0
        rows_w = w.shape[0] // n_steps
        chunk_in.append(pl.BlockSpec((rows_w, w.shape[1]), lambda j: (j, 0)))
        chunk_out.append(pl.BlockSpec((rows_w // 2, w.shape[1]), lambda j: (j, 0)))
    return pl.pallas_call(
        _kv_kernel,
        grid=(n_steps,),
        in_specs=[_const_spec(mem2d.shape, True), wspec, wspec] + chunk_in,
        out_specs=[ospec, ospec] + chunk_out,
        out_shape=[jax.ShapeDtypeStruct((M // 2, D), jnp.uint32)] * 2
        + [jax.ShapeDtypeStruct((w.shape[0] // 2, w.shape[1]), jnp.uint32) for w in cast_weights],
        compiler_params=pltpu.CompilerParams(
            dimension_semantics=("arbitrary",), vmem_limit_bytes=VMEM_LIMIT_BYTES),
        name="kv_proj",
    )(mem2d, w_k, w_v, *cast_weights)


def _xattn_kernel(x_ref, k_ref, v_ref, wq_ref, wo_ref, g_ref, b_ref, o_ref, resid_ref):
    i = pl.program_id(0)
    n_tiles = pl.num_programs(0) - 1
    slot = i % 2

    @pl.when(i == 0)
    def _():
        resid_ref[1] = jnp.zeros(resid_ref.shape[1:], _F32)

    def finish_previous():
        return _normalize_previous(resid_ref, slot, g_ref, b_ref, o_ref)

    @pl.when(i < n_tiles)
    def _():
        zeros = finish_previous()
        x = x_ref[0]
        xb = x.astype(_BF16)
        T = x.shape[0]
        scale = XATTN_HEAD_DIM ** -0.5
        heads = [slice(h * XATTN_HEAD_DIM, (h + 1) * XATTN_HEAD_DIM) for h in range(XATTN_HEADS)]

        blocks_per_head = XATTN_HEAD_DIM // ANCHOR_COLS
        anchors = [[] for _ in range(XATTN_HEADS * blocks_per_head)]
        for c, z in enumerate(zeros):
            anchors[min(c + 1, len(anchors) - 1)].append(z)

        def query(h):
            q = _dot(xb, _unpack_rows(wq_ref[:, heads[h]]))
            cols = []
            for blk in range(blocks_per_head):
                qb = q[:, blk * ANCHOR_COLS:(blk + 1) * ANCHOR_COLS]
                for z in anchors[h * blocks_per_head + blk]:
                    qb = qb + jnp.tile(z, (T // SUBLANES, ANCHOR_COLS // LANES))
                cols.append(qb)
            return jnp.concatenate(cols, axis=1).astype(_BF16)

        def probs(q, h):
            sc = lax.dot_general(q, _unpack_rows(k_ref[0, :, heads[h]]), (((1,), (1,)), ((), ())),
                                 preferred_element_type=_F32) * scale
            e = jnp.exp(sc - jnp.max(sc, axis=-1, keepdims=True))
            return (e * (1.0 / jnp.sum(e, axis=-1, keepdims=True))).astype(_BF16)

        def attend(p, h):
            return _dot(p, _unpack_rows(v_ref[0, :, heads[h]])).astype(_BF16)

        half_rows = wo_ref.shape[0] // 2
        q0 = query(0)
        q1 = query(1)
        p0 = probs(q0, 0)
        q2 = query(2)
        o0 = attend(p0, 0)
        p1 = probs(q1, 1)
        q3 = query(3)
        o1 = attend(p1, 1)
        p2 = probs(q2, 2)
        y = _dot(jnp.concatenate([o0, o1], axis=1), _unpack_rows(wo_ref[:half_rows, :]))
        o2 = attend(p2, 2)
        p3 = probs(q3, 3)
        o3 = attend(p3, 3)
        y = y + _dot(jnp.concatenate([o2, o3], axis=1), _unpack_rows(wo_ref[half_rows:, :]))
        resid_ref[slot] = DEEPNORM_ALPHA * x + y

    @pl.when(i == n_tiles)
    def _():
        finish_previous()


def _xattn(x, k, v, w_q, w_o, g, b):
    B, S, D = x.shape
    T = XATTN_TILE
    n_seq = S // T
    n_tiles = B * n_seq

    def cur(i):
        t = jnp.minimum(i, n_tiles - 1)
        return t // n_seq, t % n_seq

    def prev(i):
        t = jnp.maximum(i - 1, 0)
        return t // n_seq, t % n_seq

    kvspec = pl.BlockSpec((1, N_MEM // 2, D), lambda i: (cur(i)[0], 0, 0))
    return pl.pallas_call(
        _xattn_kernel,
        grid=(n_tiles + 1,),
        in_specs=[pl.BlockSpec((1, T, D), lambda i: (*cur(i), 0)), kvspec, kvspec,
                  _const_spec(w_q.shape, True), _const_spec(w_o.shape, True),
                  _const_spec(g.shape), _const_spec(b.shape)],
        out_specs=pl.BlockSpec((1, T, D), lambda i: (*prev(i), 0)),
        out_shape=jax.ShapeDtypeStruct(x.shape, _F32),
        scratch_shapes=[pltpu.VMEM((2, T, D), _F32)],
        compiler_params=pltpu.CompilerParams(
            dimension_semantics=("arbitrary",), vmem_limit_bytes=VMEM_LIMIT_BYTES),
        name="xattn",
    )(x, k, v, w_q, w_o, g, b)


def _mlp_kernel(x_ref, w1_ref, w2_ref, g_ref, b_ref, o_ref, xb_ref, acc_ref, *, n_f):
    i, j = pl.program_id(0), pl.program_id(1)
    n_tiles = pl.num_programs(0) - 1
    slot = i % 2
    rows = x_ref.shape[0] // n_f

    @pl.when((i == 0) & (j == 0))
    def _():
        acc_ref[1] = jnp.zeros(acc_ref.shape[1:], _F32)

    def finish_previous_rows():
        r0 = pl.multiple_of(j * rows, rows)
        o_ref[pl.ds(r0, rows), :] = _layer_norm(acc_ref[1 - slot, pl.ds(r0, rows), :], g_ref[...], b_ref[...])

    @pl.when(i < n_tiles)
    def _():
        @pl.when(j == 0)
        def _():
            xb_ref[...] = x_ref[...].astype(_BF16)
            acc_ref[slot] = DEEPNORM_ALPHA * x_ref[...]

        finish_previous_rows()
        h = jnp.square(jnp.maximum(_dot(xb_ref[...], _unpack_rows(w1_ref[...])), 0.0))
        acc_ref[slot] += _dot(h.astype(_BF16), _unpack_rows(w2_ref[...]))

    @pl.when(i == n_tiles)
    def _():
        finish_previous_rows()


def _mlp(x2d, w1, w2, g, b):
    M, D = x2d.shape
    F = w1.shape[1]
    tm, tf = MLP_TILE_M, MLP_TILE_F
    n_tiles, n_f = M // tm, F // tf
    assert tm % (n_f * SUBLANES) == 0

    def wblk(i, j):
        return jnp.where(i == n_tiles, n_f - 1, j)

    return pl.pallas_call(
        functools.partial(_mlp_kernel, n_f=n_f),
        grid=(n_tiles + 1, n_f),
        in_specs=[pl.BlockSpec((tm, D), lambda i, j: (jnp.minimum(i, n_tiles - 1), 0)),
                  pl.BlockSpec((D // 2, tf), lambda i, j: (0, wblk(i, j))),
                  pl.BlockSpec((tf // 2, D), lambda i, j: (wblk(i, j), 0)),
                  _const_spec(g.shape), _const_spec(b.shape)],
        out_specs=pl.BlockSpec((tm, D), lambda i, j: (jnp.maximum(i - 1, 0), 0)),
        out_shape=jax.ShapeDtypeStruct((M, D), _F32),
        scratch_shapes=[pltpu.VMEM((tm, D), _BF16), pltpu.VMEM((2, tm, D), _F32)],
        compiler_params=pltpu.CompilerParams(
            dimension_semantics=("arbitrary", "arbitrary"),
            vmem_limit_bytes=VMEM_LIMIT_BYTES),
        name="mlp",
    )(x2d, w1, w2, g, b)


def kernel(x, mem, w_in, conv_w, conv_b, w_a, b_a, w_x, b_x, lru_lambda, w_pool, b_pool, pool_scale,
           w_out, ln1_g, ln1_b, w_q, w_k, w_v, w_o, ln2_g, ln2_b, w_ff1, w_ff2, ln3_g, ln3_b):
    B, S, D = x.shape
    row = lambda p: p.reshape(1, -1)
    for l in range(DEPTH):
        w_ax = jnp.concatenate([w_a[l], w_x[l]], axis=-1).astype(_BF16)
        k, v, w_in_b, w_out_b = _kv_proj(mem.reshape(B * N_MEM, D), w_k[l], w_v[l], (w_in[l], w_out[l]))
        x, wq_b, wo_b, w1_b, w2_b = _mixer(
            x, w_in_b, conv_w[l], row(conv_b[l]), w_ax,
            row(b_a[l]), row(b_x[l]), row(lru_lambda[l]),
            w_pool[l].astype(_BF16), row(b_pool[l]), row(pool_scale[l]),
            w_out_b, row(ln1_g[l]), row(ln1_b[l]),
            (w_q[l], w_o[l], w_ff1[l], w_ff2[l]))
        x = _xattn(x, k.reshape(B, N_MEM // 2, D), v.reshape(B, N_MEM // 2, D),
                   wq_b, wo_b, row(ln2_g[l]), row(ln2_b[l]))
        x = _mlp(x.reshape(B * S, D), w1_b, w2_b, row(ln3_g[l]), row(ln3_b[l])).reshape(B, S, D)
    return x
```

```python
import functools

import jax
import jax.numpy as jnp
from jax import lax
from jax.experimental import pallas as pl
from jax.experimental.pallas import tpu as pltpu

D_MODEL = 2048
POOL_WIDTH = 1024
LRU_WIDTH = 1024
POOL_WINDOWS = (2, 4, 8, 16)
POOL_GROUP = 256
LRU_HEADS = 8
LRU_HEAD_DIM = 128
CONV_WIDTH = 4
LRU_C = 8.0
N_MEM = 256
XATTN_HEADS = 4
XATTN_HEAD_DIM = 512
D_FF = 4 * D_MODEL
LN_EPS = 1e-5
DEPTH = 1
DEEPNORM_ALPHA = (2.0 * DEPTH) ** 0.25

POOL_HIST = 16
SUBLANES = 8
LANES = 128

MIXER_TILE = 256
XATTN_TILE = 512
ANCHOR_COLS = 256
MLP_TILE_M = 512
MLP_TILE_F = 1024
KV_TILE_N = 256
VMEM_LIMIT_BYTES = 56 * 1024 * 1024

_F32 = jnp.float32
_BF16 = jnp.bfloat16


def _layer_norm(v, g, b):
    mu = jnp.mean(v, axis=-1, keepdims=True)
    c = v - mu
    var = jnp.mean(jnp.square(c), axis=-1, keepdims=True)
    return c * lax.rsqrt(var + LN_EPS) * g + b


def _dot(a, b):
    return jnp.dot(a, b, preferred_element_type=_F32)


def _pack_rows(v):
    return pltpu.bitcast(v.astype(_BF16), jnp.uint32)


def _unpack_rows(w):
    return pltpu.bitcast(w, _BF16)


def _ordering_zero(v):
    rows, cols = v.shape
    t = jnp.sum(v.reshape(rows // SUBLANES, SUBLANES, cols), axis=0)
    t = functools.reduce(lambda a, b: a + b, [t[:, c:c + LANES] for c in range(0, cols, LANES)])
    return jnp.minimum(jnp.abs(t), 0.0)


def _normalize_previous(resid_ref, slot, g_ref, b_ref, o_ref):
    T, D = resid_ref.shape[1:]
    chunk = T // (D // ANCHOR_COLS)
    zeros = []
    for r0 in range(0, T, chunk):
        out = _layer_norm(resid_ref[1 - slot, r0:r0 + chunk, :], g_ref[...], b_ref[...])
        o_ref[0, r0:r0 + chunk, :] = out
        zeros.append(_ordering_zero(out))
    return zeros


def _tile_zeros(zeros, rows, cols):
    width = cols // len(zeros)
    return jnp.concatenate([jnp.tile(z, (rows // SUBLANES, width // LANES)) for z in zeros], axis=1)


def _history_rows(cur_tail, prev_tail):
    sub = lax.broadcasted_iota(jnp.int32, (1,) + cur_tail.shape[1:], 1)
    return jnp.where(sub == 0, pltpu.roll(prev_tail, 1, 1), pltpu.roll(cur_tail, 1, 1))


def _tile_copies(hbm_ref, buf, sems, t, sl, n_seq, to_hbm):
    lc = buf.shape[1]
    b, s = t // n_seq, t % n_seq
    copies = []
    for j in range(SUBLANES):
        rows = hbm_ref.at[b, pl.ds(pl.multiple_of((s * SUBLANES + j) * lc, lc), lc), :]
        chunk = buf.at[sl, :, j, :]
        src, dst = (chunk, rows) if to_hbm else (rows, chunk)
        copies.append(pltpu.make_async_copy(src, dst, sems.at[sl, j]))
    return copies


def _mixer_kernel(x_hbm, w_in_ref, conv_w_ref, conv_b_ref, w_ax_ref, b_a_ref, b_x_ref, lam_ref,
                  w_pool_ref, b_pool_ref, pool_scale_ref, w_out_ref, g_ref, b_ref, *rest, n_seq):
    n_cast = (len(rest) - 9) // 2
    cast_in, o_hbm, cast_out = rest[:n_cast], rest[n_cast], rest[n_cast + 1:2 * n_cast + 1]
    xbuf, obuf, in_sem, out_sem, pool_hist, conv_hist, h_carry, resid_ref = rest[2 * n_cast + 1:]
    i = pl.program_id(0)
    n_tiles = pl.num_programs(0) - 1
    s = jnp.minimum(i, n_tiles - 1) % n_seq
    slot = i % 2

    def x_copies(t, sl):
        return _tile_copies(x_hbm, xbuf, in_sem, t, sl, n_seq, to_hbm=False)

    def out_copies(t, sl):
        return _tile_copies(o_hbm, obuf, out_sem, t, sl, n_seq, to_hbm=True)

    @pl.when(i == 0)
    def _():
        for c in x_copies(0, 0):
            c.start()
        resid_ref[1] = jnp.zeros(resid_ref.shape[1:], _F32)

    @pl.when(i < n_tiles)
    def _():
        for c in x_copies(i, slot):
            c.wait()

    @pl.when(i + 1 < n_tiles)
    def _():
        for c in x_copies(i + 1, 1 - slot):
            c.start()

    @pl.when(i >= 3)
    def _():
        for c in out_copies(i - 3, 1 - slot):
            c.wait()

    def finish_previous():
        T, D = resid_ref.shape[1:]
        chunk = T // (D // ANCHOR_COLS)
        zeros = []
        for r0 in range(0, T, chunk):
            out = _layer_norm(resid_ref[1 - slot, r0:r0 + chunk, :], g_ref[...], b_ref[...])
            obuf[1 - slot, r0 // SUBLANES:(r0 + chunk) // SUBLANES] = out.reshape(chunk // SUBLANES, SUBLANES, D)
            zeros.append(_ordering_zero(out))
        return zeros

    @pl.when(i == n_tiles)
    def _():
        finish_previous()

    @pl.when(i < n_tiles)
    def _():
        _mixer_tile(s, slot, finish_previous, xbuf, w_in_ref, conv_w_ref, conv_b_ref, w_ax_ref, b_a_ref, b_x_ref,
                    lam_ref, w_pool_ref, b_pool_ref, pool_scale_ref, w_out_ref, cast_in, cast_out,
                    pool_hist, conv_hist, h_carry, resid_ref)

    @pl.when(i >= 1)
    def _():
        for c in out_copies(i - 1, 1 - slot):
            c.start()

    @pl.when(i == n_tiles)
    def _():
        for c in out_copies(n_tiles - 2, slot) + out_copies(n_tiles - 1, 1 - slot):
            c.wait()


def _mixer_tile(s, slot, finish_previous, xbuf, w_in_ref, conv_w_ref, conv_b_ref, w_ax_ref, b_a_ref, b_x_ref,
                lam_ref, w_pool_ref, b_pool_ref, pool_scale_ref, w_out_ref, cast_in, cast_out,
                pool_hist, conv_hist, h_carry, resid_ref):
    lc, _, D = xbuf.shape[1:]
    T = lc * SUBLANES

    @pl.when(s == 0)
    def _():
        pool_hist[...] = jnp.zeros_like(pool_hist)
        conv_hist[...] = jnp.zeros_like(conv_hist)
        h_carry[...] = jnp.zeros_like(h_carry)

    zeros = finish_previous()
    for src, dst in zip(cast_in, cast_out):
        dst[...] = _pack_rows(src[...])

    x = xbuf[slot].reshape(T, D)
    xb = x.astype(_BF16)
    n_lru = LRU_WIDTH // LANES
    half = len(zeros) // 2
    up = _dot(xb, _unpack_rows(w_in_ref[:, :POOL_WIDTH])) + _tile_zeros(zeros[:half], T, POOL_WIDTH)
    up = up.reshape(lc, SUBLANES, POOL_WIDTH)
    u_lru = (_dot(xb, _unpack_rows(w_in_ref[:, POOL_WIDTH:POOL_WIDTH + LRU_WIDTH]))
             + _tile_zeros(zeros[half:], T, LRU_WIDTH)).reshape(lc, SUBLANES, LRU_WIDTH)

    kk = lax.broadcasted_iota(jnp.int32, (POOL_HIST, SUBLANES, LANES), 0)
    jj = lax.broadcasted_iota(jnp.int32, (POOL_HIST, SUBLANES, LANES), 1)
    t_head = s * T + jj * lc + kk

    ext = jnp.concatenate([_history_rows(up[lc - POOL_HIST:], pool_hist[...]), up], axis=0)
    pool_hist[...] = up[lc - POOL_HIST:]
    s2 = ext[1:] + ext[:-1]
    s2r = s2[:, :, POOL_GROUP:]
    s4 = s2r[2:] + s2r[:-2]
    s4r = s4[:, :, POOL_GROUP:]
    s8 = s4r[4:] + s4r[:-4]
    s8r = s8[:, :, POOL_GROUP:]
    s16 = s8r[8:] + s8r[:-8]
    sums = (s2[15:, :, :POOL_GROUP], s4[13:, :, :POOL_GROUP], s8[9:, :, :POOL_GROUP], s16[1:])
    y_pool = []
    for g, w in enumerate(POOL_WINDOWS):
        cs = slice(g * POOL_GROUP, (g + 1) * POOL_GROUP)
        inv_head = 1.0 / jnp.minimum(t_head + 1, w).astype(_F32)
        inv_head = jnp.concatenate([inv_head] * (POOL_GROUP // LANES), axis=-1)
        mean = jnp.concatenate([sums[g][:POOL_HIST] * inv_head, sums[g][POOL_HIST:] * (1.0 / w)], axis=0)
        mixed = mean - up[:, :, cs]
        yg = _dot(mixed.reshape(T, POOL_GROUP).astype(_BF16), w_pool_ref[g])
        y_pool.append(((yg + b_pool_ref[:, cs]) * pool_scale_ref[:, cs]).astype(_BF16))
    z_pool = jnp.concatenate(y_pool, axis=1)

    n_hist = CONV_WIDTH - 1
    sub = lax.broadcasted_iota(jnp.int32, (SUBLANES, LRU_HEAD_DIM), 0)
    first = t_head[0:1] == 0
    gate_cols = 2 * LRU_HEAD_DIM
    out_cols = D_MODEL // LRU_HEADS
    y_top, z_lru = [], []
    for h in range(LRU_HEADS):
        hs = slice(h * LRU_HEAD_DIM, (h + 1) * LRU_HEAD_DIM)
        ul = u_lru[:, :, hs]
        ext2 = jnp.concatenate([_history_rows(ul[lc - n_hist:], conv_hist[:, :, hs]), ul], axis=0)
        conv_hist[:, :, hs] = ul[lc - n_hist:]
        xc = ext2[0:lc] * conv_w_ref[0:1, hs]
        for k in range(1, CONV_WIDTH):
            xc = xc + ext2[k:k + lc] * conv_w_ref[k:k + 1, hs]
        xc = xc + conv_b_ref[:, hs]

        pre = _dot(xc.reshape(T, LRU_HEAD_DIM).astype(_BF16), w_ax_ref[h])
        pre = pre.reshape(lc, SUBLANES, 2 * LRU_HEAD_DIM)
        r = jax.nn.sigmoid(pre[:, :, :LRU_HEAD_DIM] + b_a_ref[:, hs])
        i = jax.nn.sigmoid(pre[:, :, LRU_HEAD_DIM:] + b_x_ref[:, hs])

        if h % 2 == 0:
            c0 = POOL_WIDTH + LRU_WIDTH + h * LRU_HEAD_DIM
            u_gate = _dot(xb, _unpack_rows(w_in_ref[:, c0:c0 + gate_cols])).reshape(lc, SUBLANES, gate_cols)
        else:
            c0 = (h // 2) * out_cols
            y_top.append(_dot(z_pool, _unpack_rows(w_out_ref[:POOL_WIDTH // 2, c0:c0 + out_cols])))

        log_a = (-LRU_C * r) * jax.nn.softplus(-lam_ref[:, hs])
        a = jnp.exp(log_a)
        u = -jnp.tanh(log_a) * (a * a + 1.0)
        mult = jnp.where(u == 0.0, 0.0, u * lax.rsqrt(u))
        mult = jnp.concatenate([jnp.where(first, 1.0, mult[0:1]), mult[1:]], axis=0)
        bv = mult * (i * xc)

        hl, ac = [bv[0]], [a[0]]
        for k in range(1, lc):
            hl.append(a[k] * hl[-1] + bv[k])
            ac.append(a[k] * ac[-1])
        c_a, c_b = ac[-1], hl[-1]
        for d in (1, 2, 4):
            a_s = jnp.where(sub < d, 1.0, pltpu.roll(c_a, d, 0))
            b_s = jnp.where(sub < d, 0.0, pltpu.roll(c_b, d, 0))
            c_b = c_a * b_s + c_b
            c_a = c_a * a_s
        h_prev = h_carry[:, hs]
        h_end = c_b + c_a * h_prev
        h_in = jnp.where(sub == 0, h_prev, pltpu.roll(h_end, 1, 0))
        h_carry[:, hs] = jnp.broadcast_to(h_end[SUBLANES - 1:SUBLANES, :], (SUBLANES, LRU_HEAD_DIM))
        hseq = jnp.stack(hl, axis=0) + jnp.stack(ac, axis=0) * h_in
        gate = jax.nn.gelu(u_gate[:, :, (h % 2) * LRU_HEAD_DIM:(h % 2 + 1) * LRU_HEAD_DIM])
        z_lru.append((hseq * gate).reshape(T, LRU_HEAD_DIM).astype(_BF16))

    for q in range(LRU_HEADS // 2, LRU_HEADS):
        y_top.append(_dot(z_pool, _unpack_rows(w_out_ref[:POOL_WIDTH // 2, q * out_cols:(q + 1) * out_cols])))
    y = (jnp.concatenate(y_top, axis=1)
         + _dot(jnp.concatenate(z_lru, axis=1), _unpack_rows(w_out_ref[POOL_WIDTH // 2:, :])))
    resid_ref[slot] = DEEPNORM_ALPHA * x + y


def _const_spec(shape, single_buffer=False):
    nd = len(shape)
    kwargs = {"pipeline_mode": pl.Buffered(1)} if single_buffer else {}
    return pl.BlockSpec(shape, lambda *_: (0,) * nd, **kwargs)


def _mixer(x, w_in, conv_w, conv_b, w_ax, b_a, b_x, lam, w_pool, b_pool, pool_scale, w_out, g, b,
           cast_weights):
    B, S, D = x.shape
    T = MIXER_TILE
    n_seq = S // T
    n_steps = B * n_seq
    lc = T // SUBLANES
    assert T % SUBLANES == 0 and lc >= POOL_HIST and lc % (2 * SUBLANES) == 0 and n_steps >= 3

    chunk_in, chunk_out = [], []
    for w in cast_weights:
        assert w.shape[0] % (2 * SUBLANES * n_steps) == 0
        rows_w = w.shape[0] // n_steps
        chunk_in.append(pl.BlockSpec((rows_w, w.shape[1]), lambda i: (jnp.minimum(i, n_steps - 1), 0)))
        chunk_out.append(pl.BlockSpec((rows_w // 2, w.shape[1]), lambda i: (jnp.minimum(i, n_steps - 1), 0)))
    return pl.pallas_call(
        functools.partial(_mixer_kernel, n_seq=n_seq),
        grid=(n_steps + 1,),
        in_specs=[
            pl.BlockSpec(memory_space=pl.ANY),
            _const_spec(w_in.shape, True),
            _const_spec(conv_w.shape), _const_spec(conv_b.shape),
            _const_spec(w_ax.shape, True),
            _const_spec(b_a.shape), _const_spec(b_x.shape), _const_spec(lam.shape),
            _const_spec(w_pool.shape, True),
            _const_spec(b_pool.shape), _const_spec(pool_scale.shape),
            _const_spec(w_out.shape, True),
            _const_spec(g.shape), _const_spec(b.shape),
        ] + chunk_in,
        out_specs=[pl.BlockSpec(memory_space=pl.ANY)] + chunk_out,
        out_shape=[jax.ShapeDtypeStruct(x.shape, _F32)]
        + [jax.ShapeDtypeStruct((w.shape[0] // 2, w.shape[1]), jnp.uint32) for w in cast_weights],
        scratch_shapes=[
            pltpu.VMEM((2, lc, SUBLANES, D), _F32),
            pltpu.VMEM((2, lc, SUBLANES, D), _F32),
            pltpu.SemaphoreType.DMA((2, SUBLANES)),
            pltpu.SemaphoreType.DMA((2, SUBLANES)),
            pltpu.VMEM((POOL_HIST, SUBLANES, POOL_WIDTH), _F32),
            pltpu.VMEM((CONV_WIDTH - 1, SUBLANES, LRU_WIDTH), _F32),
            pltpu.VMEM((SUBLANES, LRU_WIDTH), _F32),
            pltpu.VMEM((2, T, D), _F32),
        ],
        compiler_params=pltpu.CompilerParams(
            dimension_semantics=("arbitrary",), vmem_limit_bytes=VMEM_LIMIT_BYTES),
        name="mixer",
    )(x, w_in, conv_w, conv_b, w_ax, b_a, b_x, lam, w_pool, b_pool, pool_scale, w_out, g, b, *cast_weights)


def _kv_kernel(mem_ref, wk_ref, wv_ref, *rest):
    n_cast = (len(rest) - 2) // 2
    cast_in, k_ref, v_ref, cast_out = rest[:n_cast], rest[n_cast], rest[n_cast + 1], rest[n_cast + 2:]
    for src, dst in zip(cast_in, cast_out):
        dst[...] = _pack_rows(src[...])
    m = mem_ref[...].astype(_BF16)
    k_ref[...] = _pack_rows(_dot(m, wk_ref[...].astype(_BF16)))
    v_ref[...] = _pack_rows(_dot(m, wv_ref[...].astype(_BF16)))


def _kv_proj(mem2d, w_k, w_v, cast_weights):
    M, D = mem2d.shape
    tn = KV_TILE_N
    n_steps = D // tn
    wspec = pl.BlockSpec((D, tn), lambda j: (0, j))
    ospec = pl.BlockSpec((M // 2, tn), lambda j: (0, j))
    chunk_in, chunk_out = [], []
    for w in cast_weights:
        assert w.shape[0] % (2 * SUBLANES * n_steps) == 0
        rows_w = w.shape[0] // n_steps
        chunk_in.append(pl.BlockSpec((rows_w, w.shape[1]), lambda j: (j, 0)))
        chunk_out.append(pl.BlockSpec((rows_w // 2, w.shape[1]), lambda j: (j, 0)))
    return pl.pallas_call(
        _kv_kernel,
        grid=(n_steps,),
        in_specs=[_const_spec(mem2d.shape, True), wspec, wspec] + chunk_in,
        out_specs=[ospec, ospec] + chunk_out,
        out_shape=[jax.ShapeDtypeStruct((M // 2, D), jnp.uint32)] * 2
        + [jax.ShapeDtypeStruct((w.shape[0] // 2, w.shape[1]), jnp.uint32) for w in cast_weights],
        compiler_params=pltpu.CompilerParams(
            dimension_semantics=("arbitrary",), vmem_limit_bytes=VMEM_LIMIT_BYTES),
        name="kv_proj",
    )(mem2d, w_k, w_v, *cast_weights)


def _xattn_kernel(x_ref, k_ref, v_ref, wq_ref, wo_ref, g_ref, b_ref, o_ref, resid_ref):
    i = pl.program_id(0)
    n_tiles = pl.num_programs(0) - 1
    slot = i % 2

    @pl.when(i == 0)
    def _():
        resid_ref[1] = jnp.zeros(resid_ref.shape[1:], _F32)

    def finish_previous():
        return _normalize_previous(resid_ref, slot, g_ref, b_ref, o_ref)

    @pl.when(i < n_tiles)
    def _():
        zeros = finish_previous()
        x = x_ref[0]
        xb = x.astype(_BF16)
        T = x.shape[0]
        scale = XATTN_HEAD_DIM ** -0.5
        heads = [slice(h * XATTN_HEAD_DIM, (h + 1) * XATTN_HEAD_DIM) for h in range(XATTN_HEADS)]

        blocks_per_head = XATTN_HEAD_DIM // ANCHOR_COLS
        anchors = [[] for _ in range(XATTN_HEADS * blocks_per_head)]
        for c, z in enumerate(zeros):
            anchors[min(c + 1, len(anchors) - 1)].append(z)

        def query(h):
            q = _dot(xb, _unpack_rows(wq_ref[:, heads[h]]))
            cols = []
            for blk in range(blocks_per_head):
                qb = q[:, blk * ANCHOR_COLS:(blk + 1) * ANCHOR_COLS]
                for z in anchors[h * blocks_per_head + blk]:
                    qb = qb + jnp.tile(z, (T // SUBLANES, ANCHOR_COLS // LANES))
                cols.append(qb)
            return jnp.concatenate(cols, axis=1).astype(_BF16)

        def probs(q, h):
            sc = lax.dot_general(q, _unpack_rows(k_ref[0, :, heads[h]]), (((1,), (1,)), ((), ())),
                                 preferred_element_type=_F32) * scale
            e = jnp.exp(sc - jnp.max(sc, axis=-1, keepdims=True))
            return (e * (1.0 / jnp.sum(e, axis=-1, keepdims=True))).astype(_BF16)

        def attend(p, h):
            return _dot(p, _unpack_rows(v_ref[0, :, heads[h]])).astype(_BF16)

        half_rows = wo_ref.shape[0] // 2
        q0 = query(0)
        q1 = query(1)
        p0 = probs(q0, 0)
        q2 = query(2)
        o0 = attend(p0, 0)
        p1 = probs(q1, 1)
        q3 = query(3)
        o1 = attend(p1, 1)
        p2 = probs(q2, 2)
        y = _dot(jnp.concatenate([o0, o1], axis=1), _unpack_rows(wo_ref[:half_rows, :]))
        o2 = attend(p2, 2)
        p3 = probs(q3, 3)
        o3 = attend(p3, 3)
        y = y + _dot(jnp.concatenate([o2, o3], axis=1), _unpack_rows(wo_ref[half_rows:, :]))
        resid_ref[slot] = DEEPNORM_ALPHA * x + y

    @pl.when(i == n_tiles)
    def _():
        finish_previous()


def _xattn(x, k, v, w_q, w_o, g, b):
    B, S, D = x.shape
    T = XATTN_TILE
    n_seq = S // T
    n_tiles = B * n_seq

    def cur(i):
        t = jnp.minimum(i, n_tiles - 1)
        return t // n_seq, t % n_seq

    def prev(i):
        t = jnp.maximum(i - 1, 0)
        return t // n_seq, t % n_seq

    kvspec = pl.BlockSpec((1, N_MEM // 2, D), lambda i: (cur(i)[0], 0, 0))
    return pl.pallas_call(
        _xattn_kernel,
        grid=(n_tiles + 1,),
        in_specs=[pl.BlockSpec((1, T, D), lambda i: (*cur(i), 0)), kvspec, kvspec,
                  _const_spec(w_q.shape, True), _const_spec(w_o.shape, True),
                  _const_spec(g.shape), _const_spec(b.shape)],
        out_specs=pl.BlockSpec((1, T, D), lambda i: (*prev(i), 0)),
        out_shape=jax.ShapeDtypeStruct(x.shape, _F32),
        scratch_shapes=[pltpu.VMEM((2, T, D), _F32)],
        compiler_params=pltpu.CompilerParams(
            dimension_semantics=("arbitrary",), vmem_limit_bytes=VMEM_LIMIT_BYTES),
        name="xattn",
    )(x, k, v, w_q, w_o, g, b)


def _mlp_kernel(x_ref, w1_ref, w2_ref, g_ref, b_ref, o_ref, xb_ref, acc_ref, *, n_f):
    i, j = pl.program_id(0), pl.program_id(1)
    n_tiles = pl.num_programs(0) - 1
    slot = i % 2
    rows = x_ref.shape[0] // n_f

    @pl.when((i == 0) & (j == 0))
    def _():
        acc_ref[1] = jnp.zeros(acc_ref.shape[1:], _F32)

    def finish_previous_rows():
        r0 = pl.multiple_of(j * rows, rows)
        o_ref[pl.ds(r0, rows), :] = _layer_norm(acc_ref[1 - slot, pl.ds(r0, rows), :], g_ref[...], b_ref[...])

    @pl.when(i < n_tiles)
    def _():
        @pl.when(j == 0)
        def _():
            xb_ref[...] = x_ref[...].astype(_BF16)
            acc_ref[slot] = DEEPNORM_ALPHA * x_ref[...]

        finish_previous_rows()
        h = jnp.square(jnp.maximum(_dot(xb_ref[...], _unpack_rows(w1_ref[...])), 0.0))
        acc_ref[slot] += _dot(h.astype(_BF16), _unpack_rows(w2_ref[...]))

    @pl.when(i == n_tiles)
    def _():
        finish_previous_rows()


def _mlp(x2d, w1, w2, g, b):
    M, D = x2d.shape
    F = w1.shape[1]
    tm, tf = MLP_TILE_M, MLP_TILE_F
    n_tiles, n_f = M // tm, F // tf
    assert tm % (n_f * SUBLANES) == 0

    def wblk(i, j):
        return jnp.where(i == n_tiles, n_f - 1, j)

    return pl.pallas_call(
        functools.partial(_mlp_kernel, n_f=n_f),
        grid=(n_tiles + 1, n_f),
        in_specs=[pl.BlockSpec((tm, D), lambda i, j: (jnp.minimum(i, n_tiles - 1), 0)),
                  pl.BlockSpec((D // 2, tf), lambda i, j: (0, wblk(i, j))),
                  pl.BlockSpec((tf // 2, D), lambda i, j: (wblk(i, j), 0)),
                  _const_spec(g.shape), _const_spec(b.shape)],
        out_specs=pl.BlockSpec((tm, D), lambda i, j: (jnp.maximum(i - 1, 0), 0)),
        out_shape=jax.ShapeDtypeStruct((M, D), _F32),
        scratch_shapes=[pltpu.VMEM((tm, D), _BF16), pltpu.VMEM((2, tm, D), _F32)],
        compiler_params=pltpu.CompilerParams(
            dimension_semantics=("arbitrary", "arbitrary"),
            vmem_limit_bytes=VMEM_LIMIT_BYTES),
        name="mlp",
    )(x2d, w1, w2, g, b)


def kernel(x, mem, w_in, conv_w, conv_b, w_a, b_a, w_x, b_x, lru_lambda, w_pool, b_pool, pool_scale,
           w_out, ln1_g, ln1_b, w_q, w_k, w_v, w_o, ln2_g, ln2_b, w_ff1, w_ff2, ln3_g, ln3_b):
    B, S, D = x.shape
    row = lambda p: p.reshape(1, -1)
    for l in range(DEPTH):
        w_ax = jnp.concatenate([w_a[l], w_x[l]], axis=-1).astype(_BF16)
        k, v, w_in_b, w_out_b = _kv_proj(mem.reshape(B * N_MEM, D), w_k[l], w_v[l], (w_in[l], w_out[l]))
        x, wq_b, wo_b, w1_b, w2_b = _mixer(
            x, w_in_b, conv_w[l], row(conv_b[l]), w_ax,
            row(b_a[l]), row(b_x[l]), row(lru_lambda[l]),
            w_pool[l].astype(_BF16), row(b_pool[l]), row(pool_scale[l]),
            w_out_b, row(ln1_g[l]), row(ln1_b[l]),
            (w_q[l], w_o[l], w_ff1[l], w_ff2[l]))
        x = _xattn(x, k.reshape(B, N_MEM // 2, D), v.reshape(B, N_MEM // 2, D),
                   wq_b, wo_b, row(ln2_g[l]), row(ln2_b[l]))
        x = _mlp(x.reshape(B * S, D), w1_b, w2_b, row(ln3_g[l]), row(ln3_b[l])).reshape(B, S, D)
    return x
```

```python
import functools

import jax
import jax.numpy as jnp
from jax import lax
from jax.experimental import pallas as pl
from jax.experimental.pallas import tpu as pltpu

D_MODEL = 2048
POOL_WIDTH = 1024
LRU_WIDTH = 1024
POOL_WINDOWS = (2, 4, 8, 16)
POOL_GROUP = 256
LRU_HEADS = 8
LRU_HEAD_DIM = 128
CONV_WIDTH = 4
LRU_C = 8.0
N_MEM = 256
XATTN_HEADS = 4
XATTN_HEAD_DIM = 512
D_FF = 4 * D_MODEL
LN_EPS = 1e-5
DEPTH = 1
DEEPNORM_ALPHA = (2.0 * DEPTH) ** 0.25

POOL_HIST = 16
SUBLANES = 8
LANES = 128

MIXER_TILE = 256
XATTN_TILE = 512
ANCHOR_COLS = 256
MLP_TILE_M = 1024
MLP_ROW_HALVES = 2
MLP_TILE_F = 1024
KV_TILE_N = 256
VMEM_LIMIT_BYTES = 56 * 1024 * 1024

_F32 = jnp.float32
_BF16 = jnp.bfloat16


def _layer_norm(v, g, b):
    mu = jnp.mean(v, axis=-1, keepdims=True)
    c = v - mu
    var = jnp.mean(jnp.square(c), axis=-1, keepdims=True)
    return c * lax.rsqrt(var + LN_EPS) * g + b


def _dot(a, b):
    return jnp.dot(a, b, preferred_element_type=_F32)


def _pack_rows(v):
    return pltpu.bitcast(v.astype(_BF16), jnp.uint32)


def _unpack_rows(w):
    return pltpu.bitcast(w, _BF16)


def _ordering_zero(v):
    rows, cols = v.shape
    t = jnp.sum(v.reshape(rows // SUBLANES, SUBLANES, cols), axis=0)
    t = functools.reduce(lambda a, b: a + b, [t[:, c:c + LANES] for c in range(0, cols, LANES)])
    return jnp.minimum(jnp.abs(t), 0.0)


def _normalize_previous(resid_ref, slot, g_ref, b_ref, o_ref):
    T, D = resid_ref.shape[1:]
    chunk = T // (D // ANCHOR_COLS)
    zeros = []
    for r0 in range(0, T, chunk):
        out = _layer_norm(resid_ref[1 - slot, r0:r0 + chunk, :], g_ref[...], b_ref[...])
        o_ref[0, r0:r0 + chunk, :] = out
        zeros.append(_ordering_zero(out))
    return zeros


def _tile_zeros(zeros, rows, cols):
    width = cols // len(zeros)
    return jnp.concatenate([jnp.tile(z, (rows // SUBLANES, width // LANES)) for z in zeros], axis=1)


def _history_rows(cur_tail, prev_tail):
    sub = lax.broadcasted_iota(jnp.int32, (1,) + cur_tail.shape[1:], 1)
    return jnp.where(sub == 0, pltpu.roll(prev_tail, 1, 1), pltpu.roll(cur_tail, 1, 1))


def _tile_copies(hbm_ref, buf, sems, t, sl, n_seq, to_hbm):
    lc = buf.shape[1]
    b, s = t // n_seq, t % n_seq
    copies = []
    for j in range(SUBLANES):
        rows = hbm_ref.at[b, pl.ds(pl.multiple_of((s * SUBLANES + j) * lc, lc), lc), :]
        chunk = buf.at[sl, :, j, :]
        src, dst = (chunk, rows) if to_hbm else (rows, chunk)
        copies.append(pltpu.make_async_copy(src, dst, sems.at[sl, j]))
    return copies


def _mixer_kernel(x_hbm, w_in_ref, conv_w_ref, conv_b_ref, w_ax_ref, b_a_ref, b_x_ref, lam_ref,
                  w_pool_ref, b_pool_ref, pool_scale_ref, w_out_ref, g_ref, b_ref, *rest, n_seq):
    n_cast = (len(rest) - 9) // 2
    cast_in, o_hbm, cast_out = rest[:n_cast], rest[n_cast], rest[n_cast + 1:2 * n_cast + 1]
    xbuf, obuf, in_sem, out_sem, pool_hist, conv_hist, h_carry, resid_ref = rest[2 * n_cast + 1:]
    i = pl.program_id(0)
    n_tiles = pl.num_programs(0) - 1
    s = jnp.minimum(i, n_tiles - 1) % n_seq
    slot = i % 2

    def x_copies(t, sl):
        return _tile_copies(x_hbm, xbuf, in_sem, t, sl, n_seq, to_hbm=False)

    def out_copies(t, sl):
        return _tile_copies(o_hbm, obuf, out_sem, t, sl, n_seq, to_hbm=True)

    @pl.when(i == 0)
    def _():
        for c in x_copies(0, 0):
            c.start()
        resid_ref[1] = jnp.zeros(resid_ref.shape[1:], _F32)

    @pl.when(i < n_tiles)
    def _():
        for c in x_copies(i, slot):
            c.wait()

    @pl.when(i + 1 < n_tiles)
    def _():
        for c in x_copies(i + 1, 1 - slot):
            c.start()

    @pl.when(i >= 3)
    def _():
        for c in out_copies(i - 3, 1 - slot):
            c.wait()

    def finish_previous():
        T, D = resid_ref.shape[1:]
        chunk = T // (D // ANCHOR_COLS)
        zeros = []
        for r0 in range(0, T, chunk):
            out = _layer_norm(resid_ref[1 - slot, r0:r0 + chunk, :], g_ref[...], b_ref[...])
            obuf[1 - slot, r0 // SUBLANES:(r0 + chunk) // SUBLANES] = out.reshape(chunk // SUBLANES, SUBLANES, D)
            zeros.append(_ordering_zero(out))
        return zeros

    @pl.when(i == n_tiles)
    def _():
        finish_previous()

    @pl.when(i < n_tiles)
    def _():
        _mixer_tile(s, slot, finish_previous, xbuf, w_in_ref, conv_w_ref, conv_b_ref, w_ax_ref, b_a_ref, b_x_ref,
                    lam_ref, w_pool_ref, b_pool_ref, pool_scale_ref, w_out_ref, cast_in, cast_out,
                    pool_hist, conv_hist, h_carry, resid_ref)

    @pl.when(i >= 1)
    def _():
        for c in out_copies(i - 1, 1 - slot):
            c.start()

    @pl.when(i == n_tiles)
    def _():
        for c in out_copies(n_tiles - 2, slot) + out_copies(n_tiles - 1, 1 - slot):
            c.wait()


def _mixer_tile(s, slot, finish_previous, xbuf, w_in_ref, conv_w_ref, conv_b_ref, w_ax_ref, b_a_ref, b_x_ref,
                lam_ref, w_pool_ref, b_pool_ref, pool_scale_ref, w_out_ref, cast_in, cast_out,
                pool_hist, conv_hist, h_carry, resid_ref):
    lc, _, D = xbuf.shape[1:]
    T = lc * SUBLANES

    @pl.when(s == 0)
    def _():
        pool_hist[...] = jnp.zeros_like(pool_hist)
        conv_hist[...] = jnp.zeros_like(conv_hist)
        h_carry[...] = jnp.zeros_like(h_carry)

    zeros = finish_previous()
    for src, dst in zip(cast_in, cast_out):
        dst[...] = _pack_rows(src[...])

    x = xbuf[slot].reshape(T, D)
    xb = x.astype(_BF16)
    n_lru = LRU_WIDTH // LANES
    half = len(zeros) // 2
    up = _dot(xb, _unpack_rows(w_in_ref[:, :POOL_WIDTH])) + _tile_zeros(zeros[:half], T, POOL_WIDTH)
    up = up.reshape(lc, SUBLANES, POOL_WIDTH)
    u_lru = (_dot(xb, _unpack_rows(w_in_ref[:, POOL_WIDTH:POOL_WIDTH + LRU_WIDTH]))
             + _tile_zeros(zeros[half:], T, LRU_WIDTH)).reshape(lc, SUBLANES, LRU_WIDTH)

    kk = lax.broadcasted_iota(jnp.int32, (POOL_HIST, SUBLANES, LANES), 0)
    jj = lax.broadcasted_iota(jnp.int32, (POOL_HIST, SUBLANES, LANES), 1)
    t_head = s * T + jj * lc + kk

    ext = jnp.concatenate([_history_rows(up[lc - POOL_HIST:], pool_hist[...]), up], axis=0)
    pool_hist[...] = up[lc - POOL_HIST:]
    s2 = ext[1:] + ext[:-1]
    s2r = s2[:, :, POOL_GROUP:]
    s4 = s2r[2:] + s2r[:-2]
    s4r = s4[:, :, POOL_GROUP:]
    s8 = s4r[4:] + s4r[:-4]
    s8r = s8[:, :, POOL_GROUP:]
    s16 = s8r[8:] + s8r[:-8]
    sums = (s2[15:, :, :POOL_GROUP], s4[13:, :, :POOL_GROUP], s8[9:, :, :POOL_GROUP], s16[1:])
    y_pool = []
    for g, w in enumerate(POOL_WINDOWS):
        cs = slice(g * POOL_GROUP, (g + 1) * POOL_GROUP)
        inv_head = 1.0 / jnp.minimum(t_head + 1, w).astype(_F32)
        inv_head = jnp.concatenate([inv_head] * (POOL_GROUP // LANES), axis=-1)
        mean = jnp.concatenate([sums[g][:POOL_HIST] * inv_head, sums[g][POOL_HIST:] * (1.0 / w)], axis=0)
        mixed = mean - up[:, :, cs]
        yg = _dot(mixed.reshape(T, POOL_GROUP).astype(_BF16), w_pool_ref[g])
        y_pool.append(((yg + b_pool_ref[:, cs]) * pool_scale_ref[:, cs]).astype(_BF16))
    z_pool = jnp.concatenate(y_pool, axis=1)

    n_hist = CONV_WIDTH - 1
    sub = lax.broadcasted_iota(jnp.int32, (SUBLANES, LRU_HEAD_DIM), 0)
    first = t_head[0:1] == 0
    gate_cols = 2 * LRU_HEAD_DIM
    out_cols = D_MODEL // LRU_HEADS
    y_top, z_lru = [], []
    for h in range(LRU_HEADS):
        hs = slice(h * LRU_HEAD_DIM, (h + 1) * LRU_HEAD_DIM)
        ul = u_lru[:, :, hs]
        ext2 = jnp.concatenate([_history_rows(ul[lc - n_hist:], conv_hist[:, :, hs]), ul], axis=0)
        conv_hist[:, :, hs] = ul[lc - n_hist:]
        xc = ext2[0:lc] * conv_w_ref[0:1, hs]
        for k in range(1, CONV_WIDTH):
            xc = xc + ext2[k:k + lc] * conv_w_ref[k:k + 1, hs]
        xc = xc + conv_b_ref[:, hs]

        pre = _dot(xc.reshape(T, LRU_HEAD_DIM).astype(_BF16), w_ax_ref[h])
        pre = pre.reshape(lc, SUBLANES, 2 * LRU_HEAD_DIM)
        r = jax.nn.sigmoid(pre[:, :, :LRU_HEAD_DIM] + b_a_ref[:, hs])
        i = jax.nn.sigmoid(pre[:, :, LRU_HEAD_DIM:] + b_x_ref[:, hs])

        if h % 2 == 0:
            c0 = POOL_WIDTH + LRU_WIDTH + h * LRU_HEAD_DIM
            u_gate = _dot(xb, _unpack_rows(w_in_ref[:, c0:c0 + gate_cols])).reshape(lc, SUBLANES, gate_cols)
        else:
            c0 = (h // 2) * out_cols
            y_top.append(_dot(z_pool, _unpack_rows(w_out_ref[:POOL_WIDTH // 2, c0:c0 + out_cols])))

        log_a = (-LRU_C * r) * jax.nn.softplus(-lam_ref[:, hs])
        a = jnp.exp(log_a)
        u = -jnp.tanh(log_a) * (a * a + 1.0)
        mult = jnp.where(u == 0.0, 0.0, u * lax.rsqrt(u))
        mult = jnp.concatenate([jnp.where(first, 1.0, mult[0:1]), mult[1:]], axis=0)
        bv = mult * (i * xc)

        hl, ac = [bv[0]], [a[0]]
        for k in range(1, lc):
            hl.append(a[k] * hl[-1] + bv[k])
            ac.append(a[k] * ac[-1])
        c_a, c_b = ac[-1], hl[-1]
        for d in (1, 2, 4):
            a_s = jnp.where(sub < d, 1.0, pltpu.roll(c_a, d, 0))
            b_s = jnp.where(sub < d, 0.0, pltpu.roll(c_b, d, 0))
            c_b = c_a * b_s + c_b
            c_a = c_a * a_s
        h_prev = h_carry[:, hs]
        h_end = c_b + c_a * h_prev
        h_in = jnp.where(sub == 0, h_prev, pltpu.roll(h_end, 1, 0))
        h_carry[:, hs] = jnp.broadcast_to(h_end[SUBLANES - 1:SUBLANES, :], (SUBLANES, LRU_HEAD_DIM))
        hseq = jnp.stack(hl, axis=0) + jnp.stack(ac, axis=0) * h_in
        gate = jax.nn.gelu(u_gate[:, :, (h % 2) * LRU_HEAD_DIM:(h % 2 + 1) * LRU_HEAD_DIM])
        z_lru.append((hseq * gate).reshape(T, LRU_HEAD_DIM).astype(_BF16))

    for q in range(LRU_HEADS // 2, LRU_HEADS):
        y_top.append(_dot(z_pool, _unpack_rows(w_out_ref[:POOL_WIDTH // 2, q * out_cols:(q + 1) * out_cols])))
    y = (jnp.concatenate(y_top, axis=1)
         + _dot(jnp.concatenate(z_lru, axis=1), _unpack_rows(w_out_ref[POOL_WIDTH // 2:, :])))
    resid_ref[slot] = DEEPNORM_ALPHA * x + y


def _const_spec(shape, single_buffer=False):
    nd = len(shape)
    kwargs = {"pipeline_mode": pl.Buffered(1)} if single_buffer else {}
    return pl.BlockSpec(shape, lambda *_: (0,) * nd, **kwargs)


def _mixer(x, w_in, conv_w, conv_b, w_ax, b_a, b_x, lam, w_pool, b_pool, pool_scale, w_out, g, b,
           cast_weights):
    B, S, D = x.shape
    T = MIXER_TILE
    n_seq = S // T
    n_steps = B * n_seq
    lc = T // SUBLANES
    assert T % SUBLANES == 0 and lc >= POOL_HIST and lc % (2 * SUBLANES) == 0 and n_steps >= 3

    chunk_in, chunk_out = [], []
    for w in cast_weights:
        assert w.shape[0] % (2 * SUBLANES * n_steps) == 0
        rows_w = w.shape[0] // n_steps
        chunk_in.append(pl.BlockSpec((rows_w, w.shape[1]), lambda i: (jnp.minimum(i, n_steps - 1), 0)))
        chunk_out.append(pl.BlockSpec((rows_w // 2, w.shape[1]), lambda i: (jnp.minimum(i, n_steps - 1), 0)))
    return pl.pallas_call(
        functools.partial(_mixer_kernel, n_seq=n_seq),
        grid=(n_steps + 1,),
        in_specs=[
            pl.BlockSpec(memory_space=pl.ANY),
            _const_spec(w_in.shape, True),
            _const_spec(conv_w.shape), _const_spec(conv_b.shape),
            _const_spec(w_ax.shape, True),
            _const_spec(b_a.shape), _const_spec(b_x.shape), _const_spec(lam.shape),
            _const_spec(w_pool.shape, True),
            _const_spec(b_pool.shape), _const_spec(pool_scale.shape),
            _const_spec(w_out.shape, True),
            _const_spec(g.shape), _const_spec(b.shape),
        ] + chunk_in,
        out_specs=[pl.BlockSpec(memory_space=pl.ANY)] + chunk_out,
        out_shape=[jax.ShapeDtypeStruct(x.shape, _F32)]
        + [jax.ShapeDtypeStruct((w.shape[0] // 2, w.shape[1]), jnp.uint32) for w in cast_weights],
        scratch_shapes=[
            pltpu.VMEM((2, lc, SUBLANES, D), _F32),
            pltpu.VMEM((2, lc, SUBLANES, D), _F32),
            pltpu.SemaphoreType.DMA((2, SUBLANES)),
            pltpu.SemaphoreType.DMA((2, SUBLANES)),
            pltpu.VMEM((POOL_HIST, SUBLANES, POOL_WIDTH), _F32),
            pltpu.VMEM((CONV_WIDTH - 1, SUBLANES, LRU_WIDTH), _F32),
            pltpu.VMEM((SUBLANES, LRU_WIDTH), _F32),
            pltpu.VMEM((2, T, D), _F32),
        ],
        compiler_params=pltpu.CompilerParams(
            dimension_semantics=("arbitrary",), vmem_limit_bytes=VMEM_LIMIT_BYTES),
        name="mixer",
    )(x, w_in, conv_w, conv_b, w_ax, b_a, b_x, lam, w_pool, b_pool, pool_scale, w_out, g, b, *cast_weights)


def _kv_kernel(mem_ref, wk_ref, wv_ref, *rest):
    n_cast = (len(rest) - 2) // 2
    cast_in, k_ref, v_ref, cast_out = rest[:n_cast], rest[n_cast], rest[n_cast + 1], rest[n_cast + 2:]
    for src, dst in zip(cast_in, cast_out):
        dst[...] = _pack_rows(src[...])
    m = mem_ref[...].astype(_BF16)
    k_ref[...] = _pack_rows(_dot(m, wk_ref[...].astype(_BF16)))
    v_ref[...] = _pack_rows(_dot(m, wv_ref[...].astype(_BF16)))


def _kv_proj(mem2d, w_k, w_v, cast_weights):
    M, D = mem2d.shape
    tn = KV_TILE_N
    n_steps = D // tn
    wspec = pl.BlockSpec((D, tn), lambda j: (0, j))
    ospec = pl.BlockSpec((M // 2, tn), lambda j: (0, j))
    chunk_in, chunk_out = [], []
    for w in cast_weights:
        assert w.shape[0] % (2 * SUBLANES * n_steps) == 0
        rows_w = w.shape[0] // n_steps
        chunk_in.append(pl.BlockSpec((rows_w, w.shape[1]), lambda j: (j, 0)))
        chunk_out.append(pl.BlockSpec((rows_w // 2, w.shape[1]), lambda j: (j, 0)))
    return pl.pallas_call(
        _kv_kernel,
        grid=(n_steps,),
        in_specs=[_const_spec(mem2d.shape, True), wspec, wspec] + chunk_in,
        out_specs=[ospec, ospec] + chunk_out,
        out_shape=[jax.ShapeDtypeStruct((M // 2, D), jnp.uint32)] * 2
        + [jax.ShapeDtypeStruct((w.shape[0] // 2, w.shape[1]), jnp.uint32) for w in cast_weights],
        compiler_params=pltpu.CompilerParams(
            dimension_semantics=("arbitrary",), vmem_limit_bytes=VMEM_LIMIT_BYTES),
        name="kv_proj",
    )(mem2d, w_k, w_v, *cast_weights)


def _xattn_kernel(x_ref, k_ref, v_ref, wq_ref, wo_ref, g_ref, b_ref, o_ref, resid_ref):
    i = pl.program_id(0)
    n_tiles = pl.num_programs(0) - 1
    slot = i % 2

    @pl.when(i == 0)
    def _():
        resid_ref[1] = jnp.zeros(resid_ref.shape[1:], _F32)

    def finish_previous():
        return _normalize_previous(resid_ref, slot, g_ref, b_ref, o_ref)

    @pl.when(i < n_tiles)
    def _():
        zeros = finish_previous()
        x = x_ref[0]
        xb = x.astype(_BF16)
        T = x.shape[0]
        scale = XATTN_HEAD_DIM ** -0.5
        heads = [slice(h * XATTN_HEAD_DIM, (h + 1) * XATTN_HEAD_DIM) for h in range(XATTN_HEADS)]

        blocks_per_head = XATTN_HEAD_DIM // ANCHOR_COLS
        anchors = [[] for _ in range(XATTN_HEADS * blocks_per_head)]
        for c, z in enumerate(zeros):
            anchors[min(c + 1, len(anchors) - 1)].append(z)

        def query(h):
            q = _dot(xb, _unpack_rows(wq_ref[:, heads[h]]))
            cols = []
            for blk in range(blocks_per_head):
                qb = q[:, blk * ANCHOR_COLS:(blk + 1) * ANCHOR_COLS]
                for z in anchors[h * blocks_per_head + blk]:
                    qb = qb + jnp.tile(z, (T // SUBLANES, ANCHOR_COLS // LANES))
                cols.append(qb)
            return jnp.concatenate(cols, axis=1).astype(_BF16)

        def probs(q, h):
            sc = lax.dot_general(q, _unpack_rows(k_ref[0, :, heads[h]]), (((1,), (1,)), ((), ())),
                                 preferred_element_type=_F32) * scale
            e = jnp.exp(sc - jnp.max(sc, axis=-1, keepdims=True))
            return (e * (1.0 / jnp.sum(e, axis=-1, keepdims=True))).astype(_BF16)

        def attend(p, h):
            return _dot(p, _unpack_rows(v_ref[0, :, heads[h]])).astype(_BF16)

        half_rows = wo_ref.shape[0] // 2
        q0 = query(0)
        q1 = query(1)
        p0 = probs(q0, 0)
        q2 = query(2)
        o0 = attend(p0, 0)
        p1 = probs(q1, 1)
        q3 = query(3)
        o1 = attend(p1, 1)
        p2 = probs(q2, 2)
        y = _dot(jnp.concatenate([o0, o1], axis=1), _unpack_rows(wo_ref[:half_rows, :]))
        o2 = attend(p2, 2)
        p3 = probs(q3, 3)
        o3 = attend(p3, 3)
        y = y + _dot(jnp.concatenate([o2, o3], axis=1), _unpack_rows(wo_ref[half_rows:, :]))
        resid_ref[slot] = DEEPNORM_ALPHA * x + y

    @pl.when(i == n_tiles)
    def _():
        finish_previous()


def _xattn(x, k, v, w_q, w_o, g, b):
    B, S, D = x.shape
    T = XATTN_TILE
    n_seq = S // T
    n_tiles = B * n_seq

    def cur(i):
        t = jnp.minimum(i, n_tiles - 1)
        return t // n_seq, t % n_seq

    def prev(i):
        t = jnp.maximum(i - 1, 0)
        return t // n_seq, t % n_seq

    kvspec = pl.BlockSpec((1, N_MEM // 2, D), lambda i: (cur(i)[0], 0, 0))
    return pl.pallas_call(
        _xattn_kernel,
        grid=(n_tiles + 1,),
        in_specs=[pl.BlockSpec((1, T, D), lambda i: (*cur(i), 0)), kvspec, kvspec,
                  _const_spec(w_q.shape, True), _const_spec(w_o.shape, True),
                  _const_spec(g.shape), _const_spec(b.shape)],
        out_specs=pl.BlockSpec((1, T, D), lambda i: (*prev(i), 0)),
        out_shape=jax.ShapeDtypeStruct(x.shape, _F32),
        scratch_shapes=[pltpu.VMEM((2, T, D), _F32)],
        compiler_params=pltpu.CompilerParams(
            dimension_semantics=("arbitrary",), vmem_limit_bytes=VMEM_LIMIT_BYTES),
        name="xattn",
    )(x, k, v, w_q, w_o, g, b)


def _mlp_kernel(x_hbm, w1_ref, w2_ref, g_ref, b_ref, o_hbm, xstage, xb_ref, acc_ref, oslice, x_sem, o_sem,
                *, n_f, halves):
    i, j = pl.program_id(0), pl.program_id(1)
    n_tiles = pl.num_programs(0) - 1
    slot = i % 2
    tm = xstage.shape[0]
    rows = tm // n_f
    q = i * n_f + j
    k = q % 2

    def x_copy(t):
        return pltpu.make_async_copy(x_hbm.at[pl.ds(pl.multiple_of(t * tm, tm), tm), :], xstage, x_sem.at[0])

    def out_copy(step, buf):
        r0 = pl.multiple_of((step // n_f - 1) * tm + (step % n_f) * rows, rows)
        return pltpu.make_async_copy(oslice.at[buf], o_hbm.at[pl.ds(r0, rows), :], o_sem.at[buf])

    @pl.when(q == 0)
    def _():
        x_copy(0).start()
        acc_ref[1] = jnp.zeros(acc_ref.shape[1:], _F32)

    @pl.when((i < n_tiles) & (j == 0))
    def _():
        x_copy(i).wait()
        xb_ref[...] = xstage[...].astype(_BF16)
        acc_ref[slot] = DEEPNORM_ALPHA * xstage[...]

    @pl.when((i + 1 < n_tiles) & (j == 0))
    def _():
        x_copy(i + 1).start()

    @pl.when(q >= n_f + 2)
    def _():
        out_copy(q - 2, k).wait()

    def finish_previous_rows():
        r0 = pl.multiple_of(j * rows, rows)
        oslice[k] = _layer_norm(acc_ref[1 - slot, pl.ds(r0, rows), :], g_ref[...], b_ref[...])

    @pl.when(i < n_tiles)
    def _():
        finish_previous_rows()
        w1 = _unpack_rows(w1_ref[...])
        w2 = _unpack_rows(w2_ref[...])
        half = tm // halves
        hs = [jnp.square(jnp.maximum(_dot(xb_ref[r * half:(r + 1) * half, :], w1), 0.0)).astype(_BF16)
              for r in range(halves)]
        for r in range(halves):
            acc_ref[slot, r * half:(r + 1) * half, :] += _dot(hs[r], w2)

    @pl.when(i == n_tiles)
    def _():
        finish_previous_rows()

    @pl.when(q >= n_f)
    def _():
        out_copy(q, k).start()

    @pl.when(q == (n_tiles + 1) * n_f - 1)
    def _():
        out_copy(q - 1, 1 - k).wait()
        out_copy(q, k).wait()


def _mlp(x2d, w1, w2, g, b):
    M, D = x2d.shape
    F = w1.shape[1]
    tm, tf = MLP_TILE_M, MLP_TILE_F
    n_tiles, n_f = M // tm, F // tf
    assert tm % (n_f * SUBLANES) == 0 and tm % MLP_ROW_HALVES == 0 and n_f >= 2

    def wblk(i, j):
        return jnp.where(i == n_tiles, n_f - 1, j)

    return pl.pallas_call(
        functools.partial(_mlp_kernel, n_f=n_f, halves=MLP_ROW_HALVES),
        grid=(n_tiles + 1, n_f),
        in_specs=[pl.BlockSpec(memory_space=pl.ANY),
                  pl.BlockSpec((D // 2, tf), lambda i, j: (0, wblk(i, j))),
                  pl.BlockSpec((tf // 2, D), lambda i, j: (wblk(i, j), 0)),
                  _const_spec(g.shape), _const_spec(b.shape)],
        out_specs=pl.BlockSpec(memory_space=pl.ANY),
        out_shape=jax.ShapeDtypeStruct((M, D), _F32),
        scratch_shapes=[pltpu.VMEM((tm, D), _F32),
                        pltpu.VMEM((tm, D), _BF16),
                        pltpu.VMEM((2, tm, D), _F32),
                        pltpu.VMEM((2, tm // n_f, D), _F32),
                        pltpu.SemaphoreType.DMA((1,)),
                        pltpu.SemaphoreType.DMA((2,))],
        compiler_params=pltpu.CompilerParams(
            dimension_semantics=("arbitrary", "arbitrary"),
            vmem_limit_bytes=VMEM_LIMIT_BYTES),
        name="mlp",
    )(x2d, w1, w2, g, b)


def kernel(x, mem, w_in, conv_w, conv_b, w_a, b_a, w_x, b_x, lru_lambda, w_pool, b_pool, pool_scale,
           w_out, ln1_g, ln1_b, w_q, w_k, w_v, w_o, ln2_g, ln2_b, w_ff1, w_ff2, ln3_g, ln3_b):
    B, S, D = x.shape
    row = lambda p: p.reshape(1, -1)
    for l in range(DEPTH):
        w_ax = jnp.concatenate([w_a[l], w_x[l]], axis=-1).astype(_BF16)
        k, v, w_in_b, w_out_b = _kv_proj(mem.reshape(B * N_MEM, D), w_k[l], w_v[l], (w_in[l], w_out[l]))
        x, wq_b, wo_b, w1_b, w2_b = _mixer(
            x, w_in_b, conv_w[l], row(conv_b[l]), w_ax,
            row(b_a[l]), row(b_x[l]), row(lru_lambda[l]),
            w_pool[l].astype(_BF16), row(b_pool[l]), row(pool_scale[l]),
            w_out_b, row(ln1_g[l]), row(ln1_b[l]),
            (w_q[l], w_o[l], w_ff1[l], w_ff2[l]))
        x = _xattn(x, k.reshape(B, N_MEM // 2, D), v.reshape(B, N_MEM // 2, D),
                   wq_b, wo_b, row(ln2_g[l]), row(ln2_b[l]))
        x = _mlp(x.reshape(B * S, D), w1_b, w2_b, row(ln3_g[l]), row(ln3_b[l])).reshape(B, S, D)
    return x
```

```python
import functools

import jax
import jax.numpy as jnp
from jax import lax
from jax.experimental import pallas as pl
from jax.experimental.pallas import tpu as pltpu

D_MODEL = 2048
POOL_WIDTH = 1024
LRU_WIDTH = 1024
POOL_WINDOWS = (2, 4, 8, 16)
POOL_GROUP = 256
LRU_HEADS = 8
LRU_HEAD_DIM = 128
CONV_WIDTH = 4
LRU_C = 8.0
N_MEM = 256
XATTN_HEADS = 4
XATTN_HEAD_DIM = 512
D_FF = 4 * D_MODEL
LN_EPS = 1e-5
DEPTH = 1
DEEPNORM_ALPHA = (2.0 * DEPTH) ** 0.25

POOL_HIST = 16
SUBLANES = 8
LANES = 128

MIXER_TILE = 256
XATTN_TILE = 512
ANCHOR_COLS = 256
MLP_TILE_M = 1024
MLP_ROW_HALVES = 2
MLP_TILE_F = 1024
KV_TILE_N = 256
VMEM_LIMIT_BYTES = 56 * 1024 * 1024

_F32 = jnp.float32
_BF16 = jnp.bfloat16


def _layer_norm(v, g, b):
    mu = jnp.mean(v, axis=-1, keepdims=True)
    c = v - mu
    var = jnp.mean(jnp.square(c), axis=-1, keepdims=True)
    return c * lax.rsqrt(var + LN_EPS) * g + b


def _dot(a, b):
    return jnp.dot(a, b, preferred_element_type=_F32)


def _pack_rows(v):
    return pltpu.bitcast(v.astype(_BF16), jnp.uint32)


def _unpack_rows(w):
    return pltpu.bitcast(w, _BF16)


def _ordering_zero(v):
    rows, cols = v.shape
    t = jnp.sum(v.reshape(rows // SUBLANES, SUBLANES, cols), axis=0)
    t = functools.reduce(lambda a, b: a + b, [t[:, c:c + LANES] for c in range(0, cols, LANES)])
    return jnp.minimum(jnp.abs(t), 0.0)


def _normalize_previous(resid_ref, slot, g_ref, b_ref, o_ref):
    T, D = resid_ref.shape[1:]
    chunk = T // (D // ANCHOR_COLS)
    zeros = []
    for r0 in range(0, T, chunk):
        out = _layer_norm(resid_ref[1 - slot, r0:r0 + chunk, :], g_ref[...], b_ref[...])
        o_ref[0, r0:r0 + chunk, :] = out
        zeros.append(_ordering_zero(out))
    return zeros


def _tile_zeros(zeros, rows, cols):
    width = cols // len(zeros)
    return jnp.concatenate([jnp.tile(z, (rows // SUBLANES, width // LANES)) for z in zeros], axis=1)


def _history_rows(cur_tail, prev_tail):
    sub = lax.broadcasted_iota(jnp.int32, (1,) + cur_tail.shape[1:], 1)
    return jnp.where(sub == 0, pltpu.roll(prev_tail, 1, 1), pltpu.roll(cur_tail, 1, 1))


def _tile_copies(hbm_ref, buf, sems, t, sl, n_seq, to_hbm):
    lc = buf.shape[1]
    b, s = t // n_seq, t % n_seq
    copies = []
    for j in range(SUBLANES):
        rows = hbm_ref.at[b, pl.ds(pl.multiple_of((s * SUBLANES + j) * lc, lc), lc), :]
        chunk = buf.at[sl, :, j, :]
        src, dst = (chunk, rows) if to_hbm else (rows, chunk)
        copies.append(pltpu.make_async_copy(src, dst, sems.at[sl, j]))
    return copies


def _mixer_kernel(x_hbm, w_in_ref, conv_w_ref, conv_b_ref, w_ax_ref, b_a_ref, b_x_ref, lam_ref,
                  w_pool_ref, b_pool_ref, pool_scale_ref, w_out_ref, g_ref, b_ref, *rest, n_seq):
    n_cast = (len(rest) - 9) // 2
    cast_in, o_hbm, cast_out = rest[:n_cast], rest[n_cast], rest[n_cast + 1:2 * n_cast + 1]
    xbuf, obuf, in_sem, out_sem, pool_hist, conv_hist, h_carry, resid_ref = rest[2 * n_cast + 1:]
    i = pl.program_id(0)
    n_tiles = pl.num_programs(0) - 1
    s = jnp.minimum(i, n_tiles - 1) % n_seq
    slot = i % 2

    def x_copies(t, sl):
        return _tile_copies(x_hbm, xbuf, in_sem, t, sl, n_seq, to_hbm=False)

    def out_copies(t, sl):
        return _tile_copies(o_hbm, obuf, out_sem, t, sl, n_seq, to_hbm=True)

    @pl.when(i == 0)
    def _():
        for c in x_copies(0, 0):
            c.start()
        resid_ref[1] = jnp.zeros(resid_ref.shape[1:], _F32)

    @pl.when(i < n_tiles)
    def _():
        for c in x_copies(i, slot):
            c.wait()

    @pl.when(i + 1 < n_tiles)
    def _():
        for c in x_copies(i + 1, 1 - slot):
            c.start()

    @pl.when(i >= 3)
    def _():
        for c in out_copies(i - 3, 1 - slot):
            c.wait()

    def finish_previous():
        T, D = resid_ref.shape[1:]
        chunk = T // (D // ANCHOR_COLS)
        zeros = []
        for r0 in range(0, T, chunk):
            out = _layer_norm(resid_ref[1 - slot, r0:r0 + chunk, :], g_ref[...], b_ref[...])
            obuf[1 - slot, r0 // SUBLANES:(r0 + chunk) // SUBLANES] = out.reshape(chunk // SUBLANES, SUBLANES, D)
            zeros.append(_ordering_zero(out))
        return zeros

    @pl.when(i == n_tiles)
    def _():
        finish_previous()

    @pl.when(i < n_tiles)
    def _():
        _mixer_tile(s, slot, finish_previous, xbuf, w_in_ref, conv_w_ref, conv_b_ref, w_ax_ref, b_a_ref, b_x_ref,
                    lam_ref, w_pool_ref, b_pool_ref, pool_scale_ref, w_out_ref, cast_in, cast_out,
                    pool_hist, conv_hist, h_carry, resid_ref)

    @pl.when(i >= 1)
    def _():
        for c in out_copies(i - 1, 1 - slot):
            c.start()

    @pl.when(i == n_tiles)
    def _():
        for c in out_copies(n_tiles - 2, slot) + out_copies(n_tiles - 1, 1 - slot):
            c.wait()


def _mixer_tile(s, slot, finish_previous, xbuf, w_in_ref, conv_w_ref, conv_b_ref, w_ax_ref, b_a_ref, b_x_ref,
                lam_ref, w_pool_ref, b_pool_ref, pool_scale_ref, w_out_ref, cast_in, cast_out,
                pool_hist, conv_hist, h_carry, resid_ref):
    lc, _, D = xbuf.shape[1:]
    T = lc * SUBLANES

    @pl.when(s == 0)
    def _():
        pool_hist[...] = jnp.zeros_like(pool_hist)
        conv_hist[...] = jnp.zeros_like(conv_hist)
        h_carry[...] = jnp.zeros_like(h_carry)

    zeros = finish_previous()
    for src, dst in zip(cast_in, cast_out):
        dst[...] = _pack_rows(src[...])

    x = xbuf[slot].reshape(T, D)
    xb = x.astype(_BF16)
    n_lru = LRU_WIDTH // LANES
    half = len(zeros) // 2
    up = _dot(xb, _unpack_rows(w_in_ref[:, :POOL_WIDTH])) + _tile_zeros(zeros[:half], T, POOL_WIDTH)
    up = up.reshape(lc, SUBLANES, POOL_WIDTH)
    u_lru = (_dot(xb, _unpack_rows(w_in_ref[:, POOL_WIDTH:POOL_WIDTH + LRU_WIDTH]))
             + _tile_zeros(zeros[half:], T, LRU_WIDTH)).reshape(lc, SUBLANES, LRU_WIDTH)

    kk = lax.broadcasted_iota(jnp.int32, (POOL_HIST, SUBLANES, LANES), 0)
    jj = lax.broadcasted_iota(jnp.int32, (POOL_HIST, SUBLANES, LANES), 1)
    t_head = s * T + jj * lc + kk

    ext = jnp.concatenate([_history_rows(up[lc - POOL_HIST:], pool_hist[...]), up], axis=0)
    pool_hist[...] = up[lc - POOL_HIST:]
    s2 = ext[1:] + ext[:-1]
    s2r = s2[:, :, POOL_GROUP:]
    s4 = s2r[2:] + s2r[:-2]
    s4r = s4[:, :, POOL_GROUP:]
    s8 = s4r[4:] + s4r[:-4]
    s8r = s8[:, :, POOL_GROUP:]
    s16 = s8r[8:] + s8r[:-8]
    sums = (s2[15:, :, :POOL_GROUP], s4[13:, :, :POOL_GROUP], s8[9:, :, :POOL_GROUP], s16[1:])
    y_pool = []
    for g, w in enumerate(POOL_WINDOWS):
        cs = slice(g * POOL_GROUP, (g + 1) * POOL_GROUP)
        inv_head = 1.0 / jnp.minimum(t_head + 1, w).astype(_F32)
        inv_head = jnp.concatenate([inv_head] * (POOL_GROUP // LANES), axis=-1)
        mean = jnp.concatenate([sums[g][:POOL_HIST] * inv_head, sums[g][POOL_HIST:] * (1.0 / w)], axis=0)
        mixed = mean - up[:, :, cs]
        yg = _dot(mixed.reshape(T, POOL_GROUP).astype(_BF16), w_pool_ref[g])
        y_pool.append(((yg + b_pool_ref[:, cs]) * pool_scale_ref[:, cs]).astype(_BF16))
    z_pool = jnp.concatenate(y_pool, axis=1)

    n_hist = CONV_WIDTH - 1
    sub = lax.broadcasted_iota(jnp.int32, (SUBLANES, LRU_HEAD_DIM), 0)
    first = t_head[0:1] == 0
    gate_cols = 2 * LRU_HEAD_DIM
    out_cols = D_MODEL // LRU_HEADS
    y_top, z_lru = [], []
    for h in range(LRU_HEADS):
        hs = slice(h * LRU_HEAD_DIM, (h + 1) * LRU_HEAD_DIM)
        ul = u_lru[:, :, hs]
        ext2 = jnp.concatenate([_history_rows(ul[lc - n_hist:], conv_hist[:, :, hs]), ul], axis=0)
        conv_hist[:, :, hs] = ul[lc - n_hist:]
        xc = ext2[0:lc] * conv_w_ref[0:1, hs]
        for k in range(1, CONV_WIDTH):
            xc = xc + ext2[k:k + lc] * conv_w_ref[k:k + 1, hs]
        xc = xc + conv_b_ref[:, hs]

        pre = _dot(xc.reshape(T, LRU_HEAD_DIM).astype(_BF16), w_ax_ref[h])
        pre = pre.reshape(lc, SUBLANES, 2 * LRU_HEAD_DIM)
        r = jax.nn.sigmoid(pre[:, :, :LRU_HEAD_DIM] + b_a_ref[:, hs])
        i = jax.nn.sigmoid(pre[:, :, LRU_HEAD_DIM:] + b_x_ref[:, hs])

        if h % 2 == 0:
            c0 = POOL_WIDTH + LRU_WIDTH + h * LRU_HEAD_DIM
            u_gate = _dot(xb, _unpack_rows(w_in_ref[:, c0:c0 + gate_cols])).reshape(lc, SUBLANES, gate_cols)
        else:
            c0 = (h // 2) * out_cols
            y_top.append(_dot(z_pool, _unpack_rows(w_out_ref[:POOL_WIDTH // 2, c0:c0 + out_cols])))

        log_a = (-LRU_C * r) * jax.nn.softplus(-lam_ref[:, hs])
        a = jnp.exp(log_a)
        u = -jnp.tanh(log_a) * (a * a + 1.0)
        mult = jnp.where(u == 0.0, 0.0, u * lax.rsqrt(u))
        mult = jnp.concatenate([jnp.where(first, 1.0, mult[0:1]), mult[1:]], axis=0)
        bv = mult * (i * xc)

        hl, ac = [bv[0]], [a[0]]
        for k in range(1, lc):
            hl.append(a[k] * hl[-1] + bv[k])
            ac.append(a[k] * ac[-1])
        c_a, c_b = ac[-1], hl[-1]
        for d in (1, 2, 4):
            a_s = jnp.where(sub < d, 1.0, pltpu.roll(c_a, d, 0))
            b_s = jnp.where(sub < d, 0.0, pltpu.roll(c_b, d, 0))
            c_b = c_a * b_s + c_b
            c_a = c_a * a_s
        h_prev = h_carry[:, hs]
        h_end = c_b + c_a * h_prev
        h_in = jnp.where(sub == 0, h_prev, pltpu.roll(h_end, 1, 0))
        h_carry[:, hs] = jnp.broadcast_to(h_end[SUBLANES - 1:SUBLANES, :], (SUBLANES, LRU_HEAD_DIM))
        hseq = jnp.stack(hl, axis=0) + jnp.stack(ac, axis=0) * h_in
        gate = jax.nn.gelu(u_gate[:, :, (h % 2) * LRU_HEAD_DIM:(h % 2 + 1) * LRU_HEAD_DIM])
        z_lru.append((hseq * gate).reshape(T, LRU_HEAD_DIM).astype(_BF16))

    for q in range(LRU_HEADS // 2, LRU_HEADS):
        y_top.append(_dot(z_pool, _unpack_rows(w_out_ref[:POOL_WIDTH // 2, q * out_cols:(q + 1) * out_cols])))
    y = (jnp.concatenate(y_top, axis=1)
         + _dot(jnp.concatenate(z_lru, axis=1), _unpack_rows(w_out_ref[POOL_WIDTH // 2:, :])))
    resid_ref[slot] = DEEPNORM_ALPHA * x + y


def _const_spec(shape, single_buffer=False):
    nd = len(shape)
    kwargs = {"pipeline_mode": pl.Buffered(1)} if single_buffer else {}
    return pl.BlockSpec(shape, lambda *_: (0,) * nd, **kwargs)


def _mixer(x, w_in, conv_w, conv_b, w_ax, b_a, b_x, lam, w_pool, b_pool, pool_scale, w_out, g, b,
           cast_weights):
    B, S, D = x.shape
    T = MIXER_TILE
    n_seq = S // T
    n_steps = B * n_seq
    lc = T // SUBLANES
    assert T % SUBLANES == 0 and lc >= POOL_HIST and lc % (2 * SUBLANES) == 0 and n_steps >= 3

    chunk_in, chunk_out = [], []
    for w in cast_weights:
        assert w.shape[0] % (2 * SUBLANES * n_steps) == 0
        rows_w = w.shape[0] // n_steps
        chunk_in.append(pl.BlockSpec((rows_w, w.shape[1]), lambda i: (jnp.minimum(i, n_steps - 1), 0)))
        chunk_out.append(pl.BlockSpec((rows_w // 2, w.shape[1]), lambda i: (jnp.minimum(i, n_steps - 1), 0)))
    return pl.pallas_call(
        functools.partial(_mixer_kernel, n_seq=n_seq),
        grid=(n_steps + 1,),
        in_specs=[
            pl.BlockSpec(memory_space=pl.ANY),
            _const_spec(w_in.shape, True),
            _const_spec(conv_w.shape), _const_spec(conv_b.shape),
            _const_spec(w_ax.shape, True),
            _const_spec(b_a.shape), _const_spec(b_x.shape), _const_spec(lam.shape),
            _const_spec(w_pool.shape, True),
            _const_spec(b_pool.shape), _const_spec(pool_scale.shape),
            _const_spec(w_out.shape, True),
            _const_spec(g.shape), _const_spec(b.shape),
        ] + chunk_in,
        out_specs=[pl.BlockSpec(memory_space=pl.ANY)] + chunk_out,
        out_shape=[jax.ShapeDtypeStruct(x.shape, _F32)]
        + [jax.ShapeDtypeStruct((w.shape[0] // 2, w.shape[1]), jnp.uint32) for w in cast_weights],
        scratch_shapes=[
            pltpu.VMEM((2, lc, SUBLANES, D), _F32),
            pltpu.VMEM((2, lc, SUBLANES, D), _F32),
            pltpu.SemaphoreType.DMA((2, SUBLANES)),
            pltpu.SemaphoreType.DMA((2, SUBLANES)),
            pltpu.VMEM((POOL_HIST, SUBLANES, POOL_WIDTH), _F32),
            pltpu.VMEM((CONV_WIDTH - 1, SUBLANES, LRU_WIDTH), _F32),
            pltpu.VMEM((SUBLANES, LRU_WIDTH), _F32),
            pltpu.VMEM((2, T, D), _F32),
        ],
        compiler_params=pltpu.CompilerParams(
            dimension_semantics=("arbitrary",), vmem_limit_bytes=VMEM_LIMIT_BYTES),
        name="mixer",
    )(x, w_in, conv_w, conv_b, w_ax, b_a, b_x, lam, w_pool, b_pool, pool_scale, w_out, g, b, *cast_weights)


def _kv_kernel(mem_ref, wk_ref, wv_ref, *rest):
    n_cast = (len(rest) - 3) // 2
    cast_in, k_ref, v_ref, cast_out = rest[:n_cast], rest[n_cast], rest[n_cast + 1], rest[n_cast + 2:-1]
    mb_ref = rest[-1]
    for src, dst in zip(cast_in, cast_out):
        dst[...] = _pack_rows(src[...])

    @pl.when(pl.program_id(0) == 0)
    def _():
        mb_ref[...] = mem_ref[...].astype(_BF16)

    k_ref[...] = _pack_rows(_dot(mb_ref[...], wk_ref[...].astype(_BF16)))
    v_ref[...] = _pack_rows(_dot(mb_ref[...], wv_ref[...].astype(_BF16)))


def _kv_proj(mem2d, w_k, w_v, cast_weights):
    M, D = mem2d.shape
    tn = KV_TILE_N
    n_steps = D // tn
    wspec = pl.BlockSpec((D, tn), lambda j: (0, j))
    ospec = pl.BlockSpec((M // 2, tn), lambda j: (0, j))
    chunk_in, chunk_out = [], []
    for w in cast_weights:
        assert w.shape[0] % (2 * SUBLANES * n_steps) == 0
        rows_w = w.shape[0] // n_steps
        chunk_in.append(pl.BlockSpec((rows_w, w.shape[1]), lambda j: (j, 0)))
        chunk_out.append(pl.BlockSpec((rows_w // 2, w.shape[1]), lambda j: (j, 0)))
    return pl.pallas_call(
        _kv_kernel,
        grid=(n_steps,),
        in_specs=[_const_spec(mem2d.shape, True), wspec, wspec] + chunk_in,
        out_specs=[ospec, ospec] + chunk_out,
        out_shape=[jax.ShapeDtypeStruct((M // 2, D), jnp.uint32)] * 2
        + [jax.ShapeDtypeStruct((w.shape[0] // 2, w.shape[1]), jnp.uint32) for w in cast_weights],
        scratch_shapes=[pltpu.VMEM((M, D), _BF16)],
        compiler_params=pltpu.CompilerParams(
            dimension_semantics=("arbitrary",), vmem_limit_bytes=VMEM_LIMIT_BYTES),
        name="kv_proj",
    )(mem2d, w_k, w_v, *cast_weights)


def _xattn_kernel(x_ref, k_ref, v_ref, wq_ref, wo_ref, g_ref, b_ref, o_ref, resid_ref):
    i = pl.program_id(0)
    n_tiles = pl.num_programs(0) - 1
    slot = i % 2

    @pl.when(i == 0)
    def _():
        resid_ref[1] = jnp.zeros(resid_ref.shape[1:], _F32)

    def finish_previous():
        return _normalize_previous(resid_ref, slot, g_ref, b_ref, o_ref)

    @pl.when(i < n_tiles)
    def _():
        zeros = finish_previous()
        x = x_ref[0]
        xb = x.astype(_BF16)
        T = x.shape[0]
        scale = XATTN_HEAD_DIM ** -0.5
        heads = [slice(h * XATTN_HEAD_DIM, (h + 1) * XATTN_HEAD_DIM) for h in range(XATTN_HEADS)]

        blocks_per_head = XATTN_HEAD_DIM // ANCHOR_COLS
        anchors = [[] for _ in range(XATTN_HEADS * blocks_per_head)]
        for c, z in enumerate(zeros):
            anchors[min(c + 1, len(anchors) - 1)].append(z)

        def query(h):
            q = _dot(xb, _unpack_rows(wq_ref[:, heads[h]]))
            cols = []
            for blk in range(blocks_per_head):
                qb = q[:, blk * ANCHOR_COLS:(blk + 1) * ANCHOR_COLS]
                for z in anchors[h * blocks_per_head + blk]:
                    qb = qb + jnp.tile(z, (T // SUBLANES, ANCHOR_COLS // LANES))
                cols.append(qb)
            return jnp.concatenate(cols, axis=1).astype(_BF16)

        def probs(q, h):
            sc = lax.dot_general(q, _unpack_rows(k_ref[0, :, heads[h]]), (((1,), (1,)), ((), ())),
                                 preferred_element_type=_F32) * scale
            e = jnp.exp(sc - jnp.max(sc, axis=-1, keepdims=True))
            return (e * (1.0 / jnp.sum(e, axis=-1, keepdims=True))).astype(_BF16)

        def attend(p, h):
            return _dot(p, _unpack_rows(v_ref[0, :, heads[h]])).astype(_BF16)

        half_rows = wo_ref.shape[0] // 2
        q0 = query(0)
        q1 = query(1)
        p0 = probs(q0, 0)
        q2 = query(2)
        o0 = attend(p0, 0)
        p1 = probs(q1, 1)
        q3 = query(3)
        o1 = attend(p1, 1)
        p2 = probs(q2, 2)
        y = _dot(jnp.concatenate([o0, o1], axis=1), _unpack_rows(wo_ref[:half_rows, :]))
        o2 = attend(p2, 2)
        p3 = probs(q3, 3)
        o3 = attend(p3, 3)
        y = y + _dot(jnp.concatenate([o2, o3], axis=1), _unpack_rows(wo_ref[half_rows:, :]))
        resid_ref[slot] = DEEPNORM_ALPHA * x + y

    @pl.when(i == n_tiles)
    def _():
        finish_previous()


def _xattn(x, k, v, w_q, w_o, g, b):
    B, S, D = x.shape
    T = XATTN_TILE
    n_seq = S // T
    n_tiles = B * n_seq

    def cur(i):
        t = jnp.minimum(i, n_tiles - 1)
        return t // n_seq, t % n_seq

    def prev(i):
        t = jnp.maximum(i - 1, 0)
        return t // n_seq, t % n_seq

    kvspec = pl.BlockSpec((1, N_MEM // 2, D), lambda i: (cur(i)[0], 0, 0))
    return pl.pallas_call(
        _xattn_kernel,
        grid=(n_tiles + 1,),
        in_specs=[pl.BlockSpec((1, T, D), lambda i: (*cur(i), 0)), kvspec, kvspec,
                  _const_spec(w_q.shape, True), _const_spec(w_o.shape, True),
                  _const_spec(g.shape), _const_spec(b.shape)],
        out_specs=pl.BlockSpec((1, T, D), lambda i: (*prev(i), 0)),
        out_shape=jax.ShapeDtypeStruct(x.shape, _F32),
        scratch_shapes=[pltpu.VMEM((2, T, D), _F32)],
        compiler_params=pltpu.CompilerParams(
            dimension_semantics=("arbitrary",), vmem_limit_bytes=VMEM_LIMIT_BYTES),
        name="xattn",
    )(x, k, v, w_q, w_o, g, b)


def _mlp_kernel(x_hbm, w1_ref, w2_ref, g_ref, b_ref, o_hbm, xstage, xb_ref, acc_ref, oslice, x_sem, o_sem,
                *, n_f, halves):
    i, j = pl.program_id(0), pl.program_id(1)
    n_tiles = pl.num_programs(0) - 1
    slot = i % 2
    tm = xstage.shape[0]
    rows = tm // n_f
    q = i * n_f + j
    k = q % 2

    def x_copy(t):
        return pltpu.make_async_copy(x_hbm.at[pl.ds(pl.multiple_of(t * tm, tm), tm), :], xstage, x_sem.at[0])

    def out_copy(step, buf):
        r0 = pl.multiple_of((step // n_f - 1) * tm + (step % n_f) * rows, rows)
        return pltpu.make_async_copy(oslice.at[buf], o_hbm.at[pl.ds(r0, rows), :], o_sem.at[buf])

    @pl.when(q == 0)
    def _():
        x_copy(0).start()
        acc_ref[1] = jnp.zeros(acc_ref.shape[1:], _F32)

    @pl.when((i < n_tiles) & (j == 0))
    def _():
        x_copy(i).wait()
        xb_ref[...] = xstage[...].astype(_BF16)

    @pl.when(q >= n_f + 2)
    def _():
        out_copy(q - 2, k).wait()

    def finish_previous_rows():
        r0 = pl.multiple_of(j * rows, rows)
        oslice[k] = _layer_norm(acc_ref[1 - slot, pl.ds(r0, rows), :], g_ref[...], b_ref[...])

    def accumulate(first):
        finish_previous_rows()
        w1 = _unpack_rows(w1_ref[...])
        w2 = _unpack_rows(w2_ref[...])
        half = tm // halves
        hs = [jnp.square(jnp.maximum(_dot(xb_ref[r * half:(r + 1) * half, :], w1), 0.0)).astype(_BF16)
              for r in range(halves)]
        for r in range(halves):
            rs = slice(r * half, (r + 1) * half)
            prior = DEEPNORM_ALPHA * xstage[rs, :] if first else acc_ref[slot, rs, :]
            acc_ref[slot, rs, :] = prior + _dot(hs[r], w2)

    @pl.when((i < n_tiles) & (j == 0))
    def _():
        accumulate(first=True)

    @pl.when((i < n_tiles) & (j > 0))
    def _():
        accumulate(first=False)

    @pl.when((i + 1 < n_tiles) & (j == 0))
    def _():
        x_copy(i + 1).start()

    @pl.when(i == n_tiles)
    def _():
        finish_previous_rows()

    @pl.when(q >= n_f)
    def _():
        out_copy(q, k).start()

    @pl.when(q == (n_tiles + 1) * n_f - 1)
    def _():
        out_copy(q - 1, 1 - k).wait()
        out_copy(q, k).wait()


def _mlp(x2d, w1, w2, g, b):
    M, D = x2d.shape
    F = w1.shape[1]
    tm, tf = MLP_TILE_M, MLP_TILE_F
    n_tiles, n_f = M // tm, F // tf
    assert tm % (n_f * SUBLANES) == 0 and tm % MLP_ROW_HALVES == 0 and n_f >= 2

    def wblk(i, j):
        return jnp.where(i == n_tiles, n_f - 1, j)

    return pl.pallas_call(
        functools.partial(_mlp_kernel, n_f=n_f, halves=MLP_ROW_HALVES),
        grid=(n_tiles + 1, n_f),
        in_specs=[pl.BlockSpec(memory_space=pl.ANY),
                  pl.BlockSpec((D // 2, tf), lambda i, j: (0, wblk(i, j))),
                  pl.BlockSpec((tf // 2, D), lambda i, j: (wblk(i, j), 0)),
                  _const_spec(g.shape), _const_spec(b.shape)],
        out_specs=pl.BlockSpec(memory_space=pl.ANY),
        out_shape=jax.ShapeDtypeStruct((M, D), _F32),
        scratch_shapes=[pltpu.VMEM((tm, D), _F32),
                        pltpu.VMEM((tm, D), _BF16),
                        pltpu.VMEM((2, tm, D), _F32),
                        pltpu.VMEM((2, tm // n_f, D), _F32),
                        pltpu.SemaphoreType.DMA((1,)),
                        pltpu.SemaphoreType.DMA((2,))],
        compiler_params=pltpu.CompilerParams(
            dimension_semantics=("arbitrary", "arbitrary"),
            vmem_limit_bytes=VMEM_LIMIT_BYTES),
        name="mlp",
    )(x2d, w1, w2, g, b)


def kernel(x, mem, w_in, conv_w, conv_b, w_a, b_a, w_x, b_x, lru_lambda, w_pool, b_pool, pool_scale,
           w_out, ln1_g, ln1_b, w_q, w_k, w_v, w_o, ln2_g, ln2_b, w_ff1, w_ff2, ln3_g, ln3_b):
    B, S, D = x.shape
    row = lambda p: p.reshape(1, -1)
    for l in range(DEPTH):
        w_ax = jnp.concatenate([w_a[l], w_x[l]], axis=-1).astype(_BF16)
        k, v, w_in_b, w_out_b = _kv_proj(mem.reshape(B * N_MEM, D), w_k[l], w_v[l], (w_in[l], w_out[l]))
        x, wq_b, wo_b, w1_b, w2_b = _mixer(
            x, w_in_b, conv_w[l], row(conv_b[l]), w_ax,
            row(b_a[l]), row(b_x[l]), row(lru_lambda[l]),
            w_pool[l].astype(_BF16), row(b_pool[l]), row(pool_scale[l]),
            w_out_b, row(ln1_g[l]), row(ln1_b[l]),
            (w_q[l], w_o[l], w_ff1[l], w_ff2[l]))
        x = _xattn(x, k.reshape(B, N_MEM // 2, D), v.reshape(B, N_MEM // 2, D),
                   wq_b, wo_b, row(ln2_g[l]), row(ln2_b[l]))
        x = _mlp(x.reshape(B * S, D), w1_b, w2_b, row(ln3_g[l]), row(ln3_b[l])).reshape(B, S, D)
    return x
```

```python
import functools

import jax
import jax.numpy as jnp
from jax import lax
from jax.experimental import pallas as pl
from jax.experimental.pallas import tpu as pltpu

D_MODEL = 2048
POOL_WIDTH = 1024
LRU_WIDTH = 1024
POOL_WINDOWS = (2, 4, 8, 16)
POOL_GROUP = 256
LRU_HEADS = 8
LRU_HEAD_DIM = 128
CONV_WIDTH = 4
LRU_C = 8.0
N_MEM = 256
XATTN_HEADS = 4
XATTN_HEAD_DIM = 512
D_FF = 4 * D_MODEL
LN_EPS = 1e-5
DEPTH = 1
DEEPNORM_ALPHA = (2.0 * DEPTH) ** 0.25

POOL_HIST = 16
SUBLANES = 8
LANES = 128

MIXER_TILE = 256
XATTN_TILE = 512
ANCHOR_COLS = 256
MLP_TILE_M = 1024
MLP_ROW_HALVES = 2
MLP_TILE_F = 1024
KV_TILE_N = 256
VMEM_LIMIT_BYTES = 56 * 1024 * 1024

_F32 = jnp.float32
_BF16 = jnp.bfloat16


def _layer_norm(v, g, b):
    mu = jnp.mean(v, axis=-1, keepdims=True)
    c = v - mu
    var = jnp.mean(jnp.square(c), axis=-1, keepdims=True)
    return c * lax.rsqrt(var + LN_EPS) * g + b


def _dot(a, b):
    return jnp.dot(a, b, preferred_element_type=_F32)


def _pack_rows(v):
    return pltpu.bitcast(v.astype(_BF16), jnp.uint32)


def _unpack_rows(w):
    return pltpu.bitcast(w, _BF16)


def _ordering_zero(v):
    rows, cols = v.shape
    t = jnp.sum(v.reshape(rows // SUBLANES, SUBLANES, cols), axis=0)
    t = functools.reduce(lambda a, b: a + b, [t[:, c:c + LANES] for c in range(0, cols, LANES)])
    return jnp.minimum(jnp.abs(t), 0.0)


def _normalize_previous(resid_ref, slot, g_ref, b_ref, o_ref):
    T, D = resid_ref.shape[1:]
    chunk = T // (D // ANCHOR_COLS)
    zeros = []
    for r0 in range(0, T, chunk):
        out = _layer_norm(resid_ref[1 - slot, r0:r0 + chunk, :], g_ref[...], b_ref[...])
        o_ref[0, r0:r0 + chunk, :] = out
        zeros.append(_ordering_zero(out))
    return zeros


def _tile_zeros(zeros, rows, cols):
    width = cols // len(zeros)
    return jnp.concatenate([jnp.tile(z, (rows // SUBLANES, width // LANES)) for z in zeros], axis=1)


def _history_rows(cur_tail, prev_tail):
    sub = lax.broadcasted_iota(jnp.int32, (1,) + cur_tail.shape[1:], 1)
    return jnp.where(sub == 0, pltpu.roll(prev_tail, 1, 1), pltpu.roll(cur_tail, 1, 1))


def _tile_copies(hbm_ref, buf, sems, t, sl, n_seq, to_hbm):
    lc = buf.shape[1]
    b, s = t // n_seq, t % n_seq
    copies = []
    for j in range(SUBLANES):
        rows = hbm_ref.at[b, pl.ds(pl.multiple_of((s * SUBLANES + j) * lc, lc), lc), :]
        chunk = buf.at[sl, :, j, :]
        src, dst = (chunk, rows) if to_hbm else (rows, chunk)
        copies.append(pltpu.make_async_copy(src, dst, sems.at[sl]))
    return copies


def _mixer_kernel(x_hbm, w_in_ref, conv_w_ref, conv_b_ref, w_ax_ref, b_a_ref, b_x_ref, lam_ref,
                  w_pool_ref, b_pool_ref, pool_scale_ref, w_out_ref, g_ref, b_ref, *rest, n_seq):
    n_cast = (len(rest) - 9) // 2
    cast_in, o_hbm, cast_out = rest[:n_cast], rest[n_cast], rest[n_cast + 1:2 * n_cast + 1]
    xbuf, obuf, in_sem, out_sem, pool_hist, conv_hist, h_carry, resid_ref = rest[2 * n_cast + 1:]
    i = pl.program_id(0)
    n_tiles = pl.num_programs(0) - 1
    s = jnp.minimum(i, n_tiles - 1) % n_seq
    slot = i % 2

    def x_copies(t, sl):
        return _tile_copies(x_hbm, xbuf, in_sem, t, sl, n_seq, to_hbm=False)

    def out_copies(t, sl):
        return _tile_copies(o_hbm, obuf, out_sem, t, sl, n_seq, to_hbm=True)

    @pl.when(i == 0)
    def _():
        for c in x_copies(0, 0):
            c.start()
        resid_ref[1] = jnp.zeros(resid_ref.shape[1:], _F32)

    @pl.when(i < n_tiles)
    def _():
        for c in x_copies(i, slot):
            c.wait()

    @pl.when(i + 1 < n_tiles)
    def _():
        for c in x_copies(i + 1, 1 - slot):
            c.start()

    @pl.when(i >= 3)
    def _():
        for c in out_copies(i - 3, 1 - slot):
            c.wait()

    def finish_previous():
        T, D = resid_ref.shape[1:]
        chunk = T // (D // ANCHOR_COLS)
        zeros = []
        for r0 in range(0, T, chunk):
            out = _layer_norm(resid_ref[1 - slot, r0:r0 + chunk, :], g_ref[...], b_ref[...])
            obuf[1 - slot, r0 // SUBLANES:(r0 + chunk) // SUBLANES] = out.reshape(chunk // SUBLANES, SUBLANES, D)
            zeros.append(_ordering_zero(out))
        return zeros

    @pl.when(i == n_tiles)
    def _():
        finish_previous()

    @pl.when(i < n_tiles)
    def _():
        _mixer_tile(s, slot, finish_previous, xbuf, w_in_ref, conv_w_ref, conv_b_ref, w_ax_ref, b_a_ref, b_x_ref,
                    lam_ref, w_pool_ref, b_pool_ref, pool_scale_ref, w_out_ref, cast_in, cast_out,
                    pool_hist, conv_hist, h_carry, resid_ref)

    @pl.when(i >= 1)
    def _():
        for c in out_copies(i - 1, 1 - slot):
            c.start()

    @pl.when(i == n_tiles)
    def _():
        for c in out_copies(n_tiles - 2, slot) + out_copies(n_tiles - 1, 1 - slot):
            c.wait()


def _mixer_tile(s, slot, finish_previous, xbuf, w_in_ref, conv_w_ref, conv_b_ref, w_ax_ref, b_a_ref, b_x_ref,
                lam_ref, w_pool_ref, b_pool_ref, pool_scale_ref, w_out_ref, cast_in, cast_out,
                pool_hist, conv_hist, h_carry, resid_ref):
    lc, _, D = xbuf.shape[1:]
    T = lc * SUBLANES

    @pl.when(s == 0)
    def _():
        pool_hist[...] = jnp.zeros_like(pool_hist)
        conv_hist[...] = jnp.zeros_like(conv_hist)
        h_carry[...] = jnp.zeros_like(h_carry)

    zeros = finish_previous()
    for src, dst in zip(cast_in, cast_out):
        dst[...] = _pack_rows(src[...])

    x = xbuf[slot].reshape(T, D)
    xb = x.astype(_BF16)
    n_lru = LRU_WIDTH // LANES
    half = len(zeros) // 2
    up = _dot(xb, _unpack_rows(w_in_ref[:, :POOL_WIDTH])) + _tile_zeros(zeros[:half], T, POOL_WIDTH)
    up = up.reshape(lc, SUBLANES, POOL_WIDTH)
    u_lru = (_dot(xb, _unpack_rows(w_in_ref[:, POOL_WIDTH:POOL_WIDTH + LRU_WIDTH]))
             + _tile_zeros(zeros[half:], T, LRU_WIDTH)).reshape(lc, SUBLANES, LRU_WIDTH)

    kk = lax.broadcasted_iota(jnp.int32, (POOL_HIST, SUBLANES, LANES), 0)
    jj = lax.broadcasted_iota(jnp.int32, (POOL_HIST, SUBLANES, LANES), 1)
    t_head = s * T + jj * lc + kk

    ext = jnp.concatenate([_history_rows(up[lc - POOL_HIST:], pool_hist[...]), up], axis=0)
    pool_hist[...] = up[lc - POOL_HIST:]
    s2 = ext[1:] + ext[:-1]
    s2r = s2[:, :, POOL_GROUP:]
    s4 = s2r[2:] + s2r[:-2]
    s4r = s4[:, :, POOL_GROUP:]
    s8 = s4r[4:] + s4r[:-4]
    s8r = s8[:, :, POOL_GROUP:]
    s16 = s8r[8:] + s8r[:-8]
    sums = (s2[15:, :, :POOL_GROUP], s4[13:, :, :POOL_GROUP], s8[9:, :, :POOL_GROUP], s16[1:])
    y_pool = []
    for g, w in enumerate(POOL_WINDOWS):
        cs = slice(g * POOL_GROUP, (g + 1) * POOL_GROUP)
        inv_head = 1.0 / jnp.minimum(t_head + 1, w).astype(_F32)
        inv_head = jnp.concatenate([inv_head] * (POOL_GROUP // LANES), axis=-1)
        mean = jnp.concatenate([sums[g][:POOL_HIST] * inv_head, sums[g][POOL_HIST:] * (1.0 / w)], axis=0)
        mixed = mean - up[:, :, cs]
        yg = _dot(mixed.reshape(T, POOL_GROUP).astype(_BF16), w_pool_ref[g])
        y_pool.append(((yg + b_pool_ref[:, cs]) * pool_scale_ref[:, cs]).astype(_BF16))
    z_pool = jnp.concatenate(y_pool, axis=1)

    n_hist = CONV_WIDTH - 1
    sub = lax.broadcasted_iota(jnp.int32, (SUBLANES, LRU_HEAD_DIM), 0)
    first = t_head[0:1] == 0
    gate_cols = 2 * LRU_HEAD_DIM
    out_cols = D_MODEL // LRU_HEADS
    y_top, z_lru = [], []
    for h in range(LRU_HEADS):
        hs = slice(h * LRU_HEAD_DIM, (h + 1) * LRU_HEAD_DIM)
        ul = u_lru[:, :, hs]
        ext2 = jnp.concatenate([_history_rows(ul[lc - n_hist:], conv_hist[:, :, hs]), ul], axis=0)
        conv_hist[:, :, hs] = ul[lc - n_hist:]
        xc = ext2[0:lc] * conv_w_ref[0:1, hs]
        for k in range(1, CONV_WIDTH):
            xc = xc + ext2[k:k + lc] * conv_w_ref[k:k + 1, hs]
        xc = xc + conv_b_ref[:, hs]

        pre = _dot(xc.reshape(T, LRU_HEAD_DIM).astype(_BF16), w_ax_ref[h])
        pre = pre.reshape(lc, SUBLANES, 2 * LRU_HEAD_DIM)
        r = jax.nn.sigmoid(pre[:, :, :LRU_HEAD_DIM] + b_a_ref[:, hs])
        i = jax.nn.sigmoid(pre[:, :, LRU_HEAD_DIM:] + b_x_ref[:, hs])

        if h % 2 == 0:
            c0 = POOL_WIDTH + LRU_WIDTH + h * LRU_HEAD_DIM
            u_gate = _dot(xb, _unpack_rows(w_in_ref[:, c0:c0 + gate_cols])).reshape(lc, SUBLANES, gate_cols)
        else:
            c0 = (h // 2) * out_cols
            y_top.append(_dot(z_pool, _unpack_rows(w_out_ref[:POOL_WIDTH // 2, c0:c0 + out_cols])))

        log_a = (-LRU_C * r) * jax.nn.softplus(-lam_ref[:, hs])
        a = jnp.exp(log_a)
        u = -jnp.tanh(log_a) * (a * a + 1.0)
        mult = jnp.where(u == 0.0, 0.0, u * lax.rsqrt(u))
        mult = jnp.concatenate([jnp.where(first, 1.0, mult[0:1]), mult[1:]], axis=0)
        bv = mult * (i * xc)

        hl, ac = [bv[0]], [a[0]]
        for k in range(1, lc):
            hl.append(a[k] * hl[-1] + bv[k])
            ac.append(a[k] * ac[-1])
        c_a, c_b = ac[-1], hl[-1]
        for d in (1, 2, 4):
            a_s = jnp.where(sub < d, 1.0, pltpu.roll(c_a, d, 0))
            b_s = jnp.where(sub < d, 0.0, pltpu.roll(c_b, d, 0))
            c_b = c_a * b_s + c_b
            c_a = c_a * a_s
        h_prev = h_carry[:, hs]
        h_end = c_b + c_a * h_prev
        h_in = jnp.where(sub == 0, h_prev, pltpu.roll(h_end, 1, 0))
        h_carry[:, hs] = jnp.broadcast_to(h_end[SUBLANES - 1:SUBLANES, :], (SUBLANES, LRU_HEAD_DIM))
        hseq = jnp.stack(hl, axis=0) + jnp.stack(ac, axis=0) * h_in
        gate = jax.nn.gelu(u_gate[:, :, (h % 2) * LRU_HEAD_DIM:(h % 2 + 1) * LRU_HEAD_DIM])
        z_lru.append((hseq * gate).reshape(T, LRU_HEAD_DIM).astype(_BF16))

    for q in range(LRU_HEADS // 2, LRU_HEADS):
        y_top.append(_dot(z_pool, _unpack_rows(w_out_ref[:POOL_WIDTH // 2, q * out_cols:(q + 1) * out_cols])))
    y = (jnp.concatenate(y_top, axis=1)
         + _dot(jnp.concatenate(z_lru, axis=1), _unpack_rows(w_out_ref[POOL_WIDTH // 2:, :])))
    resid_ref[slot] = DEEPNORM_ALPHA * x + y


def _const_spec(shape, single_buffer=False):
    nd = len(shape)
    kwargs = {"pipeline_mode": pl.Buffered(1)} if single_buffer else {}
    return pl.BlockSpec(shape, lambda *_: (0,) * nd, **kwargs)


def _mixer(x, w_in, conv_w, conv_b, w_ax, b_a, b_x, lam, w_pool, b_pool, pool_scale, w_out, g, b,
           cast_weights):
    B, S, D = x.shape
    T = MIXER_TILE
    n_seq = S // T
    n_steps = B * n_seq
    lc = T // SUBLANES
    assert T % SUBLANES == 0 and lc >= POOL_HIST and lc % (2 * SUBLANES) == 0 and n_steps >= 3

    chunk_in, chunk_out = [], []
    for w in cast_weights:
        assert w.shape[0] % (2 * SUBLANES * n_steps) == 0
        rows_w = w.shape[0] // n_steps
        chunk_in.append(pl.BlockSpec((rows_w, w.shape[1]), lambda i: (jnp.minimum(i, n_steps - 1), 0)))
        chunk_out.append(pl.BlockSpec((rows_w // 2, w.shape[1]), lambda i: (jnp.minimum(i, n_steps - 1), 0)))
    return pl.pallas_call(
        functools.partial(_mixer_kernel, n_seq=n_seq),
        grid=(n_steps + 1,),
        in_specs=[
            pl.BlockSpec(memory_space=pl.ANY),
            _const_spec(w_in.shape, True),
            _const_spec(conv_w.shape), _const_spec(conv_b.shape),
            _const_spec(w_ax.shape, True),
            _const_spec(b_a.shape), _const_spec(b_x.shape), _const_spec(lam.shape),
            _const_spec(w_pool.shape, True),
            _const_spec(b_pool.shape), _const_spec(pool_scale.shape),
            _const_spec(w_out.shape, True),
            _const_spec(g.shape), _const_spec(b.shape),
        ] + chunk_in,
        out_specs=[pl.BlockSpec(memory_space=pl.ANY)] + chunk_out,
        out_shape=[jax.ShapeDtypeStruct(x.shape, _F32)]
        + [jax.ShapeDtypeStruct((w.shape[0] // 2, w.shape[1]), jnp.uint32) for w in cast_weights],
        scratch_shapes=[
            pltpu.VMEM((2, lc, SUBLANES, D), _F32),
            pltpu.VMEM((2, lc, SUBLANES, D), _F32),
            pltpu.SemaphoreType.DMA((2,)),
            pltpu.SemaphoreType.DMA((2,)),
            pltpu.VMEM((POOL_HIST, SUBLANES, POOL_WIDTH), _F32),
            pltpu.VMEM((CONV_WIDTH - 1, SUBLANES, LRU_WIDTH), _F32),
            pltpu.VMEM((SUBLANES, LRU_WIDTH), _F32),
            pltpu.VMEM((2, T, D), _F32),
        ],
        compiler_params=pltpu.CompilerParams(
            dimension_semantics=("arbitrary",), vmem_limit_bytes=VMEM_LIMIT_BYTES),
        name="mixer",
    )(x, w_in, conv_w, conv_b, w_ax, b_a, b_x, lam, w_pool, b_pool, pool_scale, w_out, g, b, *cast_weights)


def _kv_kernel(mem_ref, wk_ref, wv_ref, *rest):
    n_cast = (len(rest) - 3) // 2
    cast_in, k_ref, v_ref, cast_out = rest[:n_cast], rest[n_cast], rest[n_cast + 1], rest[n_cast + 2:-1]
    mb_ref = rest[-1]
    for src, dst in zip(cast_in, cast_out):
        dst[...] = _pack_rows(src[...])

    @pl.when(pl.program_id(0) == 0)
    def _():
        mb_ref[...] = mem_ref[...].astype(_BF16)

    k_ref[...] = _pack_rows(_dot(mb_ref[...], wk_ref[...].astype(_BF16)))
    v_ref[...] = _pack_rows(_dot(mb_ref[...], wv_ref[...].astype(_BF16)))


def _kv_proj(mem2d, w_k, w_v, cast_weights):
    M, D = mem2d.shape
    tn = KV_TILE_N
    n_steps = D // tn
    wspec = pl.BlockSpec((D, tn), lambda j: (0, j))
    ospec = pl.BlockSpec((M // 2, tn), lambda j: (0, j))
    chunk_in, chunk_out = [], []
    for w in cast_weights:
        assert w.shape[0] % (2 * SUBLANES * n_steps) == 0
        rows_w = w.shape[0] // n_steps
        chunk_in.append(pl.BlockSpec((rows_w, w.shape[1]), lambda j: (j, 0)))
        chunk_out.append(pl.BlockSpec((rows_w // 2, w.shape[1]), lambda j: (j, 0)))
    return pl.pallas_call(
        _kv_kernel,
        grid=(n_steps,),
        in_specs=[_const_spec(mem2d.shape, True), wspec, wspec] + chunk_in,
        out_specs=[ospec, ospec] + chunk_out,
        out_shape=[jax.ShapeDtypeStruct((M // 2, D), jnp.uint32)] * 2
        + [jax.ShapeDtypeStruct((w.shape[0] // 2, w.shape[1]), jnp.uint32) for w in cast_weights],
        scratch_shapes=[pltpu.VMEM((M, D), _BF16)],
        compiler_params=pltpu.CompilerParams(
            dimension_semantics=("arbitrary",), vmem_limit_bytes=VMEM_LIMIT_BYTES),
        name="kv_proj",
    )(mem2d, w_k, w_v, *cast_weights)


def _xattn_kernel(x_ref, k_ref, v_ref, wq_ref, wo_ref, g_ref, b_ref, o_ref, resid_ref):
    i = pl.program_id(0)
    n_tiles = pl.num_programs(0) - 1
    slot = i % 2

    @pl.when(i == 0)
    def _():
        resid_ref[1] = jnp.zeros(resid_ref.shape[1:], _F32)

    def finish_previous():
        return _normalize_previous(resid_ref, slot, g_ref, b_ref, o_ref)

    @pl.when(i < n_tiles)
    def _():
        zeros = finish_previous()
        x = x_ref[0]
        xb = x.astype(_BF16)
        T = x.shape[0]
        scale = XATTN_HEAD_DIM ** -0.5
        heads = [slice(h * XATTN_HEAD_DIM, (h + 1) * XATTN_HEAD_DIM) for h in range(XATTN_HEADS)]

        blocks_per_head = XATTN_HEAD_DIM // ANCHOR_COLS
        anchors = [[] for _ in range(XATTN_HEADS * blocks_per_head)]
        for c, z in enumerate(zeros):
            anchors[min(c + 1, len(anchors) - 1)].append(z)

        def query(h):
            q = _dot(xb, _unpack_rows(wq_ref[:, heads[h]]))
            cols = []
            for blk in range(blocks_per_head):
                qb = q[:, blk * ANCHOR_COLS:(blk + 1) * ANCHOR_COLS]
                for z in anchors[h * blocks_per_head + blk]:
                    qb = qb + jnp.tile(z, (T // SUBLANES, ANCHOR_COLS // LANES))
                cols.append(qb)
            return jnp.concatenate(cols, axis=1).astype(_BF16)

        def probs(q, h):
            sc = lax.dot_general(q, _unpack_rows(k_ref[0, :, heads[h]]), (((1,), (1,)), ((), ())),
                                 preferred_element_type=_F32) * scale
            e = jnp.exp(sc - jnp.max(sc, axis=-1, keepdims=True))
            return (e * (1.0 / jnp.sum(e, axis=-1, keepdims=True))).astype(_BF16)

        def attend(p, h):
            return _dot(p, _unpack_rows(v_ref[0, :, heads[h]])).astype(_BF16)

        half_rows = wo_ref.shape[0] // 2
        q0 = query(0)
        q1 = query(1)
        p0 = probs(q0, 0)
        q2 = query(2)
        o0 = attend(p0, 0)
        p1 = probs(q1, 1)
        q3 = query(3)
        o1 = attend(p1, 1)
        p2 = probs(q2, 2)
        y = _dot(jnp.concatenate([o0, o1], axis=1), _unpack_rows(wo_ref[:half_rows, :]))
        o2 = attend(p2, 2)
        p3 = probs(q3, 3)
        o3 = attend(p3, 3)
        y = y + _dot(jnp.concatenate([o2, o3], axis=1), _unpack_rows(wo_ref[half_rows:, :]))
        resid_ref[slot] = DEEPNORM_ALPHA * x + y

    @pl.when(i == n_tiles)
    def _():
        finish_previous()


def _xattn(x, k, v, w_q, w_o, g, b):
    B, S, D = x.shape
    T = XATTN_TILE
    n_seq = S // T
    n_tiles = B * n_seq

    def cur(i):
        t = jnp.minimum(i, n_tiles - 1)
        return t // n_seq, t % n_seq

    def prev(i):
        t = jnp.maximum(i - 1, 0)
        return t // n_seq, t % n_seq

    kvspec = pl.BlockSpec((1, N_MEM // 2, D), lambda i: (cur(i)[0], 0, 0))
    return pl.pallas_call(
        _xattn_kernel,
        grid=(n_tiles + 1,),
        in_specs=[pl.BlockSpec((1, T, D), lambda i: (*cur(i), 0)), kvspec, kvspec,
                  _const_spec(w_q.shape, True), _const_spec(w_o.shape, True),
                  _const_spec(g.shape), _const_spec(b.shape)],
        out_specs=pl.BlockSpec((1, T, D), lambda i: (*prev(i), 0)),
        out_shape=jax.ShapeDtypeStruct(x.shape, _F32),
        scratch_shapes=[pltpu.VMEM((2, T, D), _F32)],
        compiler_params=pltpu.CompilerParams(
            dimension_semantics=("arbitrary",), vmem_limit_bytes=VMEM_LIMIT_BYTES),
        name="xattn",
    )(x, k, v, w_q, w_o, g, b)


def _mlp_kernel(x_hbm, w1_ref, w2_ref, g_ref, b_ref, o_hbm, xstage, xb_ref, acc_ref, oslice, x_sem, o_sem,
                *, n_f, halves):
    i, j = pl.program_id(0), pl.program_id(1)
    n_tiles = pl.num_programs(0) - 1
    slot = i % 2
    tm = xstage.shape[0]
    rows = tm // n_f
    q = i * n_f + j
    k = q % 2

    def x_copy(t):
        return pltpu.make_async_copy(x_hbm.at[pl.ds(pl.multiple_of(t * tm, tm), tm), :], xstage, x_sem.at[0])

    def out_copy(step, buf):
        r0 = pl.multiple_of((step // n_f - 1) * tm + (step % n_f) * rows, rows)
        return pltpu.make_async_copy(oslice.at[buf], o_hbm.at[pl.ds(r0, rows), :], o_sem.at[buf])

    @pl.when(q == 0)
    def _():
        x_copy(0).start()
        acc_ref[1] = jnp.zeros(acc_ref.shape[1:], _F32)

    @pl.when((i < n_tiles) & (j == 0))
    def _():
        x_copy(i).wait()
        xb_ref[...] = xstage[...].astype(_BF16)

    @pl.when(q >= n_f + 2)
    def _():
        out_copy(q - 2, k).wait()

    def finish_previous_rows():
        r0 = pl.multiple_of(j * rows, rows)
        oslice[k] = _layer_norm(acc_ref[1 - slot, pl.ds(r0, rows), :], g_ref[...], b_ref[...])

    def accumulate(first):
        finish_previous_rows()
        w1 = _unpack_rows(w1_ref[...])
        w2 = _unpack_rows(w2_ref[...])
        half = tm // halves
        hs = [jnp.square(jnp.maximum(_dot(xb_ref[r * half:(r + 1) * half, :], w1), 0.0)).astype(_BF16)
              for r in range(halves)]
        for r in range(halves):
            rs = slice(r * half, (r + 1) * half)
            prior = DEEPNORM_ALPHA * xstage[rs, :] if first else acc_ref[slot, rs, :]
            acc_ref[slot, rs, :] = prior + _dot(hs[r], w2)

    @pl.when((i < n_tiles) & (j == 0))
    def _():
        accumulate(first=True)

    @pl.when((i < n_tiles) & (j > 0))
    def _():
        accumulate(first=False)

    @pl.when((i + 1 < n_tiles) & (j == 0))
    def _():
        x_copy(i + 1).start()

    @pl.when(i == n_tiles)
    def _():
        finish_previous_rows()

    @pl.when(q >= n_f)
    def _():
        out_copy(q, k).start()

    @pl.when(q == (n_tiles + 1) * n_f - 1)
    def _():
        out_copy(q - 1, 1 - k).wait()
        out_copy(q, k).wait()


def _mlp(x2d, w1, w2, g, b):
    M, D = x2d.shape
    F = w1.shape[1]
    tm, tf = MLP_TILE_M, MLP_TILE_F
    n_tiles, n_f = M // tm, F // tf
    assert tm % (n_f * SUBLANES) == 0 and tm % MLP_ROW_HALVES == 0 and n_f >= 2

    def wblk(i, j):
        return jnp.where(i == n_tiles, n_f - 1, j)

    return pl.pallas_call(
        functools.partial(_mlp_kernel, n_f=n_f, halves=MLP_ROW_HALVES),
        grid=(n_tiles + 1, n_f),
        in_specs=[pl.BlockSpec(memory_space=pl.ANY),
                  pl.BlockSpec((D // 2, tf), lambda i, j: (0, wblk(i, j))),
                  pl.BlockSpec((tf // 2, D), lambda i, j: (wblk(i, j), 0)),
                  _const_spec(g.shape), _const_spec(b.shape)],
        out_specs=pl.BlockSpec(memory_space=pl.ANY),
        out_shape=jax.ShapeDtypeStruct((M, D), _F32),
        scratch_shapes=[pltpu.VMEM((tm, D), _F32),
                        pltpu.VMEM((tm, D), _BF16),
                        pltpu.VMEM((2, tm, D), _F32),
                        pltpu.VMEM((2, tm // n_f, D), _F32),
                        pltpu.SemaphoreType.DMA((1,)),
                        pltpu.SemaphoreType.DMA((2,))],
        compiler_params=pltpu.CompilerParams(
            dimension_semantics=("arbitrary", "arbitrary"),
            vmem_limit_bytes=VMEM_LIMIT_BYTES),
        name="mlp",
    )(x2d, w1, w2, g, b)


def kernel(x, mem, w_in, conv_w, conv_b, w_a, b_a, w_x, b_x, lru_lambda, w_pool, b_pool, pool_scale,
           w_out, ln1_g, ln1_b, w_q, w_k, w_v, w_o, ln2_g, ln2_b, w_ff1, w_ff2, ln3_g, ln3_b):
    B, S, D = x.shape
    row = lambda p: p.reshape(1, -1)
    for l in range(DEPTH):
        w_ax = jnp.concatenate([w_a[l], w_x[l]], axis=-1).astype(_BF16)
        k, v, w_in_b, w_out_b = _kv_proj(mem.reshape(B * N_MEM, D), w_k[l], w_v[l], (w_in[l], w_out[l]))
        x, wq_b, wo_b, w1_b, w2_b = _mixer(
            x, w_in_b, conv_w[l], row(conv_b[l]), w_ax,
            row(b_a[l]), row(b_x[l]), row(lru_lambda[l]),
            w_pool[l].astype(_BF16), row(b_pool[l]), row(pool_scale[l]),
            w_out_b, row(ln1_g[l]), row(ln1_b[l]),
            (w_q[l], w_o[l], w_ff1[l], w_ff2[l]))
        x = _xattn(x, k.reshape(B, N_MEM // 2, D), v.reshape(B, N_MEM // 2, D),
                   wq_b, wo_b, row(ln2_g[l]), row(ln2_b[l]))
        x = _mlp(x.reshape(B * S, D), w1_b, w2_b, row(ln3_g[l]), row(ln3_b[l])).reshape(B, S, D)
    return x
```

```python
import functools

import jax
import jax.numpy as jnp
from jax import lax
from jax.experimental import pallas as pl
from jax.experimental.pallas import tpu as pltpu

D_MODEL = 2048
POOL_WIDTH = 1024
LRU_WIDTH = 1024
POOL_WINDOWS = (2, 4, 8, 16)
POOL_GROUP = 256
LRU_HEADS = 8
LRU_HEAD_DIM = 128
CONV_WIDTH = 4
LRU_C = 8.0
N_MEM = 256
XATTN_HEADS = 4
XATTN_HEAD_DIM = 512
LN_EPS = 1e-5
DEPTH = 1
DEEPNORM_ALPHA = (2.0 * DEPTH) ** 0.25

POOL_HIST = 16
SUBLANES = 8
LANES = 128

MIXER_TILE = 256
XATTN_TILE = 512
ANCHOR_COLS = 256
MLP_TILE_M = 1024
MLP_ROW_HALVES = 2
MLP_TILE_F = 1024
KV_TILE_N = 256
VMEM_LIMIT_BYTES = 56 * 1024 * 1024

_F32 = jnp.float32
_BF16 = jnp.bfloat16


def _layer_norm(v, g, b):
    mu = jnp.mean(v, axis=-1, keepdims=True)
    c = v - mu
    var = jnp.mean(jnp.square(c), axis=-1, keepdims=True)
    return c * lax.rsqrt(var + LN_EPS) * g + b


def _dot(a, b):
    return jnp.dot(a, b, preferred_element_type=_F32)


def _pack_rows(v):
    return pltpu.bitcast(v.astype(_BF16), jnp.uint32)


def _unpack_rows(w):
    return pltpu.bitcast(w, _BF16)


def _ordering_zero(v):
    rows, cols = v.shape
    t = jnp.sum(v.reshape(rows // SUBLANES, SUBLANES, cols), axis=0)
    t = functools.reduce(lambda a, b: a + b, [t[:, c:c + LANES] for c in range(0, cols, LANES)])
    return jnp.minimum(jnp.abs(t), 0.0)


def _normalize_previous(resid_ref, slot, g_ref, b_ref, o_ref):
    T, D = resid_ref.shape[1:]
    chunk = T // (D // ANCHOR_COLS)
    zeros = []
    for r0 in range(0, T, chunk):
        out = _layer_norm(resid_ref[1 - slot, r0:r0 + chunk, :], g_ref[...], b_ref[...])
        o_ref[0, r0:r0 + chunk, :] = out
        zeros.append(_ordering_zero(out))
    return zeros


def _tile_zeros(zeros, rows, cols):
    width = cols // len(zeros)
    return jnp.concatenate([jnp.tile(z, (rows // SUBLANES, width // LANES)) for z in zeros], axis=1)


def _history_rows(cur_tail, prev_tail):
    sub = lax.broadcasted_iota(jnp.int32, (1,) + cur_tail.shape[1:], 1)
    return jnp.where(sub == 0, pltpu.roll(prev_tail, 1, 1), pltpu.roll(cur_tail, 1, 1))


def _tile_copies(hbm_ref, buf, sems, t, sl, n_seq, to_hbm):
    lc = buf.shape[1]
    b, s = t // n_seq, t % n_seq
    copies = []
    for j in range(SUBLANES):
        rows = hbm_ref.at[b, pl.ds(pl.multiple_of((s * SUBLANES + j) * lc, lc), lc), :]
        chunk = buf.at[sl, :, j, :]
        src, dst = (chunk, rows) if to_hbm else (rows, chunk)
        copies.append(pltpu.make_async_copy(src, dst, sems.at[sl]))
    return copies


def _mixer_kernel(x_hbm, w_in_ref, conv_w_ref, conv_b_ref, w_ax_ref, b_a_ref, b_x_ref, lam_ref,
                  w_pool_ref, b_pool_ref, pool_scale_ref, w_out_ref, g_ref, b_ref, *rest, n_seq):
    n_cast = (len(rest) - 9) // 2
    cast_in, o_hbm, cast_out = rest[:n_cast], rest[n_cast], rest[n_cast + 1:2 * n_cast + 1]
    xbuf, obuf, in_sem, out_sem, pool_hist, conv_hist, h_carry, resid_ref = rest[2 * n_cast + 1:]
    i = pl.program_id(0)
    n_tiles = pl.num_programs(0) - 1
    s = jnp.minimum(i, n_tiles - 1) % n_seq
    slot = i % 2

    def x_copies(t, sl):
        return _tile_copies(x_hbm, xbuf, in_sem, t, sl, n_seq, to_hbm=False)

    def out_copies(t, sl):
        return _tile_copies(o_hbm, obuf, out_sem, t, sl, n_seq, to_hbm=True)

    @pl.when(i == 0)
    def _():
        for c in x_copies(0, 0):
            c.start()
        resid_ref[1] = jnp.zeros(resid_ref.shape[1:], _F32)

    @pl.when(i < n_tiles)
    def _():
        for c in x_copies(i, slot):
            c.wait()

    @pl.when(i + 1 < n_tiles)
    def _():
        for c in x_copies(i + 1, 1 - slot):
            c.start()

    @pl.when(i >= 3)
    def _():
        for c in out_copies(i - 3, 1 - slot):
            c.wait()

    def finish_previous():
        T, D = resid_ref.shape[1:]
        chunk = T // (D // ANCHOR_COLS)
        zeros = []
        for r0 in range(0, T, chunk):
            out = _layer_norm(resid_ref[1 - slot, r0:r0 + chunk, :], g_ref[...], b_ref[...])
            obuf[1 - slot, r0 // SUBLANES:(r0 + chunk) // SUBLANES] = out.reshape(chunk // SUBLANES, SUBLANES, D)
            zeros.append(_ordering_zero(out))
        return zeros

    @pl.when(i == n_tiles)
    def _():
        finish_previous()

    @pl.when(i < n_tiles)
    def _():
        _mixer_tile(s, slot, finish_previous, xbuf, w_in_ref, conv_w_ref, conv_b_ref, w_ax_ref, b_a_ref, b_x_ref,
                    lam_ref, w_pool_ref, b_pool_ref, pool_scale_ref, w_out_ref, cast_in, cast_out,
                    pool_hist, conv_hist, h_carry, resid_ref)

    @pl.when(i >= 1)
    def _():
        for c in out_copies(i - 1, 1 - slot):
            c.start()

    @pl.when(i == n_tiles)
    def _():
        for c in out_copies(n_tiles - 2, slot) + out_copies(n_tiles - 1, 1 - slot):
            c.wait()


def _mixer_tile(s, slot, finish_previous, xbuf, w_in_ref, conv_w_ref, conv_b_ref, w_ax_ref, b_a_ref, b_x_ref,
                lam_ref, w_pool_ref, b_pool_ref, pool_scale_ref, w_out_ref, cast_in, cast_out,
                pool_hist, conv_hist, h_carry, resid_ref):
    lc, _, D = xbuf.shape[1:]
    T = lc * SUBLANES

    @pl.when(s == 0)
    def _():
        pool_hist[...] = jnp.zeros_like(pool_hist)
        conv_hist[...] = jnp.zeros_like(conv_hist)
        h_carry[...] = jnp.zeros_like(h_carry)

    zeros = finish_previous()
    for src, dst in zip(cast_in, cast_out):
        dst[...] = _pack_rows(src[...])

    x = xbuf[slot].reshape(T, D)
    xb = x.astype(_BF16)
    half = len(zeros) // 2
    up = _dot(xb, _unpack_rows(w_in_ref[:, :POOL_WIDTH])) + _tile_zeros(zeros[:half], T, POOL_WIDTH)
    up = up.reshape(lc, SUBLANES, POOL_WIDTH)
    u_lru = (_dot(xb, _unpack_rows(w_in_ref[:, POOL_WIDTH:POOL_WIDTH + LRU_WIDTH]))
             + _tile_zeros(zeros[half:], T, LRU_WIDTH)).reshape(lc, SUBLANES, LRU_WIDTH)

    kk = lax.broadcasted_iota(jnp.int32, (POOL_HIST, SUBLANES, LANES), 0)
    jj = lax.broadcasted_iota(jnp.int32, (POOL_HIST, SUBLANES, LANES), 1)
    t_head = s * T + jj * lc + kk

    ext = jnp.concatenate([_history_rows(up[lc - POOL_HIST:], pool_hist[...]), up], axis=0)
    pool_hist[...] = up[lc - POOL_HIST:]
    s2 = ext[1:] + ext[:-1]
    s2r = s2[:, :, POOL_GROUP:]
    s4 = s2r[2:] + s2r[:-2]
    s4r = s4[:, :, POOL_GROUP:]
    s8 = s4r[4:] + s4r[:-4]
    s8r = s8[:, :, POOL_GROUP:]
    s16 = s8r[8:] + s8r[:-8]
    sums = (s2[15:, :, :POOL_GROUP], s4[13:, :, :POOL_GROUP], s8[9:, :, :POOL_GROUP], s16[1:])
    y_pool = []
    for g, w in enumerate(POOL_WINDOWS):
        cs = slice(g * POOL_GROUP, (g + 1) * POOL_GROUP)
        inv_head = 1.0 / jnp.minimum(t_head + 1, w).astype(_F32)
        inv_head = jnp.concatenate([inv_head] * (POOL_GROUP // LANES), axis=-1)
        mean = jnp.concatenate([sums[g][:POOL_HIST] * inv_head, sums[g][POOL_HIST:] * (1.0 / w)], axis=0)
        mixed = mean - up[:, :, cs]
        yg = _dot(mixed.reshape(T, POOL_GROUP).astype(_BF16), w_pool_ref[g])
        y_pool.append(((yg + b_pool_ref[:, cs]) * pool_scale_ref[:, cs]).astype(_BF16))
    z_pool = jnp.concatenate(y_pool, axis=1)

    n_hist = CONV_WIDTH - 1
    sub = lax.broadcasted_iota(jnp.int32, (SUBLANES, LRU_HEAD_DIM), 0)
    first = t_head[0:1] == 0
    gate_cols = 2 * LRU_HEAD_DIM
    out_cols = D_MODEL // LRU_HEADS
    y_top, z_lru = [], []
    for h in range(LRU_HEADS):
        hs = slice(h * LRU_HEAD_DIM, (h + 1) * LRU_HEAD_DIM)
        ul = u_lru[:, :, hs]
        ext2 = jnp.concatenate([_history_rows(ul[lc - n_hist:], conv_hist[:, :, hs]), ul], axis=0)
        conv_hist[:, :, hs] = ul[lc - n_hist:]
        xc = ext2[0:lc] * conv_w_ref[0:1, hs]
        for k in range(1, CONV_WIDTH):
            xc = xc + ext2[k:k + lc] * conv_w_ref[k:k + 1, hs]
        xc = xc + conv_b_ref[:, hs]

        pre = _dot(xc.reshape(T, LRU_HEAD_DIM).astype(_BF16), w_ax_ref[h])
        pre = pre.reshape(lc, SUBLANES, 2 * LRU_HEAD_DIM)
        r = jax.nn.sigmoid(pre[:, :, :LRU_HEAD_DIM] + b_a_ref[:, hs])
        i = jax.nn.sigmoid(pre[:, :, LRU_HEAD_DIM:] + b_x_ref[:, hs])

        if h % 2 == 0:
            c0 = POOL_WIDTH + LRU_WIDTH + h * LRU_HEAD_DIM
            u_gate = _dot(xb, _unpack_rows(w_in_ref[:, c0:c0 + gate_cols])).reshape(lc, SUBLANES, gate_cols)
        else:
            c0 = (h // 2) * out_cols
            y_top.append(_dot(z_pool, _unpack_rows(w_out_ref[:POOL_WIDTH // 2, c0:c0 + out_cols])))

        log_a = (-LRU_C * r) * jax.nn.softplus(-lam_ref[:, hs])
        a = jnp.exp(log_a)
        u = -jnp.tanh(log_a) * (a * a + 1.0)
        mult = jnp.where(u == 0.0, 0.0, u * lax.rsqrt(u))
        mult = jnp.concatenate([jnp.where(first, 1.0, mult[0:1]), mult[1:]], axis=0)
        bv = mult * (i * xc)

        hl, ac = [bv[0]], [a[0]]
        for k in range(1, lc):
            hl.append(a[k] * hl[-1] + bv[k])
            ac.append(a[k] * ac[-1])
        c_a, c_b = ac[-1], hl[-1]
        for d in (1, 2, 4):
            a_s = jnp.where(sub < d, 1.0, pltpu.roll(c_a, d, 0))
            b_s = jnp.where(sub < d, 0.0, pltpu.roll(c_b, d, 0))
            c_b = c_a * b_s + c_b
            c_a = c_a * a_s
        h_prev = h_carry[:, hs]
        h_end = c_b + c_a * h_prev
        h_in = jnp.where(sub == 0, h_prev, pltpu.roll(h_end, 1, 0))
        h_carry[:, hs] = jnp.broadcast_to(h_end[SUBLANES - 1:SUBLANES, :], (SUBLANES, LRU_HEAD_DIM))
        hseq = jnp.stack(hl, axis=0) + jnp.stack(ac, axis=0) * h_in
        gate = jax.nn.gelu(u_gate[:, :, (h % 2) * LRU_HEAD_DIM:(h % 2 + 1) * LRU_HEAD_DIM])
        z_lru.append((hseq * gate).reshape(T, LRU_HEAD_DIM).astype(_BF16))

    for q in range(LRU_HEADS // 2, LRU_HEADS):
        y_top.append(_dot(z_pool, _unpack_rows(w_out_ref[:POOL_WIDTH // 2, q * out_cols:(q + 1) * out_cols])))
    y = (jnp.concatenate(y_top, axis=1)
         + _dot(jnp.concatenate(z_lru, axis=1), _unpack_rows(w_out_ref[POOL_WIDTH // 2:, :])))
    resid_ref[slot] = DEEPNORM_ALPHA * x + y


def _const_spec(shape, single_buffer=False):
    nd = len(shape)
    kwargs = {"pipeline_mode": pl.Buffered(1)} if single_buffer else {}
    return pl.BlockSpec(shape, lambda *_: (0,) * nd, **kwargs)


def _mixer(x, w_in, conv_w, conv_b, w_ax, b_a, b_x, lam, w_pool, b_pool, pool_scale, w_out, g, b,
           cast_weights):
    B, S, D = x.shape
    T = MIXER_TILE
    n_seq = S // T
    n_steps = B * n_seq
    lc = T // SUBLANES
    assert T % SUBLANES == 0 and lc >= POOL_HIST and lc % (2 * SUBLANES) == 0 and n_steps >= 3

    chunk_in, chunk_out = [], []
    for w in cast_weights:
        assert w.shape[0] % (2 * SUBLANES * n_steps) == 0
        rows_w = w.shape[0] // n_steps
        chunk_in.append(pl.BlockSpec((rows_w, w.shape[1]), lambda i: (jnp.minimum(i, n_steps - 1), 0)))
        chunk_out.append(pl.BlockSpec((rows_w // 2, w.shape[1]), lambda i: (jnp.minimum(i, n_steps - 1), 0)))
    return pl.pallas_call(
        functools.partial(_mixer_kernel, n_seq=n_seq),
        grid=(n_steps + 1,),
        in_specs=[
            pl.BlockSpec(memory_space=pl.ANY),
            _const_spec(w_in.shape, True),
            _const_spec(conv_w.shape), _const_spec(conv_b.shape),
            _const_spec(w_ax.shape, True),
            _const_spec(b_a.shape), _const_spec(b_x.shape), _const_spec(lam.shape),
            _const_spec(w_pool.shape, True),
            _const_spec(b_pool.shape), _const_spec(pool_scale.shape),
            _const_spec(w_out.shape, True),
            _const_spec(g.shape), _const_spec(b.shape),
        ] + chunk_in,
        out_specs=[pl.BlockSpec(memory_space=pl.ANY)] + chunk_out,
        out_shape=[jax.ShapeDtypeStruct(x.shape, _F32)]
        + [jax.ShapeDtypeStruct((w.shape[0] // 2, w.shape[1]), jnp.uint32) for w in cast_weights],
        scratch_shapes=[
            pltpu.VMEM((2, lc, SUBLANES, D), _F32),
            pltpu.VMEM((2, lc, SUBLANES, D), _F32),
            pltpu.SemaphoreType.DMA((2,)),
            pltpu.SemaphoreType.DMA((2,)),
            pltpu.VMEM((POOL_HIST, SUBLANES, POOL_WIDTH), _F32),
            pltpu.VMEM((CONV_WIDTH - 1, SUBLANES, LRU_WIDTH), _F32),
            pltpu.VMEM((SUBLANES, LRU_WIDTH), _F32),
            pltpu.VMEM((2, T, D), _F32),
        ],
        compiler_params=pltpu.CompilerParams(
            dimension_semantics=("arbitrary",), vmem_limit_bytes=VMEM_LIMIT_BYTES),
        name="mixer",
    )(x, w_in, conv_w, conv_b, w_ax, b_a, b_x, lam, w_pool, b_pool, pool_scale, w_out, g, b, *cast_weights)


def _kv_kernel(mem_ref, wk_ref, wv_ref, wa_ref, wx_ref, wpool_ref, *rest):
    n_cast = (len(rest) - 5) // 2
    cast_in, k_ref, v_ref = rest[:n_cast], rest[n_cast], rest[n_cast + 1]
    cast_out = rest[n_cast + 2:2 * n_cast + 2]
    wax_ref, wpool_b_ref, mb_ref = rest[2 * n_cast + 2:]
    for src, dst in zip(cast_in, cast_out):
        dst[...] = _pack_rows(src[...])

    @pl.when(pl.program_id(0) == 0)
    def _():
        mb_ref[...] = mem_ref[...].astype(_BF16)
        wax_ref[...] = jnp.concatenate([wa_ref[...], wx_ref[...]], axis=-1).astype(_BF16)
        wpool_b_ref[...] = wpool_ref[...].astype(_BF16)

    k_ref[...] = _pack_rows(_dot(mb_ref[...], wk_ref[...].astype(_BF16)))
    v_ref[...] = _pack_rows(_dot(mb_ref[...], wv_ref[...].astype(_BF16)))


def _kv_proj(mem2d, w_k, w_v, w_a, w_x, w_pool, cast_weights):
    M, D = mem2d.shape
    tn = KV_TILE_N
    n_steps = D // tn
    wspec = pl.BlockSpec((D, tn), lambda j: (0, j))
    ospec = pl.BlockSpec((M // 2, tn), lambda j: (0, j))
    chunk_in, chunk_out = [], []
    for w in cast_weights:
        assert w.shape[0] % (2 * SUBLANES * n_steps) == 0
        rows_w = w.shape[0] // n_steps
        chunk_in.append(pl.BlockSpec((rows_w, w.shape[1]), lambda j: (j, 0)))
        chunk_out.append(pl.BlockSpec((rows_w // 2, w.shape[1]), lambda j: (j, 0)))
    wax_shape = w_a.shape[:-1] + (w_a.shape[-1] + w_x.shape[-1],)
    return pl.pallas_call(
        _kv_kernel,
        grid=(n_steps,),
        in_specs=[_const_spec(mem2d.shape, True), wspec, wspec,
                  _const_spec(w_a.shape), _const_spec(w_x.shape), _const_spec(w_pool.shape)] + chunk_in,
        out_specs=[ospec, ospec] + chunk_out + [_const_spec(wax_shape), _const_spec(w_pool.shape)],
        out_shape=[jax.ShapeDtypeStruct((M // 2, D), jnp.uint32)] * 2
        + [jax.ShapeDtypeStruct((w.shape[0] // 2, w.shape[1]), jnp.uint32) for w in cast_weights]
        + [jax.ShapeDtypeStruct(wax_shape, _BF16), jax.ShapeDtypeStruct(w_pool.shape, _BF16)],
        scratch_shapes=[pltpu.VMEM((M, D), _BF16)],
        compiler_params=pltpu.CompilerParams(
            dimension_semantics=("arbitrary",), vmem_limit_bytes=VMEM_LIMIT_BYTES),
        name="kv_proj",
    )(mem2d, w_k, w_v, w_a, w_x, w_pool, *cast_weights)


def _xattn_kernel(x_ref, k_ref, v_ref, wq_ref, wo_ref, g_ref, b_ref, o_ref, resid_ref):
    i = pl.program_id(0)
    n_tiles = pl.num_programs(0) - 1
    slot = i % 2

    @pl.when(i == 0)
    def _():
        resid_ref[1] = jnp.zeros(resid_ref.shape[1:], _F32)

    def finish_previous():
        return _normalize_previous(resid_ref, slot, g_ref, b_ref, o_ref)

    @pl.when(i < n_tiles)
    def _():
        zeros = finish_previous()
        x = x_ref[0]
        xb = x.astype(_BF16)
        T = x.shape[0]
        scale = XATTN_HEAD_DIM ** -0.5
        heads = [slice(h * XATTN_HEAD_DIM, (h + 1) * XATTN_HEAD_DIM) for h in range(XATTN_HEADS)]

        blocks_per_head = XATTN_HEAD_DIM // ANCHOR_COLS
        anchors = [[] for _ in range(XATTN_HEADS * blocks_per_head)]
        for c, z in enumerate(zeros):
            anchors[min(c + 1, len(anchors) - 1)].append(z)

        def query(h):
            q = _dot(xb, _unpack_rows(wq_ref[:, heads[h]]))
            cols = []
            for blk in range(blocks_per_head):
                qb = q[:, blk * ANCHOR_COLS:(blk + 1) * ANCHOR_COLS]
                for z in anchors[h * blocks_per_head + blk]:
                    qb = qb + jnp.tile(z, (T // SUBLANES, ANCHOR_COLS // LANES))
                cols.append(qb)
            return jnp.concatenate(cols, axis=1).astype(_BF16)

        def probs(q, h):
            sc = lax.dot_general(q, _unpack_rows(k_ref[0, :, heads[h]]), (((1,), (1,)), ((), ())),
                                 preferred_element_type=_F32) * scale
            e = jnp.exp(sc - jnp.max(sc, axis=-1, keepdims=True))
            return (e * (1.0 / jnp.sum(e, axis=-1, keepdims=True))).astype(_BF16)

        def attend(p, h):
            return _dot(p, _unpack_rows(v_ref[0, :, heads[h]])).astype(_BF16)

        half_rows = wo_ref.shape[0] // 2
        q0 = query(0)
        q1 = query(1)
        p0 = probs(q0, 0)
        q2 = query(2)
        o0 = attend(p0, 0)
        p1 = probs(q1, 1)
        q3 = query(3)
        o1 = attend(p1, 1)
        p2 = probs(q2, 2)
        y = _dot(jnp.concatenate([o0, o1], axis=1), _unpack_rows(wo_ref[:half_rows, :]))
        o2 = attend(p2, 2)
        p3 = probs(q3, 3)
        o3 = attend(p3, 3)
        y = y + _dot(jnp.concatenate([o2, o3], axis=1), _unpack_rows(wo_ref[half_rows:, :]))
        resid_ref[slot] = DEEPNORM_ALPHA * x + y

    @pl.when(i == n_tiles)
    def _():
        finish_previous()


def _xattn(x, k, v, w_q, w_o, g, b):
    B, S, D = x.shape
    T = XATTN_TILE
    n_seq = S // T
    n_tiles = B * n_seq

    def cur(i):
        t = jnp.minimum(i, n_tiles - 1)
        return t // n_seq, t % n_seq

    def prev(i):
        t = jnp.maximum(i - 1, 0)
        return t // n_seq, t % n_seq

    kvspec = pl.BlockSpec((1, N_MEM // 2, D), lambda i: (cur(i)[0], 0, 0))
    return pl.pallas_call(
        _xattn_kernel,
        grid=(n_tiles + 1,),
        in_specs=[pl.BlockSpec((1, T, D), lambda i: (*cur(i), 0)), kvspec, kvspec,
                  _const_spec(w_q.shape, True), _const_spec(w_o.shape, True),
                  _const_spec(g.shape), _const_spec(b.shape)],
        out_specs=pl.BlockSpec((1, T, D), lambda i: (*prev(i), 0)),
        out_shape=jax.ShapeDtypeStruct(x.shape, _F32),
        scratch_shapes=[pltpu.VMEM((2, T, D), _F32)],
        compiler_params=pltpu.CompilerParams(
            dimension_semantics=("arbitrary",), vmem_limit_bytes=VMEM_LIMIT_BYTES),
        name="xattn",
    )(x, k, v, w_q, w_o, g, b)


def _mlp_kernel(x_hbm, w1_ref, w2_ref, g_ref, b_ref, o_hbm, xstage, xb_ref, acc_ref, oslice, x_sem, o_sem,
                *, n_f, halves):
    i, j = pl.program_id(0), pl.program_id(1)
    n_tiles = pl.num_programs(0) - 1
    slot = i % 2
    tm = xstage.shape[0]
    rows = tm // n_f
    q = i * n_f + j
    k = q % 2

    def x_copy(t):
        return pltpu.make_async_copy(x_hbm.at[pl.ds(pl.multiple_of(t * tm, tm), tm), :], xstage, x_sem.at[0])

    def out_copy(step, buf):
        r0 = pl.multiple_of((step // n_f - 1) * tm + (step % n_f) * rows, rows)
        return pltpu.make_async_copy(oslice.at[buf], o_hbm.at[pl.ds(r0, rows), :], o_sem.at[buf])

    @pl.when(q == 0)
    def _():
        x_copy(0).start()
        acc_ref[1] = jnp.zeros(acc_ref.shape[1:], _F32)

    @pl.when((i < n_tiles) & (j == 0))
    def _():
        x_copy(i).wait()
        xb_ref[...] = xstage[...].astype(_BF16)

    @pl.when(q >= n_f + 2)
    def _():
        out_copy(q - 2, k).wait()

    def finish_previous_rows():
        r0 = pl.multiple_of(j * rows, rows)
        oslice[k] = _layer_norm(acc_ref[1 - slot, pl.ds(r0, rows), :], g_ref[...], b_ref[...])

    def accumulate(first):
        finish_previous_rows()
        w1 = _unpack_rows(w1_ref[...])
        w2 = _unpack_rows(w2_ref[...])
        half = tm // halves
        hs = [jnp.square(jnp.maximum(_dot(xb_ref[r * half:(r + 1) * half, :], w1), 0.0)).astype(_BF16)
              for r in range(halves)]
        for r in range(halves):
            rs = slice(r * half, (r + 1) * half)
            prior = DEEPNORM_ALPHA * xstage[rs, :] if first else acc_ref[slot, rs, :]
            acc_ref[slot, rs, :] = prior + _dot(hs[r], w2)

    @pl.when((i < n_tiles) & (j == 0))
    def _():
        accumulate(first=True)

    @pl.when((i < n_tiles) & (j > 0))
    def _():
        accumulate(first=False)

    @pl.when((i + 1 < n_tiles) & (j == 0))
    def _():
        x_copy(i + 1).start()

    @pl.when(i == n_tiles)
    def _():
        finish_previous_rows()

    @pl.when(q >= n_f)
    def _():
        out_copy(q, k).start()

    @pl.when(q == (n_tiles + 1) * n_f - 1)
    def _():
        out_copy(q - 1, 1 - k).wait()
        out_copy(q, k).wait()


def _mlp(x2d, w1, w2, g, b):
    M, D = x2d.shape
    F = w1.shape[1]
    tm, tf = MLP_TILE_M, MLP_TILE_F
    n_tiles, n_f = M // tm, F // tf
    assert tm % (n_f * SUBLANES) == 0 and tm % MLP_ROW_HALVES == 0 and n_f >= 2

    def wblk(i, j):
        return jnp.where(i == n_tiles, n_f - 1, j)

    return pl.pallas_call(
        functools.partial(_mlp_kernel, n_f=n_f, halves=MLP_ROW_HALVES),
        grid=(n_tiles + 1, n_f),
        in_specs=[pl.BlockSpec(memory_space=pl.ANY),
                  pl.BlockSpec((D // 2, tf), lambda i, j: (0, wblk(i, j))),
                  pl.BlockSpec((tf // 2, D), lambda i, j: (wblk(i, j), 0)),
                  _const_spec(g.shape), _const_spec(b.shape)],
        out_specs=pl.BlockSpec(memory_space=pl.ANY),
        out_shape=jax.ShapeDtypeStruct((M, D), _F32),
        scratch_shapes=[pltpu.VMEM((tm, D), _F32),
                        pltpu.VMEM((tm, D), _BF16),
                        pltpu.VMEM((2, tm, D), _F32),
                        pltpu.VMEM((2, tm // n_f, D), _F32),
                        pltpu.SemaphoreType.DMA((1,)),
                        pltpu.SemaphoreType.DMA((2,))],
        compiler_params=pltpu.CompilerParams(
            dimension_semantics=("arbitrary", "arbitrary"),
            vmem_limit_bytes=VMEM_LIMIT_BYTES),
        name="mlp",
    )(x2d, w1, w2, g, b)


def kernel(x, mem, w_in, conv_w, conv_b, w_a, b_a, w_x, b_x, lru_lambda, w_pool, b_pool, pool_scale,
           w_out, ln1_g, ln1_b, w_q, w_k, w_v, w_o, ln2_g, ln2_b, w_ff1, w_ff2, ln3_g, ln3_b):
    B, S, D = x.shape
    row = lambda p: p.reshape(1, -1)
    for l in range(DEPTH):
        k, v, w_in_b, w_out_b, w_ax, w_pool_b = _kv_proj(
            mem.reshape(B * N_MEM, D), w_k[l], w_v[l], w_a[l], w_x[l], w_pool[l], (w_in[l], w_out[l]))
        x, wq_b, wo_b, w1_b, w2_b = _mixer(
            x, w_in_b, conv_w[l], row(conv_b[l]), w_ax,
            row(b_a[l]), row(b_x[l]), row(lru_lambda[l]),
            w_pool_b, row(b_pool[l]), row(pool_scale[l]),
            w_out_b, row(ln1_g[l]), row(ln1_b[l]),
            (w_q[l], w_o[l], w_ff1[l], w_ff2[l]))
        x = _xattn(x, k.reshape(B, N_MEM // 2, D), v.reshape(B, N_MEM // 2, D),
                   wq_b, wo_b, row(ln2_g[l]), row(ln2_b[l]))
        x = _mlp(x.reshape(B * S, D), w1_b, w2_b, row(ln3_g[l]), row(ln3_b[l])).reshape(B, S, D)
    return x
```

```python
import functools

import jax
import jax.numpy as jnp
from jax import lax
from jax.experimental import pallas as pl
from jax.experimental.pallas import tpu as pltpu

D_MODEL = 2048
POOL_WIDTH = 1024
LRU_WIDTH = 1024
POOL_WINDOWS = (2, 4, 8, 16)
POOL_GROUP = 256
LRU_HEADS = 8
LRU_HEAD_DIM = 128
CONV_WIDTH = 4
LRU_C = 8.0
N_MEM = 256
XATTN_HEADS = 4
XATTN_HEAD_DIM = 512
LN_EPS = 1e-5
DEPTH = 1
DEEPNORM_ALPHA = (2.0 * DEPTH) ** 0.25

POOL_HIST = 16
SUBLANES = 8
LANES = 128

MIXER_TILE = 256
XATTN_TILE = 512
ANCHOR_COLS = 256
MLP_TILE_M = 1024
MLP_ROW_HALVES = 2
MLP_TILE_F = 1024
KV_TILE_N = 256
VMEM_LIMIT_BYTES = 56 * 1024 * 1024
XATTN_VMEM_LIMIT_BYTES = 60 * 1024 * 1024

_F32 = jnp.float32
_BF16 = jnp.bfloat16


def _layer_norm(v, g, b):
    mu = jnp.mean(v, axis=-1, keepdims=True)
    c = v - mu
    var = jnp.mean(jnp.square(c), axis=-1, keepdims=True)
    return c * lax.rsqrt(var + LN_EPS) * g + b


def _dot(a, b):
    return jnp.dot(a, b, preferred_element_type=_F32)


def _pack_rows(v):
    return pltpu.bitcast(v.astype(_BF16), jnp.uint32)


def _unpack_rows(w):
    return pltpu.bitcast(w, _BF16)


def _ordering_zero(v):
    rows, cols = v.shape
    t = jnp.sum(v.reshape(rows // SUBLANES, SUBLANES, cols), axis=0)
    t = functools.reduce(lambda a, b: a + b, [t[:, c:c + LANES] for c in range(0, cols, LANES)])
    return jnp.minimum(jnp.abs(t), 0.0)


def _normalize_previous(resid_ref, slot, g_ref, b_ref, o_ref):
    T, D = resid_ref.shape[1:]
    chunk = T // (D // ANCHOR_COLS)
    zeros = []
    for r0 in range(0, T, chunk):
        out = _layer_norm(resid_ref[1 - slot, r0:r0 + chunk, :], g_ref[...], b_ref[...])
        o_ref[0, r0:r0 + chunk, :] = out
        zeros.append(_ordering_zero(out))
    return zeros


def _tile_zeros(zeros, rows, cols):
    width = cols // len(zeros)
    return jnp.concatenate([jnp.tile(z, (rows // SUBLANES, width // LANES)) for z in zeros], axis=1)


def _history_rows(cur_tail, prev_tail):
    sub = lax.broadcasted_iota(jnp.int32, (1,) + cur_tail.shape[1:], 1)
    return jnp.where(sub == 0, pltpu.roll(prev_tail, 1, 1), pltpu.roll(cur_tail, 1, 1))


def _tile_copies(hbm_ref, buf, sems, t, sl, n_seq, to_hbm):
    lc = buf.shape[1]
    b, s = t // n_seq, t % n_seq
    copies = []
    for j in range(SUBLANES):
        rows = hbm_ref.at[b, pl.ds(pl.multiple_of((s * SUBLANES + j) * lc, lc), lc), :]
        chunk = buf.at[sl, :, j, :]
        src, dst = (chunk, rows) if to_hbm else (rows, chunk)
        copies.append(pltpu.make_async_copy(src, dst, sems.at[sl]))
    return copies


def _mixer_kernel(x_hbm, w_in_ref, conv_w_ref, conv_b_ref, w_ax_ref, b_a_ref, b_x_ref, lam_ref,
                  w_pool_ref, b_pool_ref, pool_scale_ref, w_out_ref, g_ref, b_ref, *rest, n_seq):
    n_cast = (len(rest) - 9) // 2
    cast_in, o_hbm, cast_out = rest[:n_cast], rest[n_cast], rest[n_cast + 1:2 * n_cast + 1]
    xbuf, obuf, in_sem, out_sem, pool_hist, conv_hist, h_carry, resid_ref = rest[2 * n_cast + 1:]
    i = pl.program_id(0)
    n_tiles = pl.num_programs(0) - 1
    s = jnp.minimum(i, n_tiles - 1) % n_seq
    slot = i % 2

    def x_copies(t, sl):
        return _tile_copies(x_hbm, xbuf, in_sem, t, sl, n_seq, to_hbm=False)

    def out_copies(t, sl):
        return _tile_copies(o_hbm, obuf, out_sem, t, sl, n_seq, to_hbm=True)

    @pl.when(i == 0)
    def _():
        for c in x_copies(0, 0):
            c.start()
        resid_ref[1] = jnp.zeros(resid_ref.shape[1:], _F32)

    @pl.when(i < n_tiles)
    def _():
        for c in x_copies(i, slot):
            c.wait()

    @pl.when(i + 1 < n_tiles)
    def _():
        for c in x_copies(i + 1, 1 - slot):
            c.start()

    @pl.when(i >= 3)
    def _():
        for c in out_copies(i - 3, 1 - slot):
            c.wait()

    def finish_previous():
        T, D = resid_ref.shape[1:]
        chunk = T // (D // ANCHOR_COLS)
        zeros = []
        for r0 in range(0, T, chunk):
            out = _layer_norm(resid_ref[1 - slot, r0:r0 + chunk, :], g_ref[...], b_ref[...])
            obuf[1 - slot, r0 // SUBLANES:(r0 + chunk) // SUBLANES] = out.reshape(chunk // SUBLANES, SUBLANES, D)
            zeros.append(_ordering_zero(out))
        return zeros

    @pl.when(i == n_tiles)
    def _():
        finish_previous()

    @pl.when(i < n_tiles)
    def _():
        _mixer_tile(s, slot, finish_previous, xbuf, w_in_ref, conv_w_ref, conv_b_ref, w_ax_ref, b_a_ref, b_x_ref,
                    lam_ref, w_pool_ref, b_pool_ref, pool_scale_ref, w_out_ref, cast_in, cast_out,
                    pool_hist, conv_hist, h_carry, resid_ref)

    @pl.when(i >= 1)
    def _():
        for c in out_copies(i - 1, 1 - slot):
            c.start()

    @pl.when(i == n_tiles)
    def _():
        for c in out_copies(n_tiles - 2, slot) + out_copies(n_tiles - 1, 1 - slot):
            c.wait()


def _mixer_tile(s, slot, finish_previous, xbuf, w_in_ref, conv_w_ref, conv_b_ref, w_ax_ref, b_a_ref, b_x_ref,
                lam_ref, w_pool_ref, b_pool_ref, pool_scale_ref, w_out_ref, cast_in, cast_out,
                pool_hist, conv_hist, h_carry, resid_ref):
    lc, _, D = xbuf.shape[1:]
    T = lc * SUBLANES

    @pl.when(s == 0)
    def _():
        pool_hist[...] = jnp.zeros_like(pool_hist)
        conv_hist[...] = jnp.zeros_like(conv_hist)
        h_carry[...] = jnp.zeros_like(h_carry)

    zeros = finish_previous()
    for src, dst in zip(cast_in, cast_out):
        dst[...] = _pack_rows(src[...])

    x = xbuf[slot].reshape(T, D)
    xb = x.astype(_BF16)
    half = len(zeros) // 2
    up = _dot(xb, _unpack_rows(w_in_ref[:, :POOL_WIDTH])) + _tile_zeros(zeros[:half], T, POOL_WIDTH)
    up = up.reshape(lc, SUBLANES, POOL_WIDTH)
    u_lru = (_dot(xb, _unpack_rows(w_in_ref[:, POOL_WIDTH:POOL_WIDTH + LRU_WIDTH]))
             + _tile_zeros(zeros[half:], T, LRU_WIDTH)).reshape(lc, SUBLANES, LRU_WIDTH)

    kk = lax.broadcasted_iota(jnp.int32, (POOL_HIST, SUBLANES, LANES), 0)
    jj = lax.broadcasted_iota(jnp.int32, (POOL_HIST, SUBLANES, LANES), 1)
    t_head = s * T + jj * lc + kk

    ext = jnp.concatenate([_history_rows(up[lc - POOL_HIST:], pool_hist[...]), up], axis=0)
    pool_hist[...] = up[lc - POOL_HIST:]
    s2 = ext[1:] + ext[:-1]
    s2r = s2[:, :, POOL_GROUP:]
    s4 = s2r[2:] + s2r[:-2]
    s4r = s4[:, :, POOL_GROUP:]
    s8 = s4r[4:] + s4r[:-4]
    s8r = s8[:, :, POOL_GROUP:]
    s16 = s8r[8:] + s8r[:-8]
    sums = (s2[15:, :, :POOL_GROUP], s4[13:, :, :POOL_GROUP], s8[9:, :, :POOL_GROUP], s16[1:])
    y_pool = []
    for g, w in enumerate(POOL_WINDOWS):
        cs = slice(g * POOL_GROUP, (g + 1) * POOL_GROUP)
        inv_head = 1.0 / jnp.minimum(t_head + 1, w).astype(_F32)
        inv_head = jnp.concatenate([inv_head] * (POOL_GROUP // LANES), axis=-1)
        mean = jnp.concatenate([sums[g][:POOL_HIST] * inv_head, sums[g][POOL_HIST:] * (1.0 / w)], axis=0)
        mixed = mean - up[:, :, cs]
        yg = _dot(mixed.reshape(T, POOL_GROUP).astype(_BF16), w_pool_ref[g])
        y_pool.append(((yg + b_pool_ref[:, cs]) * pool_scale_ref[:, cs]).astype(_BF16))
    z_pool = jnp.concatenate(y_pool, axis=1)

    n_hist = CONV_WIDTH - 1
    sub = lax.broadcasted_iota(jnp.int32, (SUBLANES, LRU_HEAD_DIM), 0)
    first = t_head[0:1] == 0
    gate_cols = 2 * LRU_HEAD_DIM
    out_cols = D_MODEL // LRU_HEADS
    y_top, z_lru = [], []
    for h in range(LRU_HEADS):
        hs = slice(h * LRU_HEAD_DIM, (h + 1) * LRU_HEAD_DIM)
        ul = u_lru[:, :, hs]
        ext2 = jnp.concatenate([_history_rows(ul[lc - n_hist:], conv_hist[:, :, hs]), ul], axis=0)
        conv_hist[:, :, hs] = ul[lc - n_hist:]
        xc = ext2[0:lc] * conv_w_ref[0:1, hs]
        for k in range(1, CONV_WIDTH):
            xc = xc + ext2[k:k + lc] * conv_w_ref[k:k + 1, hs]
        xc = xc + conv_b_ref[:, hs]

        pre = _dot(xc.reshape(T, LRU_HEAD_DIM).astype(_BF16), w_ax_ref[h])
        pre = pre.reshape(lc, SUBLANES, 2 * LRU_HEAD_DIM)
        r = jax.nn.sigmoid(pre[:, :, :LRU_HEAD_DIM] + b_a_ref[:, hs])
        i = jax.nn.sigmoid(pre[:, :, LRU_HEAD_DIM:] + b_x_ref[:, hs])

        if h % 2 == 0:
            c0 = POOL_WIDTH + LRU_WIDTH + h * LRU_HEAD_DIM
            u_gate = _dot(xb, _unpack_rows(w_in_ref[:, c0:c0 + gate_cols])).reshape(lc, SUBLANES, gate_cols)
        else:
            c0 = (h // 2) * out_cols
            y_top.append(_dot(z_pool, _unpack_rows(w_out_ref[:POOL_WIDTH // 2, c0:c0 + out_cols])))

        log_a = (-LRU_C * r) * jax.nn.softplus(-lam_ref[:, hs])
        a = jnp.exp(log_a)
        u = -jnp.tanh(log_a) * (a * a + 1.0)
        mult = jnp.where(u == 0.0, 0.0, u * lax.rsqrt(u))
        mult = jnp.concatenate([jnp.where(first, 1.0, mult[0:1]), mult[1:]], axis=0)
        bv = mult * (i * xc)

        hl, ac = [bv[0]], [a[0]]
        for k in range(1, lc):
            hl.append(a[k] * hl[-1] + bv[k])
            ac.append(a[k] * ac[-1])
        c_a, c_b = ac[-1], hl[-1]
        for d in (1, 2, 4):
            a_s = jnp.where(sub < d, 1.0, pltpu.roll(c_a, d, 0))
            b_s = jnp.where(sub < d, 0.0, pltpu.roll(c_b, d, 0))
            c_b = c_a * b_s + c_b
            c_a = c_a * a_s
        h_prev = h_carry[:, hs]
        h_end = c_b + c_a * h_prev
        h_in = jnp.where(sub == 0, h_prev, pltpu.roll(h_end, 1, 0))
        h_carry[:, hs] = jnp.broadcast_to(h_end[SUBLANES - 1:SUBLANES, :], (SUBLANES, LRU_HEAD_DIM))
        hseq = jnp.stack(hl, axis=0) + jnp.stack(ac, axis=0) * h_in
        gate = jax.nn.gelu(u_gate[:, :, (h % 2) * LRU_HEAD_DIM:(h % 2 + 1) * LRU_HEAD_DIM])
        z_lru.append((hseq * gate).reshape(T, LRU_HEAD_DIM).astype(_BF16))

    for q in range(LRU_HEADS // 2, LRU_HEADS):
        y_top.append(_dot(z_pool, _unpack_rows(w_out_ref[:POOL_WIDTH // 2, q * out_cols:(q + 1) * out_cols])))
    y = (jnp.concatenate(y_top, axis=1)
         + _dot(jnp.concatenate(z_lru, axis=1), _unpack_rows(w_out_ref[POOL_WIDTH // 2:, :])))
    resid_ref[slot] = DEEPNORM_ALPHA * x + y


def _const_spec(shape, single_buffer=False):
    nd = len(shape)
    kwargs = {"pipeline_mode": pl.Buffered(1)} if single_buffer else {}
    return pl.BlockSpec(shape, lambda *_: (0,) * nd, **kwargs)


def _mixer(x, w_in, conv_w, conv_b, w_ax, b_a, b_x, lam, w_pool, b_pool, pool_scale, w_out, g, b,
           cast_weights):
    B, S, D = x.shape
    T = MIXER_TILE
    n_seq = S // T
    n_steps = B * n_seq
    lc = T // SUBLANES
    assert T % SUBLANES == 0 and lc >= POOL_HIST and lc % (2 * SUBLANES) == 0 and n_steps >= 3

    chunk_in, chunk_out = [], []
    for w in cast_weights:
        assert w.shape[0] % (2 * SUBLANES * n_steps) == 0
        rows_w = w.shape[0] // n_steps
        chunk_in.append(pl.BlockSpec((rows_w, w.shape[1]), lambda i: (jnp.minimum(i, n_steps - 1), 0)))
        chunk_out.append(pl.BlockSpec((rows_w // 2, w.shape[1]), lambda i: (jnp.minimum(i, n_steps - 1), 0)))
    return pl.pallas_call(
        functools.partial(_mixer_kernel, n_seq=n_seq),
        grid=(n_steps + 1,),
        in_specs=[
            pl.BlockSpec(memory_space=pl.ANY),
            _const_spec(w_in.shape, True),
            _const_spec(conv_w.shape), _const_spec(conv_b.shape),
            _const_spec(w_ax.shape, True),
            _const_spec(b_a.shape), _const_spec(b_x.shape), _const_spec(lam.shape),
            _const_spec(w_pool.shape, True),
            _const_spec(b_pool.shape), _const_spec(pool_scale.shape),
            _const_spec(w_out.shape, True),
            _const_spec(g.shape), _const_spec(b.shape),
        ] + chunk_in,
        out_specs=[pl.BlockSpec(memory_space=pl.ANY)] + chunk_out,
        out_shape=[jax.ShapeDtypeStruct(x.shape, _F32)]
        + [jax.ShapeDtypeStruct((w.shape[0] // 2, w.shape[1]), jnp.uint32) for w in cast_weights],
        scratch_shapes=[
            pltpu.VMEM((2, lc, SUBLANES, D), _F32),
            pltpu.VMEM((2, lc, SUBLANES, D), _F32),
            pltpu.SemaphoreType.DMA((2,)),
            pltpu.SemaphoreType.DMA((2,)),
            pltpu.VMEM((POOL_HIST, SUBLANES, POOL_WIDTH), _F32),
            pltpu.VMEM((CONV_WIDTH - 1, SUBLANES, LRU_WIDTH), _F32),
            pltpu.VMEM((SUBLANES, LRU_WIDTH), _F32),
            pltpu.VMEM((2, T, D), _F32),
        ],
        compiler_params=pltpu.CompilerParams(
            dimension_semantics=("arbitrary",), vmem_limit_bytes=VMEM_LIMIT_BYTES),
        name="mixer",
    )(x, w_in, conv_w, conv_b, w_ax, b_a, b_x, lam, w_pool, b_pool, pool_scale, w_out, g, b, *cast_weights)


def _kv_kernel(mem_ref, wk_ref, wv_ref, wa_ref, wx_ref, wpool_ref, *rest):
    n_cast = (len(rest) - 5) // 2
    cast_in, k_ref, v_ref = rest[:n_cast], rest[n_cast], rest[n_cast + 1]
    cast_out = rest[n_cast + 2:2 * n_cast + 2]
    wax_ref, wpool_b_ref, mb_ref = rest[2 * n_cast + 2:]
    for src, dst in zip(cast_in, cast_out):
        dst[...] = _pack_rows(src[...])

    @pl.when(pl.program_id(0) == 0)
    def _():
        mb_ref[...] = mem_ref[...].astype(_BF16)
        wax_ref[...] = jnp.concatenate([wa_ref[...], wx_ref[...]], axis=-1).astype(_BF16)
        wpool_b_ref[...] = wpool_ref[...].astype(_BF16)

    k_ref[...] = _pack_rows(_dot(mb_ref[...], wk_ref[...].astype(_BF16)))
    v_ref[...] = _pack_rows(_dot(mb_ref[...], wv_ref[...].astype(_BF16)))


def _kv_proj(mem2d, w_k, w_v, w_a, w_x, w_pool, cast_weights):
    M, D = mem2d.shape
    tn = KV_TILE_N
    n_steps = D // tn
    wspec = pl.BlockSpec((D, tn), lambda j: (0, j))
    ospec = pl.BlockSpec((M // 2, tn), lambda j: (0, j))
    chunk_in, chunk_out = [], []
    for w in cast_weights:
        assert w.shape[0] % (2 * SUBLANES * n_steps) == 0
        rows_w = w.shape[0] // n_steps
        chunk_in.append(pl.BlockSpec((rows_w, w.shape[1]), lambda j: (j, 0)))
        chunk_out.append(pl.BlockSpec((rows_w // 2, w.shape[1]), lambda j: (j, 0)))
    wax_shape = w_a.shape[:-1] + (w_a.shape[-1] + w_x.shape[-1],)
    return pl.pallas_call(
        _kv_kernel,
        grid=(n_steps,),
        in_specs=[_const_spec(mem2d.shape, True), wspec, wspec,
                  _const_spec(w_a.shape), _const_spec(w_x.shape), _const_spec(w_pool.shape)] + chunk_in,
        out_specs=[ospec, ospec] + chunk_out + [_const_spec(wax_shape), _const_spec(w_pool.shape)],
        out_shape=[jax.ShapeDtypeStruct((M // 2, D), jnp.uint32)] * 2
        + [jax.ShapeDtypeStruct((w.shape[0] // 2, w.shape[1]), jnp.uint32) for w in cast_weights]
        + [jax.ShapeDtypeStruct(wax_shape, _BF16), jax.ShapeDtypeStruct(w_pool.shape, _BF16)],
        scratch_shapes=[pltpu.VMEM((M, D), _BF16)],
        compiler_params=pltpu.CompilerParams(
            dimension_semantics=("arbitrary",), vmem_limit_bytes=VMEM_LIMIT_BYTES),
        name="kv_proj",
    )(mem2d, w_k, w_v, w_a, w_x, w_pool, *cast_weights)


def _xattn_kernel(x_ref, xn_ref, k_ref, v_ref, wq_ref, wo_ref, g_ref, b_ref, o_ref, resid_ref, q_ref):
    i = pl.program_id(0)
    n_tiles = pl.num_programs(0) - 1
    slot = i % 2
    T = x_ref.shape[1]
    scale = XATTN_HEAD_DIM ** -0.5
    heads = [slice(h * XATTN_HEAD_DIM, (h + 1) * XATTN_HEAD_DIM) for h in range(XATTN_HEADS)]

    def query(xb, h):
        return _dot(xb, _unpack_rows(wq_ref[:, heads[h]])).astype(_BF16)

    @pl.when(i == 0)
    def _():
        resid_ref[1] = jnp.zeros(resid_ref.shape[1:], _F32)
        xb0 = x_ref[0].astype(_BF16)
        for h in range(XATTN_HEADS):
            q_ref[:, heads[h]] = query(xb0, h)

    def finish_previous():
        return _normalize_previous(resid_ref, slot, g_ref, b_ref, o_ref)

    @pl.when(i < n_tiles)
    def _():
        zeros = finish_previous()
        per_head = len(zeros) // XATTN_HEADS
        x = x_ref[0]
        xnb = xn_ref[0].astype(_BF16)

        def probs(h):
            sc = lax.dot_general(q_ref[:, heads[h]], _unpack_rows(k_ref[0, :, heads[h]]),
                                 (((1,), (1,)), ((), ())), preferred_element_type=_F32) * scale
            sc = sc + _tile_zeros(zeros[h * per_head:(h + 1) * per_head], T, N_MEM)
            e = jnp.exp(sc - jnp.max(sc, axis=-1, keepdims=True))
            return (e * (1.0 / jnp.sum(e, axis=-1, keepdims=True))).astype(_BF16)

        def attend(p, h):
            return _dot(p, _unpack_rows(v_ref[0, :, heads[h]])).astype(_BF16)

        o_parts = []
        for h in range(XATTN_HEADS):
            p = probs(h)
            q_ref[:, heads[h]] = query(xnb, h)
            o_parts.append(attend(p, h))
        y = _dot(jnp.concatenate(o_parts, axis=1), _unpack_rows(wo_ref[...]))
        resid_ref[slot] = DEEPNORM_ALPHA * x + y

    @pl.when(i == n_tiles)
    def _():
        finish_previous()


def _xattn(x, k, v, w_q, w_o, g, b):
    B, S, D = x.shape
    T = XATTN_TILE
    n_seq = S // T
    n_tiles = B * n_seq

    def tile(t):
        t = jnp.clip(t, 0, n_tiles - 1)
        return t // n_seq, t % n_seq

    kvspec = pl.BlockSpec((1, N_MEM // 2, D), lambda i: (tile(i)[0], 0, 0))
    return pl.pallas_call(
        _xattn_kernel,
        grid=(n_tiles + 1,),
        in_specs=[pl.BlockSpec((1, T, D), lambda i: (*tile(i), 0)),
                  pl.BlockSpec((1, T, D), lambda i: (*tile(i + 1), 0)),
                  kvspec, kvspec,
                  _const_spec(w_q.shape, True), _const_spec(w_o.shape, True),
                  _const_spec(g.shape), _const_spec(b.shape)],
        out_specs=pl.BlockSpec((1, T, D), lambda i: (*tile(i - 1), 0)),
        out_shape=jax.ShapeDtypeStruct(x.shape, _F32),
        scratch_shapes=[pltpu.VMEM((2, T, D), _F32), pltpu.VMEM((T, D), _BF16)],
        compiler_params=pltpu.CompilerParams(
            dimension_semantics=("arbitrary",), vmem_limit_bytes=XATTN_VMEM_LIMIT_BYTES),
        name="xattn",
    )(x, x, k, v, w_q, w_o, g, b)


def _mlp_kernel(x_hbm, w1_ref, w2_ref, g_ref, b_ref, o_hbm, xstage, xb_ref, acc_ref, oslice, x_sem, o_sem,
                *, n_f, halves):
    i, j = pl.program_id(0), pl.program_id(1)
    n_tiles = pl.num_programs(0) - 1
    slot = i % 2
    tm = xstage.shape[0]
    rows = tm // n_f
    q = i * n_f + j
    k = q % 2

    def x_copy(t):
        return pltpu.make_async_copy(x_hbm.at[pl.ds(pl.multiple_of(t * tm, tm), tm), :], xstage, x_sem.at[0])

    def out_copy(step, buf):
        r0 = pl.multiple_of((step // n_f - 1) * tm + (step % n_f) * rows, rows)
        return pltpu.make_async_copy(oslice.at[buf], o_hbm.at[pl.ds(r0, rows), :], o_sem.at[buf])

    @pl.when(q == 0)
    def _():
        x_copy(0).start()
        acc_ref[1] = jnp.zeros(acc_ref.shape[1:], _F32)

    @pl.when((i < n_tiles) & (j == 0))
    def _():
        x_copy(i).wait()
        xb_ref[...] = xstage[...].astype(_BF16)

    @pl.when(q >= n_f + 2)
    def _():
        out_copy(q - 2, k).wait()

    def finish_previous_rows():
        r0 = pl.multiple_of(j * rows, rows)
        oslice[k] = _layer_norm(acc_ref[1 - slot, pl.ds(r0, rows), :], g_ref[...], b_ref[...])

    def accumulate(first):
        finish_previous_rows()
        w1 = _unpack_rows(w1_ref[...])
        w2 = _unpack_rows(w2_ref[...])
        half = tm // halves
        hs = [jnp.square(jnp.maximum(_dot(xb_ref[r * half:(r + 1) * half, :], w1), 0.0)).astype(_BF16)
              for r in range(halves)]
        for r in range(halves):
            rs = slice(r * half, (r + 1) * half)
            prior = DEEPNORM_ALPHA * xstage[rs, :] if first else acc_ref[slot, rs, :]
            acc_ref[slot, rs, :] = prior + _dot(hs[r], w2)

    @pl.when((i < n_tiles) & (j == 0))
    def _():
        accumulate(first=True)

    @pl.when((i < n_tiles) & (j > 0))
    def _():
        accumulate(first=False)

    @pl.when((i + 1 < n_tiles) & (j == 0))
    def _():
        x_copy(i + 1).start()

    @pl.when(i == n_tiles)
    def _():
        finish_previous_rows()

    @pl.when(q >= n_f)
    def _():
        out_copy(q, k).start()

    @pl.when(q == (n_tiles + 1) * n_f - 1)
    def _():
        out_copy(q - 1, 1 - k).wait()
        out_copy(q, k).wait()


def _mlp(x2d, w1, w2, g, b):
    M, D = x2d.shape
    F = w1.shape[1]
    tm, tf = MLP_TILE_M, MLP_TILE_F
    n_tiles, n_f = M // tm, F // tf
    assert tm % (n_f * SUBLANES) == 0 and tm % MLP_ROW_HALVES == 0 and n_f >= 2

    def wblk(i, j):
        return jnp.where(i == n_tiles, n_f - 1, j)

    return pl.pallas_call(
        functools.partial(_mlp_kernel, n_f=n_f, halves=MLP_ROW_HALVES),
        grid=(n_tiles + 1, n_f),
        in_specs=[pl.BlockSpec(memory_space=pl.ANY),
                  pl.BlockSpec((D // 2, tf), lambda i, j: (0, wblk(i, j))),
                  pl.BlockSpec((tf // 2, D), lambda i, j: (wblk(i, j), 0)),
                  _const_spec(g.shape), _const_spec(b.shape)],
        out_specs=pl.BlockSpec(memory_space=pl.ANY),
        out_shape=jax.ShapeDtypeStruct((M, D), _F32),
        scratch_shapes=[pltpu.VMEM((tm, D), _F32),
                        pltpu.VMEM((tm, D), _BF16),
                        pltpu.VMEM((2, tm, D), _F32),
                        pltpu.VMEM((2, tm // n_f, D), _F32),
                        pltpu.SemaphoreType.DMA((1,)),
                        pltpu.SemaphoreType.DMA((2,))],
        compiler_params=pltpu.CompilerParams(
            dimension_semantics=("arbitrary", "arbitrary"),
            vmem_limit_bytes=VMEM_LIMIT_BYTES),
        name="mlp",
    )(x2d, w1, w2, g, b)


def kernel(x, mem, w_in, conv_w, conv_b, w_a, b_a, w_x, b_x, lru_lambda, w_pool, b_pool, pool_scale,
           w_out, ln1_g, ln1_b, w_q, w_k, w_v, w_o, ln2_g, ln2_b, w_ff1, w_ff2, ln3_g, ln3_b):
    B, S, D = x.shape
    row = lambda p: p.reshape(1, -1)
    for l in range(DEPTH):
        k, v, w_in_b, w_out_b, w_ax, w_pool_b = _kv_proj(
            mem.reshape(B * N_MEM, D), w_k[l], w_v[l], w_a[l], w_x[l], w_pool[l], (w_in[l], w_out[l]))
        x, wq_b, wo_b, w1_b, w2_b = _mixer(
            x, w_in_b, conv_w[l], row(conv_b[l]), w_ax,
            row(b_a[l]), row(b_x[l]), row(lru_lambda[l]),
            w_pool_b, row(b_pool[l]), row(pool_scale[l]),
            w_out_b, row(ln1_g[l]), row(ln1_b[l]),
            (w_q[l], w_o[l], w_ff1[l], w_ff2[l]))
        x = _xattn(x, k.reshape(B, N_MEM // 2, D), v.reshape(B, N_MEM // 2, D),
                   wq_b, wo_b, row(ln2_g[l]), row(ln2_b[l]))
        x = _mlp(x.reshape(B * S, D), w1_b, w2_b, row(ln3_g[l]), row(ln3_b[l])).reshape(B, S, D)
    return x
```

```python
import functools

import jax
import jax.numpy as jnp
from jax import lax
from jax.experimental import pallas as pl
from jax.experimental.pallas import tpu as pltpu

D_MODEL = 2048
POOL_WIDTH = 1024
LRU_WIDTH = 1024
POOL_WINDOWS = (2, 4, 8, 16)
POOL_GROUP = 256
LRU_HEADS = 8
LRU_HEAD_DIM = 128
CONV_WIDTH = 4
LRU_C = 8.0
N_MEM = 256
XATTN_HEADS = 4
XATTN_HEAD_DIM = 512
LN_EPS = 1e-5
DEPTH = 1
DEEPNORM_ALPHA = (2.0 * DEPTH) ** 0.25

POOL_HIST = 16
SUBLANES = 8
LANES = 128

MIXER_TILE = 256
XATTN_TILE = 512
LN_CHUNKS = 8
MLP_TILE_M = 1024
MLP_ROW_HALVES = 2
MLP_TILE_F = 1024
KV_TILE_N = 256
VMEM_LIMIT_BYTES = 56 * 1024 * 1024
XATTN_VMEM_LIMIT_BYTES = 60 * 1024 * 1024

_F32 = jnp.float32
_BF16 = jnp.bfloat16


def _layer_norm(v, g, b):
    mu = jnp.mean(v, axis=-1, keepdims=True)
    c = v - mu
    var = jnp.mean(jnp.square(c), axis=-1, keepdims=True)
    return c * lax.rsqrt(var + LN_EPS) * g + b


def _dot(a, b):
    return jnp.dot(a, b, preferred_element_type=_F32)


def _pack_rows(v):
    return pltpu.bitcast(v.astype(_BF16), jnp.uint32)


def _unpack_rows(w):
    return pltpu.bitcast(w, _BF16)


def _ordering_zero(v):
    rows, cols = v.shape
    t = jnp.sum(v.reshape(rows // SUBLANES, SUBLANES, cols), axis=0)
    t = functools.reduce(lambda a, b: a + b, [t[:, c:c + LANES] for c in range(0, cols, LANES)])
    return jnp.minimum(jnp.abs(t), 0.0)


def _normalize_previous(resid_ref, slot, g_ref, b_ref, o_ref):
    T = resid_ref.shape[1]
    chunk = T // LN_CHUNKS
    zeros = []
    for r0 in range(0, T, chunk):
        out = _layer_norm(resid_ref[1 - slot, r0:r0 + chunk, :], g_ref[...], b_ref[...])
        o_ref[0, r0:r0 + chunk, :] = out
        zeros.append(_ordering_zero(out))
    return zeros


def _tile_zeros(zeros, rows, cols):
    width = cols // len(zeros)
    return jnp.concatenate([jnp.tile(z, (rows // SUBLANES, width // LANES)) for z in zeros], axis=1)


def _history_rows(cur_tail, prev_tail):
    sub = lax.broadcasted_iota(jnp.int32, (1,) + cur_tail.shape[1:], 1)
    return jnp.where(sub == 0, pltpu.roll(prev_tail, 1, 1), pltpu.roll(cur_tail, 1, 1))


def _tile_copies(hbm_ref, buf, sems, t, sl, n_seq, to_hbm):
    lc = buf.shape[1]
    b, s = t // n_seq, t % n_seq
    copies = []
    for j in range(SUBLANES):
        rows = hbm_ref.at[b, pl.ds(pl.multiple_of((s * SUBLANES + j) * lc, lc), lc), :]
        chunk = buf.at[sl, :, j, :]
        src, dst = (chunk, rows) if to_hbm else (rows, chunk)
        copies.append(pltpu.make_async_copy(src, dst, sems.at[sl]))
    return copies


def _mixer_kernel(x_hbm, w_in_ref, conv_w_ref, conv_b_ref, w_ax_ref, b_a_ref, b_x_ref, lam_ref,
                  w_pool_ref, b_pool_ref, pool_scale_ref, w_out_ref, g_ref, b_ref, *rest, n_seq):
    n_cast = (len(rest) - 9) // 2
    cast_in, o_hbm, cast_out = rest[:n_cast], rest[n_cast], rest[n_cast + 1:2 * n_cast + 1]
    xbuf, obuf, in_sem, out_sem, pool_hist, conv_hist, h_carry, resid_ref = rest[2 * n_cast + 1:]
    i = pl.program_id(0)
    n_tiles = pl.num_programs(0) - 1
    s = jnp.minimum(i, n_tiles - 1) % n_seq
    slot = i % 2

    def x_copies(t, sl):
        return _tile_copies(x_hbm, xbuf, in_sem, t, sl, n_seq, to_hbm=False)

    def out_copies(t, sl):
        return _tile_copies(o_hbm, obuf, out_sem, t, sl, n_seq, to_hbm=True)

    @pl.when(i == 0)
    def _():
        for c in x_copies(0, 0):
            c.start()
        resid_ref[1] = jnp.zeros(resid_ref.shape[1:], _F32)

    @pl.when(i < n_tiles)
    def _():
        for c in x_copies(i, slot):
            c.wait()

    @pl.when(i + 1 < n_tiles)
    def _():
        for c in x_copies(i + 1, 1 - slot):
            c.start()

    @pl.when(i >= 3)
    def _():
        for c in out_copies(i - 3, 1 - slot):
            c.wait()

    def finish_previous():
        T, D = resid_ref.shape[1:]
        chunk = T // LN_CHUNKS
        zeros = []
        for r0 in range(0, T, chunk):
            out = _layer_norm(resid_ref[1 - slot, r0:r0 + chunk, :], g_ref[...], b_ref[...])
            obuf[1 - slot, r0 // SUBLANES:(r0 + chunk) // SUBLANES] = out.reshape(chunk // SUBLANES, SUBLANES, D)
            zeros.append(_ordering_zero(out))
        return zeros

    @pl.when(i == n_tiles)
    def _():
        finish_previous()

    @pl.when(i < n_tiles)
    def _():
        _mixer_tile(s, slot, finish_previous, xbuf, w_in_ref, conv_w_ref, conv_b_ref, w_ax_ref, b_a_ref, b_x_ref,
                    lam_ref, w_pool_ref, b_pool_ref, pool_scale_ref, w_out_ref, cast_in, cast_out,
                    pool_hist, conv_hist, h_carry, resid_ref)

    @pl.when(i >= 1)
    def _():
        for c in out_copies(i - 1, 1 - slot):
            c.start()

    @pl.when(i == n_tiles)
    def _():
        for c in out_copies(n_tiles - 2, slot) + out_copies(n_tiles - 1, 1 - slot):
            c.wait()


def _mixer_tile(s, slot, finish_previous, xbuf, w_in_ref, conv_w_ref, conv_b_ref, w_ax_ref, b_a_ref, b_x_ref,
                lam_ref, w_pool_ref, b_pool_ref, pool_scale_ref, w_out_ref, cast_in, cast_out,
                pool_hist, conv_hist, h_carry, resid_ref):
    lc, _, D = xbuf.shape[1:]
    T = lc * SUBLANES

    @pl.when(s == 0)
    def _():
        pool_hist[...] = jnp.zeros_like(pool_hist)
        conv_hist[...] = jnp.zeros_like(conv_hist)
        h_carry[...] = jnp.zeros_like(h_carry)

    zeros = finish_previous()
    for src, dst in zip(cast_in, cast_out):
        dst[...] = _pack_rows(src[...])

    x = xbuf[slot].reshape(T, D)
    xb = x.astype(_BF16)
    half = len(zeros) // 2
    up = _dot(xb, _unpack_rows(w_in_ref[:, :POOL_WIDTH])) + _tile_zeros(zeros[:half], T, POOL_WIDTH)
    up = up.reshape(lc, SUBLANES, POOL_WIDTH)
    u_lru = (_dot(xb, _unpack_rows(w_in_ref[:, POOL_WIDTH:POOL_WIDTH + LRU_WIDTH]))
             + _tile_zeros(zeros[half:], T, LRU_WIDTH)).reshape(lc, SUBLANES, LRU_WIDTH)

    kk = lax.broadcasted_iota(jnp.int32, (POOL_HIST, SUBLANES, LANES), 0)
    jj = lax.broadcasted_iota(jnp.int32, (POOL_HIST, SUBLANES, LANES), 1)
    t_head = s * T + jj * lc + kk

    ext = jnp.concatenate([_history_rows(up[lc - POOL_HIST:], pool_hist[...]), up], axis=0)
    pool_hist[...] = up[lc - POOL_HIST:]
    s2 = ext[1:] + ext[:-1]
    s2r = s2[:, :, POOL_GROUP:]
    s4 = s2r[2:] + s2r[:-2]
    s4r = s4[:, :, POOL_GROUP:]
    s8 = s4r[4:] + s4r[:-4]
    s8r = s8[:, :, POOL_GROUP:]
    s16 = s8r[8:] + s8r[:-8]
    sums = (s2[15:, :, :POOL_GROUP], s4[13:, :, :POOL_GROUP], s8[9:, :, :POOL_GROUP], s16[1:])
    y_pool = []
    for g, w in enumerate(POOL_WINDOWS):
        cs = slice(g * POOL_GROUP, (g + 1) * POOL_GROUP)
        inv_head = 1.0 / jnp.minimum(t_head + 1, w).astype(_F32)
        inv_head = jnp.concatenate([inv_head] * (POOL_GROUP // LANES), axis=-1)
        mean = jnp.concatenate([sums[g][:POOL_HIST] * inv_head, sums[g][POOL_HIST:] * (1.0 / w)], axis=0)
        mixed = mean - up[:, :, cs]
        yg = _dot(mixed.reshape(T, POOL_GROUP).astype(_BF16), w_pool_ref[g])
        y_pool.append(((yg + b_pool_ref[:, cs]) * pool_scale_ref[:, cs]).astype(_BF16))
    z_pool = jnp.concatenate(y_pool, axis=1)

    n_hist = CONV_WIDTH - 1
    sub = lax.broadcasted_iota(jnp.int32, (SUBLANES, LRU_HEAD_DIM), 0)
    first = t_head[0:1] == 0
    gate_cols = 2 * LRU_HEAD_DIM
    out_cols = D_MODEL // LRU_HEADS
    y_top, z_lru = [], []
    for h in range(LRU_HEADS):
        hs = slice(h * LRU_HEAD_DIM, (h + 1) * LRU_HEAD_DIM)
        ul = u_lru[:, :, hs]
        ext2 = jnp.concatenate([_history_rows(ul[lc - n_hist:], conv_hist[:, :, hs]), ul], axis=0)
        conv_hist[:, :, hs] = ul[lc - n_hist:]
        xc = ext2[0:lc] * conv_w_ref[0:1, hs]
        for k in range(1, CONV_WIDTH):
            xc = xc + ext2[k:k + lc] * conv_w_ref[k:k + 1, hs]
        xc = xc + conv_b_ref[:, hs]

        pre = _dot(xc.reshape(T, LRU_HEAD_DIM).astype(_BF16), w_ax_ref[h])
        pre = pre.reshape(lc, SUBLANES, 2 * LRU_HEAD_DIM)
        r = jax.nn.sigmoid(pre[:, :, :LRU_HEAD_DIM] + b_a_ref[:, hs])
        i = jax.nn.sigmoid(pre[:, :, LRU_HEAD_DIM:] + b_x_ref[:, hs])

        if h % 2 == 0:
            c0 = POOL_WIDTH + LRU_WIDTH + h * LRU_HEAD_DIM
            u_gate = _dot(xb, _unpack_rows(w_in_ref[:, c0:c0 + gate_cols])).reshape(lc, SUBLANES, gate_cols)
        else:
            c0 = (h // 2) * out_cols
            y_top.append(_dot(z_pool, _unpack_rows(w_out_ref[:POOL_WIDTH // 2, c0:c0 + out_cols])))

        log_a = (-LRU_C * r) * jax.nn.softplus(-lam_ref[:, hs])
        a = jnp.exp(log_a)
        u = -jnp.tanh(log_a) * (a * a + 1.0)
        mult = jnp.where(u == 0.0, 0.0, u * lax.rsqrt(u))
        mult = jnp.concatenate([jnp.where(first, 1.0, mult[0:1]), mult[1:]], axis=0)
        bv = mult * (i * xc)

        hl, ac = [bv[0]], [a[0]]
        for k in range(1, lc):
            hl.append(a[k] * hl[-1] + bv[k])
            ac.append(a[k] * ac[-1])
        c_a, c_b = ac[-1], hl[-1]
        for d in (1, 2, 4):
            a_s = jnp.where(sub < d, 1.0, pltpu.roll(c_a, d, 0))
            b_s = jnp.where(sub < d, 0.0, pltpu.roll(c_b, d, 0))
            c_b = c_a * b_s + c_b
            c_a = c_a * a_s
        h_prev = h_carry[:, hs]
        h_end = c_b + c_a * h_prev
        h_in = jnp.where(sub == 0, h_prev, pltpu.roll(h_end, 1, 0))
        h_carry[:, hs] = jnp.broadcast_to(h_end[SUBLANES - 1:SUBLANES, :], (SUBLANES, LRU_HEAD_DIM))
        hseq = jnp.stack(hl, axis=0) + jnp.stack(ac, axis=0) * h_in
        gate = jax.nn.gelu(u_gate[:, :, (h % 2) * LRU_HEAD_DIM:(h % 2 + 1) * LRU_HEAD_DIM])
        z_lru.append((hseq * gate).reshape(T, LRU_HEAD_DIM).astype(_BF16))

    for q in range(LRU_HEADS // 2, LRU_HEADS):
        y_top.append(_dot(z_pool, _unpack_rows(w_out_ref[:POOL_WIDTH // 2, q * out_cols:(q + 1) * out_cols])))
    y = (jnp.concatenate(y_top, axis=1)
         + _dot(jnp.concatenate(z_lru, axis=1), _unpack_rows(w_out_ref[POOL_WIDTH // 2:, :])))
    resid_ref[slot] = DEEPNORM_ALPHA * x + y


def _const_spec(shape, single_buffer=False):
    nd = len(shape)
    kwargs = {"pipeline_mode": pl.Buffered(1)} if single_buffer else {}
    return pl.BlockSpec(shape, lambda *_: (0,) * nd, **kwargs)


def _mixer(x, w_in, conv_w, conv_b, w_ax, b_a, b_x, lam, w_pool, b_pool, pool_scale, w_out, g, b,
           cast_weights):
    B, S, D = x.shape
    T = MIXER_TILE
    n_seq = S // T
    n_steps = B * n_seq
    lc = T // SUBLANES
    assert T % SUBLANES == 0 and lc >= POOL_HIST and lc % (2 * SUBLANES) == 0 and n_steps >= 3
    assert T % (LN_CHUNKS * SUBLANES) == 0 and LN_CHUNKS % 2 == 0

    chunk_in, chunk_out = [], []
    for w in cast_weights:
        assert w.shape[0] % (2 * SUBLANES * n_steps) == 0
        rows_w = w.shape[0] // n_steps
        chunk_in.append(pl.BlockSpec((rows_w, w.shape[1]), lambda i: (jnp.minimum(i, n_steps - 1), 0)))
        chunk_out.append(pl.BlockSpec((rows_w // 2, w.shape[1]), lambda i: (jnp.minimum(i, n_steps - 1), 0)))
    return pl.pallas_call(
        functools.partial(_mixer_kernel, n_seq=n_seq),
        grid=(n_steps + 1,),
        in_specs=[
            pl.BlockSpec(memory_space=pl.ANY),
            _const_spec(w_in.shape, True),
            _const_spec(conv_w.shape), _const_spec(conv_b.shape),
            _const_spec(w_ax.shape, True),
            _const_spec(b_a.shape), _const_spec(b_x.shape), _const_spec(lam.shape),
            _const_spec(w_pool.shape, True),
            _const_spec(b_pool.shape), _const_spec(pool_scale.shape),
            _const_spec(w_out.shape, True),
            _const_spec(g.shape), _const_spec(b.shape),
        ] + chunk_in,
        out_specs=[pl.BlockSpec(memory_space=pl.ANY)] + chunk_out,
        out_shape=[jax.ShapeDtypeStruct(x.shape, _F32)]
        + [jax.ShapeDtypeStruct((w.shape[0] // 2, w.shape[1]), jnp.uint32) for w in cast_weights],
        scratch_shapes=[
            pltpu.VMEM((2, lc, SUBLANES, D), _F32),
            pltpu.VMEM((2, lc, SUBLANES, D), _F32),
            pltpu.SemaphoreType.DMA((2,)),
            pltpu.SemaphoreType.DMA((2,)),
            pltpu.VMEM((POOL_HIST, SUBLANES, POOL_WIDTH), _F32),
            pltpu.VMEM((CONV_WIDTH - 1, SUBLANES, LRU_WIDTH), _F32),
            pltpu.VMEM((SUBLANES, LRU_WIDTH), _F32),
            pltpu.VMEM((2, T, D), _F32),
        ],
        compiler_params=pltpu.CompilerParams(
            dimension_semantics=("arbitrary",), vmem_limit_bytes=VMEM_LIMIT_BYTES),
        name="mixer",
    )(x, w_in, conv_w, conv_b, w_ax, b_a, b_x, lam, w_pool, b_pool, pool_scale, w_out, g, b, *cast_weights)


def _kv_kernel(mem_ref, wk_ref, wv_ref, wa_ref, wx_ref, wpool_ref, *rest):
    n_cast = (len(rest) - 5) // 2
    cast_in, k_ref, v_ref = rest[:n_cast], rest[n_cast], rest[n_cast + 1]
    cast_out = rest[n_cast + 2:2 * n_cast + 2]
    wax_ref, wpool_b_ref, mb_ref = rest[2 * n_cast + 2:]
    for src, dst in zip(cast_in, cast_out):
        dst[...] = _pack_rows(src[...])

    @pl.when(pl.program_id(0) == 0)
    def _():
        mb_ref[...] = mem_ref[...].astype(_BF16)
        wax_ref[...] = jnp.concatenate([wa_ref[...], wx_ref[...]], axis=-1).astype(_BF16)
        wpool_b_ref[...] = wpool_ref[...].astype(_BF16)

    k_ref[...] = _pack_rows(_dot(mb_ref[...], wk_ref[...].astype(_BF16)))
    v_ref[...] = _pack_rows(_dot(mb_ref[...], wv_ref[...].astype(_BF16)))


def _kv_proj(mem2d, w_k, w_v, w_a, w_x, w_pool, cast_weights):
    M, D = mem2d.shape
    tn = KV_TILE_N
    n_steps = D // tn
    wspec = pl.BlockSpec((D, tn), lambda j: (0, j))
    ospec = pl.BlockSpec((M // 2, tn), lambda j: (0, j))
    chunk_in, chunk_out = [], []
    for w in cast_weights:
        assert w.shape[0] % (2 * SUBLANES * n_steps) == 0
        rows_w = w.shape[0] // n_steps
        chunk_in.append(pl.BlockSpec((rows_w, w.shape[1]), lambda j: (j, 0)))
        chunk_out.append(pl.BlockSpec((rows_w // 2, w.shape[1]), lambda j: (j, 0)))
    wax_shape = w_a.shape[:-1] + (w_a.shape[-1] + w_x.shape[-1],)
    return pl.pallas_call(
        _kv_kernel,
        grid=(n_steps,),
        in_specs=[_const_spec(mem2d.shape, True), wspec, wspec,
                  _const_spec(w_a.shape), _const_spec(w_x.shape), _const_spec(w_pool.shape)] + chunk_in,
        out_specs=[ospec, ospec] + chunk_out + [_const_spec(wax_shape), _const_spec(w_pool.shape)],
        out_shape=[jax.ShapeDtypeStruct((M // 2, D), jnp.uint32)] * 2
        + [jax.ShapeDtypeStruct((w.shape[0] // 2, w.shape[1]), jnp.uint32) for w in cast_weights]
        + [jax.ShapeDtypeStruct(wax_shape, _BF16), jax.ShapeDtypeStruct(w_pool.shape, _BF16)],
        scratch_shapes=[pltpu.VMEM((M, D), _BF16)],
        compiler_params=pltpu.CompilerParams(
            dimension_semantics=("arbitrary",), vmem_limit_bytes=VMEM_LIMIT_BYTES),
        name="kv_proj",
    )(mem2d, w_k, w_v, w_a, w_x, w_pool, *cast_weights)


def _xattn_kernel(x_ref, xn_ref, k_ref, v_ref, wq_ref, wo_ref, g_ref, b_ref, o_ref, resid_ref, q_ref):
    i = pl.program_id(0)
    n_tiles = pl.num_programs(0) - 1
    slot = i % 2
    T = x_ref.shape[1]
    scale = XATTN_HEAD_DIM ** -0.5
    heads = [slice(h * XATTN_HEAD_DIM, (h + 1) * XATTN_HEAD_DIM) for h in range(XATTN_HEADS)]

    def query(xb, h):
        return _dot(xb, _unpack_rows(wq_ref[:, heads[h]])).astype(_BF16)

    @pl.when(i == 0)
    def _():
        resid_ref[1] = jnp.zeros(resid_ref.shape[1:], _F32)
        xb0 = x_ref[0].astype(_BF16)
        for h in range(XATTN_HEADS):
            q_ref[:, heads[h]] = query(xb0, h)

    def finish_previous():
        return _normalize_previous(resid_ref, slot, g_ref, b_ref, o_ref)

    @pl.when(i < n_tiles)
    def _():
        zeros = finish_previous()
        per_head = len(zeros) // XATTN_HEADS
        x = x_ref[0]
        xnb = xn_ref[0].astype(_BF16)

        def probs(h):
            sc = lax.dot_general(q_ref[:, heads[h]], _unpack_rows(k_ref[0, :, heads[h]]),
                                 (((1,), (1,)), ((), ())), preferred_element_type=_F32) * scale
            sc = sc + _tile_zeros(zeros[h * per_head:(h + 1) * per_head], T, N_MEM)
            e = jnp.exp(sc - jnp.max(sc, axis=-1, keepdims=True))
            return (e * (1.0 / jnp.sum(e, axis=-1, keepdims=True))).astype(_BF16)

        def attend(p, h):
            return _dot(p, _unpack_rows(v_ref[0, :, heads[h]])).astype(_BF16)

        o_parts = []
        for h in range(XATTN_HEADS):
            p = probs(h)
            q_ref[:, heads[h]] = query(xnb, h)
            o_parts.append(attend(p, h))
        y = _dot(jnp.concatenate(o_parts, axis=1), _unpack_rows(wo_ref[...]))
        resid_ref[slot] = DEEPNORM_ALPHA * x + y

    @pl.when(i == n_tiles)
    def _():
        finish_previous()


def _xattn(x, k, v, w_q, w_o, g, b):
    B, S, D = x.shape
    T = XATTN_TILE
    n_seq = S // T
    n_tiles = B * n_seq
    assert T % (LN_CHUNKS * SUBLANES) == 0 and LN_CHUNKS % XATTN_HEADS == 0

    def tile(t):
        t = jnp.clip(t, 0, n_tiles - 1)
        return t // n_seq, t % n_seq

    kvspec = pl.BlockSpec((1, N_MEM // 2, D), lambda i: (tile(i)[0], 0, 0))
    return pl.pallas_call(
        _xattn_kernel,
        grid=(n_tiles + 1,),
        in_specs=[pl.BlockSpec((1, T, D), lambda i: (*tile(i), 0)),
                  pl.BlockSpec((1, T, D), lambda i: (*tile(i + 1), 0)),
                  kvspec, kvspec,
                  _const_spec(w_q.shape, True), _const_spec(w_o.shape, True),
                  _const_spec(g.shape), _const_spec(b.shape)],
        out_specs=pl.BlockSpec((1, T, D), lambda i: (*tile(i - 1), 0)),
        out_shape=jax.ShapeDtypeStruct(x.shape, _F32),
        scratch_shapes=[pltpu.VMEM((2, T, D), _F32), pltpu.VMEM((T, D), _BF16)],
        compiler_params=pltpu.CompilerParams(
            dimension_semantics=("arbitrary",), vmem_limit_bytes=XATTN_VMEM_LIMIT_BYTES),
        name="xattn",
    )(x, x, k, v, w_q, w_o, g, b)


def _mlp_kernel(x_hbm, w1_ref, w2_ref, g_ref, b_ref, o_hbm, xstage, xb_ref, acc_ref, oslice, x_sem, o_sem,
                *, n_f, halves):
    i, j = pl.program_id(0), pl.program_id(1)
    n_tiles = pl.num_programs(0) - 1
    slot = i % 2
    tm = xstage.shape[0]
    rows = tm // n_f
    q = i * n_f + j
    k = q % 2

    def x_copy(t):
        return pltpu.make_async_copy(x_hbm.at[pl.ds(pl.multiple_of(t * tm, tm), tm), :], xstage, x_sem.at[0])

    def out_copy(step, buf):
        r0 = pl.multiple_of((step // n_f - 1) * tm + (step % n_f) * rows, rows)
        return pltpu.make_async_copy(oslice.at[buf], o_hbm.at[pl.ds(r0, rows), :], o_sem.at[buf])

    @pl.when(q == 0)
    def _():
        x_copy(0).start()
        acc_ref[1] = jnp.zeros(acc_ref.shape[1:], _F32)

    @pl.when((i < n_tiles) & (j == 0))
    def _():
        x_copy(i).wait()
        xb_ref[...] = xstage[...].astype(_BF16)

    @pl.when(q >= n_f + 2)
    def _():
        out_copy(q - 2, k).wait()

    def finish_previous_rows():
        r0 = pl.multiple_of(j * rows, rows)
        oslice[k] = _layer_norm(acc_ref[1 - slot, pl.ds(r0, rows), :], g_ref[...], b_ref[...])

    def accumulate(first):
        finish_previous_rows()
        w1 = _unpack_rows(w1_ref[...])
        w2 = _unpack_rows(w2_ref[...])
        half = tm // halves
        hs = [jnp.square(jnp.maximum(_dot(xb_ref[r * half:(r + 1) * half, :], w1), 0.0)).astype(_BF16)
              for r in range(halves)]
        for r in range(halves):
            rs = slice(r * half, (r + 1) * half)
            prior = DEEPNORM_ALPHA * xstage[rs, :] if first else acc_ref[slot, rs, :]
            acc_ref[slot, rs, :] = prior + _dot(hs[r], w2)

    @pl.when((i < n_tiles) & (j == 0))
    def _():
        accumulate(first=True)

    @pl.when((i < n_tiles) & (j > 0))
    def _():
        accumulate(first=False)

    @pl.when((i + 1 < n_tiles) & (j == 0))
    def _():
        x_copy(i + 1).start()

    @pl.when(i == n_tiles)
    def _():
        finish_previous_rows()

    @pl.when(q >= n_f)
    def _():
        out_copy(q, k).start()

    @pl.when(q == (n_tiles + 1) * n_f - 1)
    def _():
        out_copy(q - 1, 1 - k).wait()
        out_copy(q, k).wait()


def _mlp(x2d, w1, w2, g, b):
    M, D = x2d.shape
    F = w1.shape[1]
    tm, tf = MLP_TILE_M, MLP_TILE_F
    n_tiles, n_f = M // tm, F // tf
    assert tm % (n_f * SUBLANES) == 0 and tm % MLP_ROW_HALVES == 0 and n_f >= 2

    def wblk(i, j):
        return jnp.where(i == n_tiles, n_f - 1, j)

    return pl.pallas_call(
        functools.partial(_mlp_kernel, n_f=n_f, halves=MLP_ROW_HALVES),
        grid=(n_tiles + 1, n_f),
        in_specs=[pl.BlockSpec(memory_space=pl.ANY),
                  pl.BlockSpec((D // 2, tf), lambda i, j: (0, wblk(i, j))),
                  pl.BlockSpec((tf // 2, D), lambda i, j: (wblk(i, j), 0)),
                  _const_spec(g.shape), _const_spec(b.shape)],
        out_specs=pl.BlockSpec(memory_space=pl.ANY),
        out_shape=jax.ShapeDtypeStruct((M, D), _F32),
        scratch_shapes=[pltpu.VMEM((tm, D), _F32),
                        pltpu.VMEM((tm, D), _BF16),
                        pltpu.VMEM((2, tm, D), _F32),
                        pltpu.VMEM((2, tm // n_f, D), _F32),
                        pltpu.SemaphoreType.DMA((1,)),
                        pltpu.SemaphoreType.DMA((2,))],
        compiler_params=pltpu.CompilerParams(
            dimension_semantics=("arbitrary", "arbitrary"),
            vmem_limit_bytes=VMEM_LIMIT_BYTES),
        name="mlp",
    )(x2d, w1, w2, g, b)


def kernel(x, mem, w_in, conv_w, conv_b, w_a, b_a, w_x, b_x, lru_lambda, w_pool, b_pool, pool_scale,
           w_out, ln1_g, ln1_b, w_q, w_k, w_v, w_o, ln2_g, ln2_b, w_ff1, w_ff2, ln3_g, ln3_b):
    B, S, D = x.shape
    row = lambda p: p.reshape(1, -1)
    for l in range(DEPTH):
        k, v, w_in_b, w_out_b, w_ax, w_pool_b = _kv_proj(
            mem.reshape(B * N_MEM, D), w_k[l], w_v[l], w_a[l], w_x[l], w_pool[l], (w_in[l], w_out[l]))
        x, wq_b, wo_b, w1_b, w2_b = _mixer(
            x, w_in_b, conv_w[l], row(conv_b[l]), w_ax,
            row(b_a[l]), row(b_x[l]), row(lru_lambda[l]),
            w_pool_b, row(b_pool[l]), row(pool_scale[l]),
            w_out_b, row(ln1_g[l]), row(ln1_b[l]),
            (w_q[l], w_o[l], w_ff1[l], w_ff2[l]))
        x = _xattn(x, k.reshape(B, N_MEM // 2, D), v.reshape(B, N_MEM // 2, D),
                   wq_b, wo_b, row(ln2_g[l]), row(ln2_b[l]))
        x = _mlp(x.reshape(B * S, D), w1_b, w2_b, row(ln3_g[l]), row(ln3_b[l])).reshape(B, S, D)
    return x
```

```python
import functools

import jax
import jax.numpy as jnp
from jax import lax
from jax.experimental import pallas as pl
from jax.experimental.pallas import tpu as pltpu

D_MODEL = 2048
POOL_WIDTH = 1024
LRU_WIDTH = 1024
POOL_WINDOWS = (2, 4, 8, 16)
POOL_GROUP = 256
LRU_HEADS = 8
LRU_HEAD_DIM = 128
CONV_WIDTH = 4
LRU_C = 8.0
N_MEM = 256
XATTN_HEADS = 4
XATTN_HEAD_DIM = 512
LN_EPS = 1e-5
DEPTH = 1
DEEPNORM_ALPHA = (2.0 * DEPTH) ** 0.25

POOL_HIST = 16
SUBLANES = 8
LANES = 128

MIXER_TILE = 512
XATTN_TILE = 256
LN_CHUNKS = 8
MLP_TILE_M = 1024
MLP_ROW_HALVES = 2
MLP_TILE_F = 1024
KV_TILE_N = 256
VMEM_LIMIT_BYTES = 56 * 1024 * 1024
MIXER_VMEM_LIMIT_BYTES = 60 * 1024 * 1024

_F32 = jnp.float32
_BF16 = jnp.bfloat16


def _layer_norm(v, g, b):
    mu = jnp.mean(v, axis=-1, keepdims=True)
    c = v - mu
    var = jnp.mean(jnp.square(c), axis=-1, keepdims=True)
    return c * lax.rsqrt(var + LN_EPS) * g + b


def _dot(a, b):
    return jnp.dot(a, b, preferred_element_type=_F32)


def _pack_rows(v):
    return pltpu.bitcast(v.astype(_BF16), jnp.uint32)


def _unpack_rows(w):
    return pltpu.bitcast(w, _BF16)


def _ordering_zero(v):
    rows, cols = v.shape
    t = jnp.sum(v.reshape(rows // SUBLANES, SUBLANES, cols), axis=0)
    t = functools.reduce(lambda a, b: a + b, [t[:, c:c + LANES] for c in range(0, cols, LANES)])
    return jnp.minimum(jnp.abs(t), 0.0)


def _normalize_previous(resid_ref, slot, g_ref, b_ref, o_ref):
    T = resid_ref.shape[1]
    chunk = T // LN_CHUNKS
    zeros = []
    for r0 in range(0, T, chunk):
        out = _layer_norm(resid_ref[1 - slot, r0:r0 + chunk, :], g_ref[...], b_ref[...])
        o_ref[0, r0:r0 + chunk, :] = out
        zeros.append(_ordering_zero(out))
    return zeros


def _tile_zeros(zeros, rows, cols):
    width = cols // len(zeros)
    return jnp.concatenate([jnp.tile(z, (rows // SUBLANES, width // LANES)) for z in zeros], axis=1)


def _history_rows(cur_tail, prev_tail):
    sub = lax.broadcasted_iota(jnp.int32, (1,) + cur_tail.shape[1:], 1)
    return jnp.where(sub == 0, pltpu.roll(prev_tail, 1, 1), pltpu.roll(cur_tail, 1, 1))


def _tile_copies(hbm_ref, buf, sems, t, sl, n_seq, to_hbm):
    lc = buf.shape[1]
    b, s = t // n_seq, t % n_seq
    copies = []
    for j in range(SUBLANES):
        rows = hbm_ref.at[b, pl.ds(pl.multiple_of((s * SUBLANES + j) * lc, lc), lc), :]
        chunk = buf.at[sl, :, j, :]
        src, dst = (chunk, rows) if to_hbm else (rows, chunk)
        copies.append(pltpu.make_async_copy(src, dst, sems.at[sl]))
    return copies


def _mixer_kernel(x_hbm, w_in_ref, conv_w_ref, conv_b_ref, w_ax_ref, b_a_ref, b_x_ref, lam_ref,
                  w_pool_ref, b_pool_ref, pool_scale_ref, w_out_ref, g_ref, b_ref, *rest, n_seq):
    n_cast = (len(rest) - 9) // 2
    cast_in, o_hbm, cast_out = rest[:n_cast], rest[n_cast], rest[n_cast + 1:2 * n_cast + 1]
    xbuf, obuf, in_sem, out_sem, pool_hist, conv_hist, h_carry, resid_ref = rest[2 * n_cast + 1:]
    i = pl.program_id(0)
    n_tiles = pl.num_programs(0) - 1
    s = jnp.minimum(i, n_tiles - 1) % n_seq
    slot = i % 2

    def x_copies(t, sl):
        return _tile_copies(x_hbm, xbuf, in_sem, t, sl, n_seq, to_hbm=False)

    def out_copies(t, sl):
        return _tile_copies(o_hbm, obuf, out_sem, t, sl, n_seq, to_hbm=True)

    @pl.when(i == 0)
    def _():
        for c in x_copies(0, 0):
            c.start()
        resid_ref[1] = jnp.zeros(resid_ref.shape[1:], _F32)

    @pl.when(i < n_tiles)
    def _():
        for c in x_copies(i, slot):
            c.wait()

    @pl.when(i + 1 < n_tiles)
    def _():
        for c in x_copies(i + 1, 1 - slot):
            c.start()

    @pl.when(i >= 3)
    def _():
        for c in out_copies(i - 3, 1 - slot):
            c.wait()

    def finish_previous():
        T, D = resid_ref.shape[1:]
        chunk = T // LN_CHUNKS
        zeros = []
        for r0 in range(0, T, chunk):
            out = _layer_norm(resid_ref[1 - slot, r0:r0 + chunk, :], g_ref[...], b_ref[...])
            obuf[1 - slot, r0 // SUBLANES:(r0 + chunk) // SUBLANES] = out.reshape(chunk // SUBLANES, SUBLANES, D)
            zeros.append(_ordering_zero(out))
        return zeros

    @pl.when(i == n_tiles)
    def _():
        finish_previous()

    @pl.when(i < n_tiles)
    def _():
        _mixer_tile(s, slot, finish_previous, xbuf, w_in_ref, conv_w_ref, conv_b_ref, w_ax_ref, b_a_ref, b_x_ref,
                    lam_ref, w_pool_ref, b_pool_ref, pool_scale_ref, w_out_ref, cast_in, cast_out,
                    pool_hist, conv_hist, h_carry, resid_ref)

    @pl.when(i >= 1)
    def _():
        for c in out_copies(i - 1, 1 - slot):
            c.start()

    @pl.when(i == n_tiles)
    def _():
        for c in out_copies(n_tiles - 2, slot) + out_copies(n_tiles - 1, 1 - slot):
            c.wait()


def _mixer_tile(s, slot, finish_previous, xbuf, w_in_ref, conv_w_ref, conv_b_ref, w_ax_ref, b_a_ref, b_x_ref,
                lam_ref, w_pool_ref, b_pool_ref, pool_scale_ref, w_out_ref, cast_in, cast_out,
                pool_hist, conv_hist, h_carry, resid_ref):
    lc, _, D = xbuf.shape[1:]
    T = lc * SUBLANES

    @pl.when(s == 0)
    def _():
        pool_hist[...] = jnp.zeros_like(pool_hist)
        conv_hist[...] = jnp.zeros_like(conv_hist)
        h_carry[...] = jnp.zeros_like(h_carry)

    zeros = finish_previous()
    for src, dst in zip(cast_in, cast_out):
        dst[...] = _pack_rows(src[...])

    x = xbuf[slot].reshape(T, D)
    xb = x.astype(_BF16)
    half = len(zeros) // 2
    up = _dot(xb, _unpack_rows(w_in_ref[:, :POOL_WIDTH])) + _tile_zeros(zeros[:half], T, POOL_WIDTH)
    up = up.reshape(lc, SUBLANES, POOL_WIDTH)
    u_lru = (_dot(xb, _unpack_rows(w_in_ref[:, POOL_WIDTH:POOL_WIDTH + LRU_WIDTH]))
             + _tile_zeros(zeros[half:], T, LRU_WIDTH)).reshape(lc, SUBLANES, LRU_WIDTH)

    kk = lax.broadcasted_iota(jnp.int32, (POOL_HIST, SUBLANES, LANES), 0)
    jj = lax.broadcasted_iota(jnp.int32, (POOL_HIST, SUBLANES, LANES), 1)
    t_head = s * T + jj * lc + kk

    ext = jnp.concatenate([_history_rows(up[lc - POOL_HIST:], pool_hist[...]), up], axis=0)
    pool_hist[...] = up[lc - POOL_HIST:]
    s2 = ext[1:] + ext[:-1]
    s2r = s2[:, :, POOL_GROUP:]
    s4 = s2r[2:] + s2r[:-2]
    s4r = s4[:, :, POOL_GROUP:]
    s8 = s4r[4:] + s4r[:-4]
    s8r = s8[:, :, POOL_GROUP:]
    s16 = s8r[8:] + s8r[:-8]
    sums = (s2[15:, :, :POOL_GROUP], s4[13:, :, :POOL_GROUP], s8[9:, :, :POOL_GROUP], s16[1:])
    y_pool = []
    for g, w in enumerate(POOL_WINDOWS):
        cs = slice(g * POOL_GROUP, (g + 1) * POOL_GROUP)
        inv_head = 1.0 / jnp.minimum(t_head + 1, w).astype(_F32)
        inv_head = jnp.concatenate([inv_head] * (POOL_GROUP // LANES), axis=-1)
        mean = jnp.concatenate([sums[g][:POOL_HIST] * inv_head, sums[g][POOL_HIST:] * (1.0 / w)], axis=0)
        mixed = mean - up[:, :, cs]
        yg = _dot(mixed.reshape(T, POOL_GROUP).astype(_BF16), w_pool_ref[g])
        y_pool.append(((yg + b_pool_ref[:, cs]) * pool_scale_ref[:, cs]).astype(_BF16))
    z_pool = jnp.concatenate(y_pool, axis=1)

    n_hist = CONV_WIDTH - 1
    sub = lax.broadcasted_iota(jnp.int32, (SUBLANES, LRU_HEAD_DIM), 0)
    first = t_head[0:1] == 0
    gate_cols = 2 * LRU_HEAD_DIM
    out_cols = D_MODEL // LRU_HEADS
    y_top, z_lru = [], []
    for h in range(LRU_HEADS):
        hs = slice(h * LRU_HEAD_DIM, (h + 1) * LRU_HEAD_DIM)
        ul = u_lru[:, :, hs]
        ext2 = jnp.concatenate([_history_rows(ul[lc - n_hist:], conv_hist[:, :, hs]), ul], axis=0)
        conv_hist[:, :, hs] = ul[lc - n_hist:]
        xc = ext2[0:lc] * conv_w_ref[0:1, hs]
        for k in range(1, CONV_WIDTH):
            xc = xc + ext2[k:k + lc] * conv_w_ref[k:k + 1, hs]
        xc = xc + conv_b_ref[:, hs]

        pre = _dot(xc.reshape(T, LRU_HEAD_DIM).astype(_BF16), w_ax_ref[h])
        pre = pre.reshape(lc, SUBLANES, 2 * LRU_HEAD_DIM)
        r = jax.nn.sigmoid(pre[:, :, :LRU_HEAD_DIM] + b_a_ref[:, hs])
        i = jax.nn.sigmoid(pre[:, :, LRU_HEAD_DIM:] + b_x_ref[:, hs])

        if h % 2 == 0:
            c0 = POOL_WIDTH + LRU_WIDTH + h * LRU_HEAD_DIM
            u_gate = _dot(xb, _unpack_rows(w_in_ref[:, c0:c0 + gate_cols])).reshape(lc, SUBLANES, gate_cols)
        else:
            c0 = (h // 2) * out_cols
            y_top.append(_dot(z_pool, _unpack_rows(w_out_ref[:POOL_WIDTH // 2, c0:c0 + out_cols])))

        log_a = (-LRU_C * r) * jax.nn.softplus(-lam_ref[:, hs])
        a = jnp.exp(log_a)
        u = -jnp.tanh(log_a) * (a * a + 1.0)
        mult = jnp.where(u == 0.0, 0.0, u * lax.rsqrt(u))
        mult = jnp.concatenate([jnp.where(first, 1.0, mult[0:1]), mult[1:]], axis=0)
        bv = mult * (i * xc)

        hl, ac = [bv[0]], [a[0]]
        for k in range(1, lc):
            hl.append(a[k] * hl[-1] + bv[k])
            ac.append(a[k] * ac[-1])
        c_a, c_b = ac[-1], hl[-1]
        for d in (1, 2, 4):
            a_s = jnp.where(sub < d, 1.0, pltpu.roll(c_a, d, 0))
            b_s = jnp.where(sub < d, 0.0, pltpu.roll(c_b, d, 0))
            c_b = c_a * b_s + c_b
            c_a = c_a * a_s
        h_prev = h_carry[:, hs]
        h_end = c_b + c_a * h_prev
        h_in = jnp.where(sub == 0, h_prev, pltpu.roll(h_end, 1, 0))
        h_carry[:, hs] = jnp.broadcast_to(h_end[SUBLANES - 1:SUBLANES, :], (SUBLANES, LRU_HEAD_DIM))
        hseq = jnp.stack(hl, axis=0) + jnp.stack(ac, axis=0) * h_in
        gate = jax.nn.gelu(u_gate[:, :, (h % 2) * LRU_HEAD_DIM:(h % 2 + 1) * LRU_HEAD_DIM])
        z_lru.append((hseq * gate).reshape(T, LRU_HEAD_DIM).astype(_BF16))

    for q in range(LRU_HEADS // 2, LRU_HEADS):
        y_top.append(_dot(z_pool, _unpack_rows(w_out_ref[:POOL_WIDTH // 2, q * out_cols:(q + 1) * out_cols])))
    y = (jnp.concatenate(y_top, axis=1)
         + _dot(jnp.concatenate(z_lru, axis=1), _unpack_rows(w_out_ref[POOL_WIDTH // 2:, :])))
    resid_ref[slot] = DEEPNORM_ALPHA * x + y


def _const_spec(shape, single_buffer=False):
    nd = len(shape)
    kwargs = {"pipeline_mode": pl.Buffered(1)} if single_buffer else {}
    return pl.BlockSpec(shape, lambda *_: (0,) * nd, **kwargs)


def _mixer(x, w_in, conv_w, conv_b, w_ax, b_a, b_x, lam, w_pool, b_pool, pool_scale, w_out, g, b,
           cast_weights):
    B, S, D = x.shape
    T = MIXER_TILE
    n_seq = S // T
    n_steps = B * n_seq
    lc = T // SUBLANES
    assert T % SUBLANES == 0 and lc >= POOL_HIST and lc % (2 * SUBLANES) == 0 and n_steps >= 3
    assert T % (LN_CHUNKS * SUBLANES) == 0 and LN_CHUNKS % 2 == 0

    chunk_in, chunk_out = [], []
    for w in cast_weights:
        assert w.shape[0] % (2 * SUBLANES * n_steps) == 0
        rows_w = w.shape[0] // n_steps
        chunk_in.append(pl.BlockSpec((rows_w, w.shape[1]), lambda i: (jnp.minimum(i, n_steps - 1), 0)))
        chunk_out.append(pl.BlockSpec((rows_w // 2, w.shape[1]), lambda i: (jnp.minimum(i, n_steps - 1), 0)))
    return pl.pallas_call(
        functools.partial(_mixer_kernel, n_seq=n_seq),
        grid=(n_steps + 1,),
        in_specs=[
            pl.BlockSpec(memory_space=pl.ANY),
            _const_spec(w_in.shape, True),
            _const_spec(conv_w.shape), _const_spec(conv_b.shape),
            _const_spec(w_ax.shape, True),
            _const_spec(b_a.shape), _const_spec(b_x.shape), _const_spec(lam.shape),
            _const_spec(w_pool.shape, True),
            _const_spec(b_pool.shape), _const_spec(pool_scale.shape),
            _const_spec(w_out.shape, True),
            _const_spec(g.shape), _const_spec(b.shape),
        ] + chunk_in,
        out_specs=[pl.BlockSpec(memory_space=pl.ANY)] + chunk_out,
        out_shape=[jax.ShapeDtypeStruct(x.shape, _F32)]
        + [jax.ShapeDtypeStruct((w.shape[0] // 2, w.shape[1]), jnp.uint32) for w in cast_weights],
        scratch_shapes=[
            pltpu.VMEM((2, lc, SUBLANES, D), _F32),
            pltpu.VMEM((2, lc, SUBLANES, D), _F32),
            pltpu.SemaphoreType.DMA((2,)),
            pltpu.SemaphoreType.DMA((2,)),
            pltpu.VMEM((POOL_HIST, SUBLANES, POOL_WIDTH), _F32),
            pltpu.VMEM((CONV_WIDTH - 1, SUBLANES, LRU_WIDTH), _F32),
            pltpu.VMEM((SUBLANES, LRU_WIDTH), _F32),
            pltpu.VMEM((2, T, D), _F32),
        ],
        compiler_params=pltpu.CompilerParams(
            dimension_semantics=("arbitrary",), vmem_limit_bytes=MIXER_VMEM_LIMIT_BYTES),
        name="mixer",
    )(x, w_in, conv_w, conv_b, w_ax, b_a, b_x, lam, w_pool, b_pool, pool_scale, w_out, g, b, *cast_weights)


def _kv_kernel(mem_ref, wk_ref, wv_ref, wa_ref, wx_ref, wpool_ref, *rest):
    n_cast = (len(rest) - 5) // 2
    cast_in, k_ref, v_ref = rest[:n_cast], rest[n_cast], rest[n_cast + 1]
    cast_out = rest[n_cast + 2:2 * n_cast + 2]
    wax_ref, wpool_b_ref, mb_ref = rest[2 * n_cast + 2:]
    for src, dst in zip(cast_in, cast_out):
        dst[...] = _pack_rows(src[...])

    @pl.when(pl.program_id(0) == 0)
    def _():
        mb_ref[...] = mem_ref[...].astype(_BF16)
        wax_ref[...] = jnp.concatenate([wa_ref[...], wx_ref[...]], axis=-1).astype(_BF16)
        wpool_b_ref[...] = wpool_ref[...].astype(_BF16)

    k_ref[...] = _pack_rows(_dot(mb_ref[...], wk_ref[...].astype(_BF16)))
    v_ref[...] = _pack_rows(_dot(mb_ref[...], wv_ref[...].astype(_BF16)))


def _kv_proj(mem2d, w_k, w_v, w_a, w_x, w_pool, cast_weights):
    M, D = mem2d.shape
    tn = KV_TILE_N
    n_steps = D // tn
    wspec = pl.BlockSpec((D, tn), lambda j: (0, j))
    ospec = pl.BlockSpec((M // 2, tn), lambda j: (0, j))
    chunk_in, chunk_out = [], []
    for w in cast_weights:
        assert w.shape[0] % (2 * SUBLANES * n_steps) == 0
        rows_w = w.shape[0] // n_steps
        chunk_in.append(pl.BlockSpec((rows_w, w.shape[1]), lambda j: (j, 0)))
        chunk_out.append(pl.BlockSpec((rows_w // 2, w.shape[1]), lambda j: (j, 0)))
    wax_shape = w_a.shape[:-1] + (w_a.shape[-1] + w_x.shape[-1],)
    return pl.pallas_call(
        _kv_kernel,
        grid=(n_steps,),
        in_specs=[_const_spec(mem2d.shape, True), wspec, wspec,
                  _const_spec(w_a.shape), _const_spec(w_x.shape), _const_spec(w_pool.shape)] + chunk_in,
        out_specs=[ospec, ospec] + chunk_out + [_const_spec(wax_shape), _const_spec(w_pool.shape)],
        out_shape=[jax.ShapeDtypeStruct((M // 2, D), jnp.uint32)] * 2
        + [jax.ShapeDtypeStruct((w.shape[0] // 2, w.shape[1]), jnp.uint32) for w in cast_weights]
        + [jax.ShapeDtypeStruct(wax_shape, _BF16), jax.ShapeDtypeStruct(w_pool.shape, _BF16)],
        scratch_shapes=[pltpu.VMEM((M, D), _BF16)],
        compiler_params=pltpu.CompilerParams(
            dimension_semantics=("arbitrary",), vmem_limit_bytes=VMEM_LIMIT_BYTES),
        name="kv_proj",
    )(mem2d, w_k, w_v, w_a, w_x, w_pool, *cast_weights)


def _xattn_kernel(x_ref, xn_ref, k_ref, v_ref, wq_ref, wo_ref, g_ref, b_ref, *rest):
    n_cast = (len(rest) - 3) // 2
    cast_in, o_ref, cast_out = rest[:n_cast], rest[n_cast], rest[n_cast + 1:2 * n_cast + 1]
    resid_ref, q_ref = rest[2 * n_cast + 1:]
    i = pl.program_id(0)
    n_tiles = pl.num_programs(0) - 1
    slot = i % 2
    T = x_ref.shape[1]
    scale = XATTN_HEAD_DIM ** -0.5
    heads = [slice(h * XATTN_HEAD_DIM, (h + 1) * XATTN_HEAD_DIM) for h in range(XATTN_HEADS)]

    def query(xb, h):
        return _dot(xb, _unpack_rows(wq_ref[:, heads[h]])).astype(_BF16)

    @pl.when(i == 0)
    def _():
        resid_ref[1] = jnp.zeros(resid_ref.shape[1:], _F32)
        xb0 = x_ref[0].astype(_BF16)
        for h in range(XATTN_HEADS):
            q_ref[:, heads[h]] = query(xb0, h)

    def finish_previous():
        return _normalize_previous(resid_ref, slot, g_ref, b_ref, o_ref)

    @pl.when(i < n_tiles)
    def _():
        zeros = finish_previous()
        per_head = len(zeros) // XATTN_HEADS
        for src, dst in zip(cast_in, cast_out):
            dst[...] = _pack_rows(src[...])
        x = x_ref[0]
        xnb = xn_ref[0].astype(_BF16)

        def probs(h):
            sc = lax.dot_general(q_ref[:, heads[h]], _unpack_rows(k_ref[0, :, heads[h]]),
                                 (((1,), (1,)), ((), ())), preferred_element_type=_F32) * scale
            sc = sc + _tile_zeros(zeros[h * per_head:(h + 1) * per_head], T, N_MEM)
            e = jnp.exp(sc - jnp.max(sc, axis=-1, keepdims=True))
            return (e * (1.0 / jnp.sum(e, axis=-1, keepdims=True))).astype(_BF16)

        def attend(p, h):
            return _dot(p, _unpack_rows(v_ref[0, :, heads[h]])).astype(_BF16)

        o_parts = []
        for h in range(XATTN_HEADS):
            p = probs(h)
            q_ref[:, heads[h]] = query(xnb, h)
            o_parts.append(attend(p, h))
        y = _dot(jnp.concatenate(o_parts, axis=1), _unpack_rows(wo_ref[...]))
        resid_ref[slot] = DEEPNORM_ALPHA * x + y

    @pl.when(i == n_tiles)
    def _():
        finish_previous()


def _xattn(x, k, v, w_q, w_o, g, b, cast_weights):
    B, S, D = x.shape
    T = XATTN_TILE
    n_seq = S // T
    n_tiles = B * n_seq
    chunk_in, chunk_out = [], []
    for w in cast_weights:
        assert w.shape[0] % (2 * SUBLANES * n_tiles) == 0
        rows_w = w.shape[0] // n_tiles
        chunk_in.append(pl.BlockSpec((rows_w, w.shape[1]), lambda i: (jnp.minimum(i, n_tiles - 1), 0)))
        chunk_out.append(pl.BlockSpec((rows_w // 2, w.shape[1]), lambda i: (jnp.minimum(i, n_tiles - 1), 0)))
    assert T % (LN_CHUNKS * SUBLANES) == 0 and LN_CHUNKS % XATTN_HEADS == 0

    def tile(t):
        t = jnp.clip(t, 0, n_tiles - 1)
        return t // n_seq, t % n_seq

    kvspec = pl.BlockSpec((1, N_MEM // 2, D), lambda i: (tile(i)[0], 0, 0))
    return pl.pallas_call(
        _xattn_kernel,
        grid=(n_tiles + 1,),
        in_specs=[pl.BlockSpec((1, T, D), lambda i: (*tile(i), 0)),
                  pl.BlockSpec((1, T, D), lambda i: (*tile(i + 1), 0)),
                  kvspec, kvspec,
                  _const_spec(w_q.shape, True), _const_spec(w_o.shape, True),
                  _const_spec(g.shape), _const_spec(b.shape)] + chunk_in,
        out_specs=[pl.BlockSpec((1, T, D), lambda i: (*tile(i - 1), 0))] + chunk_out,
        out_shape=[jax.ShapeDtypeStruct(x.shape, _F32)]
        + [jax.ShapeDtypeStruct((w.shape[0] // 2, w.shape[1]), jnp.uint32) for w in cast_weights],
        scratch_shapes=[pltpu.VMEM((2, T, D), _F32), pltpu.VMEM((T, D), _BF16)],
        compiler_params=pltpu.CompilerParams(
            dimension_semantics=("arbitrary",), vmem_limit_bytes=VMEM_LIMIT_BYTES),
        name="xattn",
    )(x, x, k, v, w_q, w_o, g, b, *cast_weights)


def _mlp_kernel(x_hbm, w1_ref, w2_ref, g_ref, b_ref, o_hbm, xstage, xb_ref, acc_ref, oslice, x_sem, o_sem,
                *, n_f, halves):
    i, j = pl.program_id(0), pl.program_id(1)
    n_tiles = pl.num_programs(0) - 1
    slot = i % 2
    tm = xstage.shape[0]
    rows = tm // n_f
    q = i * n_f + j
    k = q % 2

    def x_copy(t):
        return pltpu.make_async_copy(x_hbm.at[pl.ds(pl.multiple_of(t * tm, tm), tm), :], xstage, x_sem.at[0])

    def out_copy(step, buf):
        r0 = pl.multiple_of((step // n_f - 1) * tm + (step % n_f) * rows, rows)
        return pltpu.make_async_copy(oslice.at[buf], o_hbm.at[pl.ds(r0, rows), :], o_sem.at[buf])

    @pl.when(q == 0)
    def _():
        x_copy(0).start()
        acc_ref[1] = jnp.zeros(acc_ref.shape[1:], _F32)

    @pl.when((i < n_tiles) & (j == 0))
    def _():
        x_copy(i).wait()
        xb_ref[...] = xstage[...].astype(_BF16)

    @pl.when(q >= n_f + 2)
    def _():
        out_copy(q - 2, k).wait()

    def finish_previous_rows():
        r0 = pl.multiple_of(j * rows, rows)
        oslice[k] = _layer_norm(acc_ref[1 - slot, pl.ds(r0, rows), :], g_ref[...], b_ref[...])

    def accumulate(first):
        finish_previous_rows()
        w1 = _unpack_rows(w1_ref[...])
        w2 = _unpack_rows(w2_ref[...])
        half = tm // halves
        hs = [jnp.square(jnp.maximum(_dot(xb_ref[r * half:(r + 1) * half, :], w1), 0.0)).astype(_BF16)
              for r in range(halves)]
        for r in range(halves):
            rs = slice(r * half, (r + 1) * half)
            prior = DEEPNORM_ALPHA * xstage[rs, :] if first else acc_ref[slot, rs, :]
            acc_ref[slot, rs, :] = prior + _dot(hs[r], w2)

    @pl.when((i < n_tiles) & (j == 0))
    def _():
        accumulate(first=True)

    @pl.when((i < n_tiles) & (j > 0))
    def _():
        accumulate(first=False)

    @pl.when((i + 1 < n_tiles) & (j == 0))
    def _():
        x_copy(i + 1).start()

    @pl.when(i == n_tiles)
    def _():
        finish_previous_rows()

    @pl.when(q >= n_f)
    def _():
        out_copy(q, k).start()

    @pl.when(q == (n_tiles + 1) * n_f - 1)
    def _():
        out_copy(q - 1, 1 - k).wait()
        out_copy(q, k).wait()


def _mlp(x2d, w1, w2, g, b):
    M, D = x2d.shape
    F = w1.shape[1]
    tm, tf = MLP_TILE_M, MLP_TILE_F
    n_tiles, n_f = M // tm, F // tf
    assert tm % (n_f * SUBLANES) == 0 and tm % MLP_ROW_HALVES == 0 and n_f >= 2

    def wblk(i, j):
        return jnp.where(i == n_tiles, n_f - 1, j)

    return pl.pallas_call(
        functools.partial(_mlp_kernel, n_f=n_f, halves=MLP_ROW_HALVES),
        grid=(n_tiles + 1, n_f),
        in_specs=[pl.BlockSpec(memory_space=pl.ANY),
                  pl.BlockSpec((D // 2, tf), lambda i, j: (0, wblk(i, j))),
                  pl.BlockSpec((tf // 2, D), lambda i, j: (wblk(i, j), 0)),
                  _const_spec(g.shape), _const_spec(b.shape)],
        out_specs=pl.BlockSpec(memory_space=pl.ANY),
        out_shape=jax.ShapeDtypeStruct((M, D), _F32),
        scratch_shapes=[pltpu.VMEM((tm, D), _F32),
                        pltpu.VMEM((tm, D), _BF16),
                        pltpu.VMEM((2, tm, D), _F32),
                        pltpu.VMEM((2, tm // n_f, D), _F32),
                        pltpu.SemaphoreType.DMA((1,)),
                        pltpu.SemaphoreType.DMA((2,))],
        compiler_params=pltpu.CompilerParams(
            dimension_semantics=("arbitrary", "arbitrary"),
            vmem_limit_bytes=VMEM_LIMIT_BYTES),
        name="mlp",
    )(x2d, w1, w2, g, b)


def kernel(x, mem, w_in, conv_w, conv_b, w_a, b_a, w_x, b_x, lru_lambda, w_pool, b_pool, pool_scale,
           w_out, ln1_g, ln1_b, w_q, w_k, w_v, w_o, ln2_g, ln2_b, w_ff1, w_ff2, ln3_g, ln3_b):
    B, S, D = x.shape
    row = lambda p: p.reshape(1, -1)
    for l in range(DEPTH):
        k, v, w_in_b, w_out_b, w_ax, w_pool_b = _kv_proj(
            mem.reshape(B * N_MEM, D), w_k[l], w_v[l], w_a[l], w_x[l], w_pool[l], (w_in[l], w_out[l]))
        x, wq_b, wo_b = _mixer(
            x, w_in_b, conv_w[l], row(conv_b[l]), w_ax,
            row(b_a[l]), row(b_x[l]), row(lru_lambda[l]),
            w_pool_b, row(b_pool[l]), row(pool_scale[l]),
            w_out_b, row(ln1_g[l]), row(ln1_b[l]),
            (w_q[l], w_o[l]))
        x, w1_b, w2_b = _xattn(x, k.reshape(B, N_MEM // 2, D), v.reshape(B, N_MEM // 2, D),
                               wq_b, wo_b, row(ln2_g[l]), row(ln2_b[l]), (w_ff1[l], w_ff2[l]))
        x = _mlp(x.reshape(B * S, D), w1_b, w2_b, row(ln3_g[l]), row(ln3_b[l])).reshape(B, S, D)
    return x
```

```python
import functools

import jax
import jax.numpy as jnp
from jax import lax
from jax.experimental import pallas as pl
from jax.experimental.pallas import tpu as pltpu

D_MODEL = 2048
POOL_WIDTH = 1024
LRU_WIDTH = 1024
POOL_WINDOWS = (2, 4, 8, 16)
POOL_GROUP = 256
LRU_HEADS = 8
LRU_HEAD_DIM = 128
CONV_WIDTH = 4
LRU_C = 8.0
N_MEM = 256
XATTN_HEADS = 4
XATTN_HEAD_DIM = 512
LN_EPS = 1e-5
DEPTH = 1
DEEPNORM_ALPHA = (2.0 * DEPTH) ** 0.25

POOL_HIST = 16
SUBLANES = 8
LANES = 128

MIXER_TILE = 512
XATTN_TILE = 256
LN_CHUNKS = 4
MLP_TILE_M = 1024
MLP_ROW_HALVES = 2
MLP_TILE_F = 1024
KV_TILE_N = 256
VMEM_LIMIT_BYTES = 56 * 1024 * 1024
MIXER_VMEM_LIMIT_BYTES = 60 * 1024 * 1024

_F32 = jnp.float32
_BF16 = jnp.bfloat16


def _layer_norm(v, g, b):
    mu = jnp.mean(v, axis=-1, keepdims=True)
    c = v - mu
    var = jnp.mean(jnp.square(c), axis=-1, keepdims=True)
    return c * lax.rsqrt(var + LN_EPS) * g + b


def _dot(a, b):
    return jnp.dot(a, b, preferred_element_type=_F32)


def _pack_rows(v):
    return pltpu.bitcast(v.astype(_BF16), jnp.uint32)


def _unpack_rows(w):
    return pltpu.bitcast(w, _BF16)


def _ordering_zero(v):
    rows, cols = v.shape
    t = jnp.sum(v.reshape(rows // SUBLANES, SUBLANES, cols), axis=0)
    t = functools.reduce(lambda a, b: a + b, [t[:, c:c + LANES] for c in range(0, cols, LANES)])
    return jnp.minimum(jnp.abs(t), 0.0)


def _normalize_previous(resid_ref, slot, g_ref, b_ref, o_ref):
    T = resid_ref.shape[1]
    chunk = T // LN_CHUNKS
    zeros = []
    for r0 in range(0, T, chunk):
        out = _layer_norm(resid_ref[1 - slot, r0:r0 + chunk, :], g_ref[...], b_ref[...])
        o_ref[0, r0:r0 + chunk, :] = out
        zeros.append(_ordering_zero(out))
    return zeros


def _tile_zeros(zeros, rows, cols):
    width = cols // len(zeros)
    return jnp.concatenate([jnp.tile(z, (rows // SUBLANES, width // LANES)) for z in zeros], axis=1)


def _history_rows(cur_tail, prev_tail):
    sub = lax.broadcasted_iota(jnp.int32, (1,) + cur_tail.shape[1:], 1)
    return jnp.where(sub == 0, pltpu.roll(prev_tail, 1, 1), pltpu.roll(cur_tail, 1, 1))


def _tile_copies(hbm_ref, buf, sems, t, sl, n_seq, to_hbm):
    lc = buf.shape[1]
    b, s = t // n_seq, t % n_seq
    copies = []
    for j in range(SUBLANES):
        rows = hbm_ref.at[b, pl.ds(pl.multiple_of((s * SUBLANES + j) * lc, lc), lc), :]
        chunk = buf.at[sl, :, j, :]
        src, dst = (chunk, rows) if to_hbm else (rows, chunk)
        copies.append(pltpu.make_async_copy(src, dst, sems.at[sl]))
    return copies


def _mixer_kernel(x_hbm, w_in_ref, conv_w_ref, conv_b_ref, w_ax_ref, b_a_ref, b_x_ref, lam_ref,
                  w_pool_ref, b_pool_ref, pool_scale_ref, w_out_ref, g_ref, b_ref, *rest, n_seq):
    n_cast = (len(rest) - 9) // 2
    cast_in, o_hbm, cast_out = rest[:n_cast], rest[n_cast], rest[n_cast + 1:2 * n_cast + 1]
    xbuf, obuf, in_sem, out_sem, pool_hist, conv_hist, h_carry, resid_ref = rest[2 * n_cast + 1:]
    i = pl.program_id(0)
    n_tiles = pl.num_programs(0) - 1
    s = jnp.minimum(i, n_tiles - 1) % n_seq
    slot = i % 2

    def x_copies(t, sl):
        return _tile_copies(x_hbm, xbuf, in_sem, t, sl, n_seq, to_hbm=False)

    def out_copies(t, sl):
        return _tile_copies(o_hbm, obuf, out_sem, t, sl, n_seq, to_hbm=True)

    @pl.when(i == 0)
    def _():
        for c in x_copies(0, 0):
            c.start()
        resid_ref[1] = jnp.zeros(resid_ref.shape[1:], _F32)

    @pl.when(i < n_tiles)
    def _():
        for c in x_copies(i, slot):
            c.wait()

    @pl.when(i + 1 < n_tiles)
    def _():
        for c in x_copies(i + 1, 1 - slot):
            c.start()

    @pl.when(i >= 3)
    def _():
        for c in out_copies(i - 3, 1 - slot):
            c.wait()

    def finish_previous():
        T, D = resid_ref.shape[1:]
        chunk = T // LN_CHUNKS
        zeros = []
        for r0 in range(0, T, chunk):
            out = _layer_norm(resid_ref[1 - slot, r0:r0 + chunk, :], g_ref[...], b_ref[...])
            obuf[1 - slot, r0 // SUBLANES:(r0 + chunk) // SUBLANES] = out.reshape(chunk // SUBLANES, SUBLANES, D)
            zeros.append(_ordering_zero(out))
        return zeros

    @pl.when(i == n_tiles)
    def _():
        finish_previous()

    @pl.when(i < n_tiles)
    def _():
        _mixer_tile(s, slot, finish_previous, xbuf, w_in_ref, conv_w_ref, conv_b_ref, w_ax_ref, b_a_ref, b_x_ref,
                    lam_ref, w_pool_ref, b_pool_ref, pool_scale_ref, w_out_ref, cast_in, cast_out,
                    pool_hist, conv_hist, h_carry, resid_ref)

    @pl.when(i >= 1)
    def _():
        for c in out_copies(i - 1, 1 - slot):
            c.start()

    @pl.when(i == n_tiles)
    def _():
        for c in out_copies(n_tiles - 2, slot) + out_copies(n_tiles - 1, 1 - slot):
            c.wait()


def _mixer_tile(s, slot, finish_previous, xbuf, w_in_ref, conv_w_ref, conv_b_ref, w_ax_ref, b_a_ref, b_x_ref,
                lam_ref, w_pool_ref, b_pool_ref, pool_scale_ref, w_out_ref, cast_in, cast_out,
                pool_hist, conv_hist, h_carry, resid_ref):
    lc, _, D = xbuf.shape[1:]
    T = lc * SUBLANES

    @pl.when(s == 0)
    def _():
        pool_hist[...] = jnp.zeros_like(pool_hist)
        conv_hist[...] = jnp.zeros_like(conv_hist)
        h_carry[...] = jnp.zeros_like(h_carry)

    zeros = finish_previous()
    for src, dst in zip(cast_in, cast_out):
        dst[...] = _pack_rows(src[...])

    x = xbuf[slot].reshape(T, D)
    xb = x.astype(_BF16)
    half = len(zeros) // 2
    up = _dot(xb, _unpack_rows(w_in_ref[:, :POOL_WIDTH])) + _tile_zeros(zeros[:half], T, POOL_WIDTH)
    up = up.reshape(lc, SUBLANES, POOL_WIDTH)
    u_lru = (_dot(xb, _unpack_rows(w_in_ref[:, POOL_WIDTH:POOL_WIDTH + LRU_WIDTH]))
             + _tile_zeros(zeros[half:], T, LRU_WIDTH)).reshape(lc, SUBLANES, LRU_WIDTH)

    kk = lax.broadcasted_iota(jnp.int32, (POOL_HIST, SUBLANES, LANES), 0)
    jj = lax.broadcasted_iota(jnp.int32, (POOL_HIST, SUBLANES, LANES), 1)
    t_head = s * T + jj * lc + kk

    ext = jnp.concatenate([_history_rows(up[lc - POOL_HIST:], pool_hist[...]), up], axis=0)
    pool_hist[...] = up[lc - POOL_HIST:]
    s2 = ext[1:] + ext[:-1]
    s2r = s2[:, :, POOL_GROUP:]
    s4 = s2r[2:] + s2r[:-2]
    s4r = s4[:, :, POOL_GROUP:]
    s8 = s4r[4:] + s4r[:-4]
    s8r = s8[:, :, POOL_GROUP:]
    s16 = s8r[8:] + s8r[:-8]
    sums = (s2[15:, :, :POOL_GROUP], s4[13:, :, :POOL_GROUP], s8[9:, :, :POOL_GROUP], s16[1:])
    y_pool = []
    for g, w in enumerate(POOL_WINDOWS):
        cs = slice(g * POOL_GROUP, (g + 1) * POOL_GROUP)
        inv_head = 1.0 / jnp.minimum(t_head + 1, w).astype(_F32)
        inv_head = jnp.concatenate([inv_head] * (POOL_GROUP // LANES), axis=-1)
        mean = jnp.concatenate([sums[g][:POOL_HIST] * inv_head, sums[g][POOL_HIST:] * (1.0 / w)], axis=0)
        mixed = mean - up[:, :, cs]
        yg = _dot(mixed.reshape(T, POOL_GROUP).astype(_BF16), w_pool_ref[g])
        y_pool.append(((yg + b_pool_ref[:, cs]) * pool_scale_ref[:, cs]).astype(_BF16))
    z_pool = jnp.concatenate(y_pool, axis=1)

    n_hist = CONV_WIDTH - 1
    sub = lax.broadcasted_iota(jnp.int32, (SUBLANES, LRU_HEAD_DIM), 0)
    first = t_head[0:1] == 0
    gate_cols = 2 * LRU_HEAD_DIM
    out_cols = D_MODEL // LRU_HEADS
    y_top, z_lru = [], []
    for h in range(LRU_HEADS):
        hs = slice(h * LRU_HEAD_DIM, (h + 1) * LRU_HEAD_DIM)
        ul = u_lru[:, :, hs]
        ext2 = jnp.concatenate([_history_rows(ul[lc - n_hist:], conv_hist[:, :, hs]), ul], axis=0)
        conv_hist[:, :, hs] = ul[lc - n_hist:]
        xc = ext2[0:lc] * conv_w_ref[0:1, hs]
        for k in range(1, CONV_WIDTH):
            xc = xc + ext2[k:k + lc] * conv_w_ref[k:k + 1, hs]
        xc = xc + conv_b_ref[:, hs]

        pre = _dot(xc.reshape(T, LRU_HEAD_DIM).astype(_BF16), w_ax_ref[h])
        pre = pre.reshape(lc, SUBLANES, 2 * LRU_HEAD_DIM)
        r = jax.nn.sigmoid(pre[:, :, :LRU_HEAD_DIM] + b_a_ref[:, hs])
        i = jax.nn.sigmoid(pre[:, :, LRU_HEAD_DIM:] + b_x_ref[:, hs])

        if h % 2 == 0:
            c0 = POOL_WIDTH + LRU_WIDTH + h * LRU_HEAD_DIM
            u_gate = _dot(xb, _unpack_rows(w_in_ref[:, c0:c0 + gate_cols])).reshape(lc, SUBLANES, gate_cols)
        else:
            c0 = (h // 2) * out_cols
            y_top.append(_dot(z_pool, _unpack_rows(w_out_ref[:POOL_WIDTH // 2, c0:c0 + out_cols])))

        log_a = (-LRU_C * r) * jax.nn.softplus(-lam_ref[:, hs])
        a = jnp.exp(log_a)
        u = -jnp.tanh(log_a) * (a * a + 1.0)
        mult = jnp.where(u == 0.0, 0.0, u * lax.rsqrt(u))
        mult = jnp.concatenate([jnp.where(first, 1.0, mult[0:1]), mult[1:]], axis=0)
        bv = mult * (i * xc)

        hl, ac = [bv[0]], [a[0]]
        for k in range(1, lc):
            hl.append(a[k] * hl[-1] + bv[k])
            ac.append(a[k] * ac[-1])
        c_a, c_b = ac[-1], hl[-1]
        for d in (1, 2, 4):
            a_s = jnp.where(sub < d, 1.0, pltpu.roll(c_a, d, 0))
            b_s = jnp.where(sub < d, 0.0, pltpu.roll(c_b, d, 0))
            c_b = c_a * b_s + c_b
            c_a = c_a * a_s
        h_prev = h_carry[:, hs]
        h_end = c_b + c_a * h_prev
        h_in = jnp.where(sub == 0, h_prev, pltpu.roll(h_end, 1, 0))
        h_carry[:, hs] = jnp.broadcast_to(h_end[SUBLANES - 1:SUBLANES, :], (SUBLANES, LRU_HEAD_DIM))
        hseq = jnp.stack(hl, axis=0) + jnp.stack(ac, axis=0) * h_in
        gate = jax.nn.gelu(u_gate[:, :, (h % 2) * LRU_HEAD_DIM:(h % 2 + 1) * LRU_HEAD_DIM])
        z_lru.append((hseq * gate).reshape(T, LRU_HEAD_DIM).astype(_BF16))

    for q in range(LRU_HEADS // 2, LRU_HEADS):
        y_top.append(_dot(z_pool, _unpack_rows(w_out_ref[:POOL_WIDTH // 2, q * out_cols:(q + 1) * out_cols])))
    y = (jnp.concatenate(y_top, axis=1)
         + _dot(jnp.concatenate(z_lru, axis=1), _unpack_rows(w_out_ref[POOL_WIDTH // 2:, :])))
    resid_ref[slot] = DEEPNORM_ALPHA * x + y


def _const_spec(shape, single_buffer=False):
    nd = len(shape)
    kwargs = {"pipeline_mode": pl.Buffered(1)} if single_buffer else {}
    return pl.BlockSpec(shape, lambda *_: (0,) * nd, **kwargs)


def _mixer(x, w_in, conv_w, conv_b, w_ax, b_a, b_x, lam, w_pool, b_pool, pool_scale, w_out, g, b,
           cast_weights):
    B, S, D = x.shape
    T = MIXER_TILE
    n_seq = S // T
    n_steps = B * n_seq
    lc = T // SUBLANES
    assert T % SUBLANES == 0 and lc >= POOL_HIST and lc % (2 * SUBLANES) == 0 and n_steps >= 3
    assert T % (LN_CHUNKS * SUBLANES) == 0 and LN_CHUNKS % 2 == 0

    chunk_in, chunk_out = [], []
    for w in cast_weights:
        assert w.shape[0] % (2 * SUBLANES * n_steps) == 0
        rows_w = w.shape[0] // n_steps
        chunk_in.append(pl.BlockSpec((rows_w, w.shape[1]), lambda i: (jnp.minimum(i, n_steps - 1), 0)))
        chunk_out.append(pl.BlockSpec((rows_w // 2, w.shape[1]), lambda i: (jnp.minimum(i, n_steps - 1), 0)))
    return pl.pallas_call(
        functools.partial(_mixer_kernel, n_seq=n_seq),
        grid=(n_steps + 1,),
        in_specs=[
            pl.BlockSpec(memory_space=pl.ANY),
            _const_spec(w_in.shape, True),
            _const_spec(conv_w.shape), _const_spec(conv_b.shape),
            _const_spec(w_ax.shape, True),
            _const_spec(b_a.shape), _const_spec(b_x.shape), _const_spec(lam.shape),
            _const_spec(w_pool.shape, True),
            _const_spec(b_pool.shape), _const_spec(pool_scale.shape),
            _const_spec(w_out.shape, True),
            _const_spec(g.shape), _const_spec(b.shape),
        ] + chunk_in,
        out_specs=[pl.BlockSpec(memory_space=pl.ANY)] + chunk_out,
        out_shape=[jax.ShapeDtypeStruct(x.shape, _F32)]
        + [jax.ShapeDtypeStruct((w.shape[0] // 2, w.shape[1]), jnp.uint32) for w in cast_weights],
        scratch_shapes=[
            pltpu.VMEM((2, lc, SUBLANES, D), _F32),
            pltpu.VMEM((2, lc, SUBLANES, D), _F32),
            pltpu.SemaphoreType.DMA((2,)),
            pltpu.SemaphoreType.DMA((2,)),
            pltpu.VMEM((POOL_HIST, SUBLANES, POOL_WIDTH), _F32),
            pltpu.VMEM((CONV_WIDTH - 1, SUBLANES, LRU_WIDTH), _F32),
            pltpu.VMEM((SUBLANES, LRU_WIDTH), _F32),
            pltpu.VMEM((2, T, D), _F32),
        ],
        compiler_params=pltpu.CompilerParams(
            dimension_semantics=("arbitrary",), vmem_limit_bytes=MIXER_VMEM_LIMIT_BYTES),
        name="mixer",
    )(x, w_in, conv_w, conv_b, w_ax, b_a, b_x, lam, w_pool, b_pool, pool_scale, w_out, g, b, *cast_weights)


def _kv_kernel(mem_ref, wk_ref, wv_ref, wa_ref, wx_ref, wpool_ref, *rest):
    n_cast = (len(rest) - 5) // 2
    cast_in, k_ref, v_ref = rest[:n_cast], rest[n_cast], rest[n_cast + 1]
    cast_out = rest[n_cast + 2:2 * n_cast + 2]
    wax_ref, wpool_b_ref, mb_ref = rest[2 * n_cast + 2:]
    for src, dst in zip(cast_in, cast_out):
        dst[...] = _pack_rows(src[...])

    @pl.when(pl.program_id(0) == 0)
    def _():
        mb_ref[...] = mem_ref[...].astype(_BF16)
        wax_ref[...] = jnp.concatenate([wa_ref[...], wx_ref[...]], axis=-1).astype(_BF16)
        wpool_b_ref[...] = wpool_ref[...].astype(_BF16)

    k_ref[...] = _pack_rows(_dot(mb_ref[...], wk_ref[...].astype(_BF16)))
    v_ref[...] = _pack_rows(_dot(mb_ref[...], wv_ref[...].astype(_BF16)))


def _kv_proj(mem2d, w_k, w_v, w_a, w_x, w_pool, cast_weights):
    M, D = mem2d.shape
    tn = KV_TILE_N
    n_steps = D // tn
    wspec = pl.BlockSpec((D, tn), lambda j: (0, j))
    ospec = pl.BlockSpec((M // 2, tn), lambda j: (0, j))
    chunk_in, chunk_out = [], []
    for w in cast_weights:
        assert w.shape[0] % (2 * SUBLANES * n_steps) == 0
        rows_w = w.shape[0] // n_steps
        chunk_in.append(pl.BlockSpec((rows_w, w.shape[1]), lambda j: (j, 0)))
        chunk_out.append(pl.BlockSpec((rows_w // 2, w.shape[1]), lambda j: (j, 0)))
    wax_shape = w_a.shape[:-1] + (w_a.shape[-1] + w_x.shape[-1],)
    return pl.pallas_call(
        _kv_kernel,
        grid=(n_steps,),
        in_specs=[_const_spec(mem2d.shape, True), wspec, wspec,
                  _const_spec(w_a.shape), _const_spec(w_x.shape), _const_spec(w_pool.shape)] + chunk_in,
        out_specs=[ospec, ospec] + chunk_out + [_const_spec(wax_shape), _const_spec(w_pool.shape)],
        out_shape=[jax.ShapeDtypeStruct((M // 2, D), jnp.uint32)] * 2
        + [jax.ShapeDtypeStruct((w.shape[0] // 2, w.shape[1]), jnp.uint32) for w in cast_weights]
        + [jax.ShapeDtypeStruct(wax_shape, _BF16), jax.ShapeDtypeStruct(w_pool.shape, _BF16)],
        scratch_shapes=[pltpu.VMEM((M, D), _BF16)],
        compiler_params=pltpu.CompilerParams(
            dimension_semantics=("arbitrary",), vmem_limit_bytes=VMEM_LIMIT_BYTES),
        name="kv_proj",
    )(mem2d, w_k, w_v, w_a, w_x, w_pool, *cast_weights)


def _xattn_kernel(x_ref, xn_ref, k_ref, v_ref, wq_ref, wo_ref, g_ref, b_ref, *rest):
    n_cast = (len(rest) - 3) // 2
    cast_in, o_ref, cast_out = rest[:n_cast], rest[n_cast], rest[n_cast + 1:2 * n_cast + 1]
    resid_ref, q_ref = rest[2 * n_cast + 1:]
    i = pl.program_id(0)
    n_tiles = pl.num_programs(0) - 1
    slot = i % 2
    T = x_ref.shape[1]
    scale = XATTN_HEAD_DIM ** -0.5
    heads = [slice(h * XATTN_HEAD_DIM, (h + 1) * XATTN_HEAD_DIM) for h in range(XATTN_HEADS)]

    def query(xb, h):
        return _dot(xb, _unpack_rows(wq_ref[:, heads[h]])).astype(_BF16)

    @pl.when(i == 0)
    def _():
        resid_ref[1] = jnp.zeros(resid_ref.shape[1:], _F32)
        xb0 = x_ref[0].astype(_BF16)
        for h in range(XATTN_HEADS):
            q_ref[:, heads[h]] = query(xb0, h)

    def finish_previous():
        return _normalize_previous(resid_ref, slot, g_ref, b_ref, o_ref)

    @pl.when(i < n_tiles)
    def _():
        zeros = finish_previous()
        per_head = len(zeros) // XATTN_HEADS
        for src, dst in zip(cast_in, cast_out):
            dst[...] = _pack_rows(src[...])
        x = x_ref[0]
        xnb = xn_ref[0].astype(_BF16)

        def probs(h):
            sc = lax.dot_general(q_ref[:, heads[h]], _unpack_rows(k_ref[0, :, heads[h]]),
                                 (((1,), (1,)), ((), ())), preferred_element_type=_F32) * scale
            sc = sc + _tile_zeros(zeros[h * per_head:(h + 1) * per_head], T, N_MEM)
            e = jnp.exp(sc - jnp.max(sc, axis=-1, keepdims=True))
            return (e * (1.0 / jnp.sum(e, axis=-1, keepdims=True))).astype(_BF16)

        def attend(p, h):
            return _dot(p, _unpack_rows(v_ref[0, :, heads[h]])).astype(_BF16)

        o_parts = []
        for h in range(XATTN_HEADS):
            p = probs(h)
            q_ref[:, heads[h]] = query(xnb, h)
            o_parts.append(attend(p, h))
        y = _dot(jnp.concatenate(o_parts, axis=1), _unpack_rows(wo_ref[...]))
        resid_ref[slot] = DEEPNORM_ALPHA * x + y

    @pl.when(i == n_tiles)
    def _():
        finish_previous()


def _xattn(x, k, v, w_q, w_o, g, b, cast_weights):
    B, S, D = x.shape
    T = XATTN_TILE
    n_seq = S // T
    n_tiles = B * n_seq
    chunk_in, chunk_out = [], []
    for w in cast_weights:
        assert w.shape[0] % (2 * SUBLANES * n_tiles) == 0
        rows_w = w.shape[0] // n_tiles
        chunk_in.append(pl.BlockSpec((rows_w, w.shape[1]), lambda i: (jnp.minimum(i, n_tiles - 1), 0)))
        chunk_out.append(pl.BlockSpec((rows_w // 2, w.shape[1]), lambda i: (jnp.minimum(i, n_tiles - 1), 0)))
    assert T % (LN_CHUNKS * SUBLANES) == 0 and LN_CHUNKS % XATTN_HEADS == 0

    def tile(t):
        t = jnp.clip(t, 0, n_tiles - 1)
        return t // n_seq, t % n_seq

    kvspec = pl.BlockSpec((1, N_MEM // 2, D), lambda i: (tile(i)[0], 0, 0))
    return pl.pallas_call(
        _xattn_kernel,
        grid=(n_tiles + 1,),
        in_specs=[pl.BlockSpec((1, T, D), lambda i: (*tile(i), 0)),
                  pl.BlockSpec((1, T, D), lambda i: (*tile(i + 1), 0)),
                  kvspec, kvspec,
                  _const_spec(w_q.shape, True), _const_spec(w_o.shape, True),
                  _const_spec(g.shape), _const_spec(b.shape)] + chunk_in,
        out_specs=[pl.BlockSpec((1, T, D), lambda i: (*tile(i - 1), 0))] + chunk_out,
        out_shape=[jax.ShapeDtypeStruct(x.shape, _F32)]
        + [jax.ShapeDtypeStruct((w.shape[0] // 2, w.shape[1]), jnp.uint32) for w in cast_weights],
        scratch_shapes=[pltpu.VMEM((2, T, D), _F32), pltpu.VMEM((T, D), _BF16)],
        compiler_params=pltpu.CompilerParams(
            dimension_semantics=("arbitrary",), vmem_limit_bytes=VMEM_LIMIT_BYTES),
        name="xattn",
    )(x, x, k, v, w_q, w_o, g, b, *cast_weights)


def _mlp_kernel(x_hbm, w1_ref, w2_ref, g_ref, b_ref, o_hbm, xstage, xb_ref, acc_ref, oslice, x_sem, o_sem,
                *, n_f, halves):
    i, j = pl.program_id(0), pl.program_id(1)
    n_tiles = pl.num_programs(0) - 1
    slot = i % 2
    tm = xstage.shape[0]
    rows = tm // n_f
    q = i * n_f + j
    k = q % 2

    def x_copy(t):
        return pltpu.make_async_copy(x_hbm.at[pl.ds(pl.multiple_of(t * tm, tm), tm), :], xstage, x_sem.at[0])

    def out_copy(step, buf):
        r0 = pl.multiple_of((step // n_f - 1) * tm + (step % n_f) * rows, rows)
        return pltpu.make_async_copy(oslice.at[buf], o_hbm.at[pl.ds(r0, rows), :], o_sem.at[buf])

    @pl.when(q == 0)
    def _():
        x_copy(0).start()
        acc_ref[1] = jnp.zeros(acc_ref.shape[1:], _F32)

    @pl.when((i < n_tiles) & (j == 0))
    def _():
        x_copy(i).wait()
        xb_ref[...] = xstage[...].astype(_BF16)

    @pl.when(q >= n_f + 2)
    def _():
        out_copy(q - 2, k).wait()

    def finish_previous_rows():
        r0 = pl.multiple_of(j * rows, rows)
        oslice[k] = _layer_norm(acc_ref[1 - slot, pl.ds(r0, rows), :], g_ref[...], b_ref[...])

    def accumulate(first):
        finish_previous_rows()
        w1 = _unpack_rows(w1_ref[...])
        w2 = _unpack_rows(w2_ref[...])
        half = tm // halves
        hs = [jnp.square(jnp.maximum(_dot(xb_ref[r * half:(r + 1) * half, :], w1), 0.0)).astype(_BF16)
              for r in range(halves)]
        for r in range(halves):
            rs = slice(r * half, (r + 1) * half)
            prior = DEEPNORM_ALPHA * xstage[rs, :] if first else acc_ref[slot, rs, :]
            acc_ref[slot, rs, :] = prior + _dot(hs[r], w2)

    @pl.when((i < n_tiles) & (j == 0))
    def _():
        accumulate(first=True)

    @pl.when((i < n_tiles) & (j > 0))
    def _():
        accumulate(first=False)

    @pl.when((i + 1 < n_tiles) & (j == 0))
    def _():
        x_copy(i + 1).start()

    @pl.when(i == n_tiles)
    def _():
        finish_previous_rows()

    @pl.when(q >= n_f)
    def _():
        out_copy(q, k).start()

    @pl.when(q == (n_tiles + 1) * n_f - 1)
    def _():
        out_copy(q - 1, 1 - k).wait()
        out_copy(q, k).wait()


def _mlp(x2d, w1, w2, g, b):
    M, D = x2d.shape
    F = w1.shape[1]
    tm, tf = MLP_TILE_M, MLP_TILE_F
    n_tiles, n_f = M // tm, F // tf
    assert tm % (n_f * SUBLANES) == 0 and tm % MLP_ROW_HALVES == 0 and n_f >= 2

    def wblk(i, j):
        return jnp.where(i == n_tiles, n_f - 1, j)

    return pl.pallas_call(
        functools.partial(_mlp_kernel, n_f=n_f, halves=MLP_ROW_HALVES),
        grid=(n_tiles + 1, n_f),
        in_specs=[pl.BlockSpec(memory_space=pl.ANY),
                  pl.BlockSpec((D // 2, tf), lambda i, j: (0, wblk(i, j))),
                  pl.BlockSpec((tf // 2, D), lambda i, j: (wblk(i, j), 0)),
                  _const_spec(g.shape), _const_spec(b.shape)],
        out_specs=pl.BlockSpec(memory_space=pl.ANY),
        out_shape=jax.ShapeDtypeStruct((M, D), _F32),
        scratch_shapes=[pltpu.VMEM((tm, D), _F32),
                        pltpu.VMEM((tm, D), _BF16),
                        pltpu.VMEM((2, tm, D), _F32),
                        pltpu.VMEM((2, tm // n_f, D), _F32),
                        pltpu.SemaphoreType.DMA((1,)),
                        pltpu.SemaphoreType.DMA((2,))],
        compiler_params=pltpu.CompilerParams(
            dimension_semantics=("arbitrary", "arbitrary"),
            vmem_limit_bytes=VMEM_LIMIT_BYTES),
        name="mlp",
    )(x2d, w1, w2, g, b)


def kernel(x, mem, w_in, conv_w, conv_b, w_a, b_a, w_x, b_x, lru_lambda, w_pool, b_pool, pool_scale,
           w_out, ln1_g, ln1_b, w_q, w_k, w_v, w_o, ln2_g, ln2_b, w_ff1, w_ff2, ln3_g, ln3_b):
    B, S, D = x.shape
    row = lambda p: p.reshape(1, -1)
    for l in range(DEPTH):
        k, v, w_in_b, w_out_b, w_ax, w_pool_b = _kv_proj(
            mem.reshape(B * N_MEM, D), w_k[l], w_v[l], w_a[l], w_x[l], w_pool[l], (w_in[l], w_out[l]))
        x, wq_b, wo_b = _mixer(
            x, w_in_b, conv_w[l], row(conv_b[l]), w_ax,
            row(b_a[l]), row(b_x[l]), row(lru_lambda[l]),
            w_pool_b, row(b_pool[l]), row(pool_scale[l]),
            w_out_b, row(ln1_g[l]), row(ln1_b[l]),
            (w_q[l], w_o[l]))
        x, w1_b, w2_b = _xattn(x, k.reshape(B, N_MEM // 2, D), v.reshape(B, N_MEM // 2, D),
                               wq_b, wo_b, row(ln2_g[l]), row(ln2_b[l]), (w_ff1[l], w_ff2[l]))
        x = _mlp(x.reshape(B * S, D), w1_b, w2_b, row(ln3_g[l]), row(ln3_b[l])).reshape(B, S, D)
    return x
```

```python
import functools

import jax
import jax.numpy as jnp
from jax import lax
from jax.experimental import pallas as pl
from jax.experimental.pallas import tpu as pltpu

D_MODEL = 2048
POOL_WIDTH = 1024
LRU_WIDTH = 1024
POOL_WINDOWS = (2, 4, 8, 16)
POOL_GROUP = 256
LRU_HEADS = 8
LRU_HEAD_DIM = 128
CONV_WIDTH = 4
LRU_C = 8.0
N_MEM = 256
XATTN_HEADS = 4
XATTN_HEAD_DIM = 512
LN_EPS = 1e-5
DEPTH = 1
DEEPNORM_ALPHA = (2.0 * DEPTH) ** 0.25

POOL_HIST = 16
SUBLANES = 8
LANES = 128

MIXER_TILE = 512
XATTN_TILE = 256
LN_CHUNKS = 4
MLP_TILE_M = 1024
MLP_ROW_HALVES = 2
X_CAST_CHUNKS = 4
MLP_TILE_F = 1024
KV_TILE_N = 256
VMEM_LIMIT_BYTES = 56 * 1024 * 1024
MIXER_VMEM_LIMIT_BYTES = 60 * 1024 * 1024

_F32 = jnp.float32
_BF16 = jnp.bfloat16


def _layer_norm(v, g, b):
    mu = jnp.mean(v, axis=-1, keepdims=True)
    c = v - mu
    var = jnp.mean(jnp.square(c), axis=-1, keepdims=True)
    return c * lax.rsqrt(var + LN_EPS) * g + b


def _dot(a, b):
    return jnp.dot(a, b, preferred_element_type=_F32)


def _pack_rows(v):
    return pltpu.bitcast(v.astype(_BF16), jnp.uint32)


def _unpack_rows(w):
    return pltpu.bitcast(w, _BF16)


def _ordering_zero(v):
    rows, cols = v.shape
    t = jnp.sum(v.reshape(rows // SUBLANES, SUBLANES, cols), axis=0)
    t = functools.reduce(lambda a, b: a + b, [t[:, c:c + LANES] for c in range(0, cols, LANES)])
    return jnp.minimum(jnp.abs(t), 0.0)


def _normalize_previous(resid_ref, slot, g_ref, b_ref, o_ref):
    T = resid_ref.shape[1]
    chunk = T // LN_CHUNKS
    zeros = []
    for r0 in range(0, T, chunk):
        out = _layer_norm(resid_ref[1 - slot, r0:r0 + chunk, :], g_ref[...], b_ref[...])
        o_ref[0, r0:r0 + chunk, :] = out
        zeros.append(_ordering_zero(out))
    return zeros


def _tile_zeros(zeros, rows, cols):
    width = cols // len(zeros)
    return jnp.concatenate([jnp.tile(z, (rows // SUBLANES, width // LANES)) for z in zeros], axis=1)


def _history_rows(cur_tail, prev_tail):
    sub = lax.broadcasted_iota(jnp.int32, (1,) + cur_tail.shape[1:], 1)
    return jnp.where(sub == 0, pltpu.roll(prev_tail, 1, 1), pltpu.roll(cur_tail, 1, 1))


def _tile_copies(hbm_ref, buf, sems, t, sl, n_seq, to_hbm):
    lc = buf.shape[1]
    b, s = t // n_seq, t % n_seq
    copies = []
    for j in range(SUBLANES):
        rows = hbm_ref.at[b, pl.ds(pl.multiple_of((s * SUBLANES + j) * lc, lc), lc), :]
        chunk = buf.at[sl, :, j, :]
        src, dst = (chunk, rows) if to_hbm else (rows, chunk)
        copies.append(pltpu.make_async_copy(src, dst, sems.at[sl]))
    return copies


def _mixer_kernel(x_hbm, w_in_ref, conv_w_ref, conv_b_ref, w_ax_ref, b_a_ref, b_x_ref, lam_ref,
                  w_pool_ref, b_pool_ref, pool_scale_ref, w_out_ref, g_ref, b_ref, *rest, n_seq):
    n_cast = (len(rest) - 9) // 2
    cast_in, o_hbm, cast_out = rest[:n_cast], rest[n_cast], rest[n_cast + 1:2 * n_cast + 1]
    xbuf, obuf, in_sem, out_sem, pool_hist, conv_hist, h_carry, resid_ref = rest[2 * n_cast + 1:]
    i = pl.program_id(0)
    n_tiles = pl.num_programs(0) - 1
    s = jnp.minimum(i, n_tiles - 1) % n_seq
    slot = i % 2

    def x_copies(t, sl):
        return _tile_copies(x_hbm, xbuf, in_sem, t, sl, n_seq, to_hbm=False)

    def out_copies(t, sl):
        return _tile_copies(o_hbm, obuf, out_sem, t, sl, n_seq, to_hbm=True)

    @pl.when(i == 0)
    def _():
        for c in x_copies(0, 0):
            c.start()
        resid_ref[1] = jnp.zeros(resid_ref.shape[1:], _F32)

    @pl.when(i < n_tiles)
    def _():
        for c in x_copies(i, slot):
            c.wait()

    @pl.when(i + 1 < n_tiles)
    def _():
        for c in x_copies(i + 1, 1 - slot):
            c.start()

    @pl.when(i >= 3)
    def _():
        for c in out_copies(i - 3, 1 - slot):
            c.wait()

    def finish_previous():
        T, D = resid_ref.shape[1:]
        chunk = T // LN_CHUNKS
        zeros = []
        for r0 in range(0, T, chunk):
            out = _layer_norm(resid_ref[1 - slot, r0:r0 + chunk, :], g_ref[...], b_ref[...])
            obuf[1 - slot, r0 // SUBLANES:(r0 + chunk) // SUBLANES] = out.reshape(chunk // SUBLANES, SUBLANES, D)
            zeros.append(_ordering_zero(out))
        return zeros

    @pl.when(i == n_tiles)
    def _():
        finish_previous()

    @pl.when(i < n_tiles)
    def _():
        _mixer_tile(s, slot, finish_previous, xbuf, w_in_ref, conv_w_ref, conv_b_ref, w_ax_ref, b_a_ref, b_x_ref,
                    lam_ref, w_pool_ref, b_pool_ref, pool_scale_ref, w_out_ref, cast_in, cast_out,
                    pool_hist, conv_hist, h_carry, resid_ref)

    @pl.when(i >= 1)
    def _():
        for c in out_copies(i - 1, 1 - slot):
            c.start()

    @pl.when(i == n_tiles)
    def _():
        for c in out_copies(n_tiles - 2, slot) + out_copies(n_tiles - 1, 1 - slot):
            c.wait()


def _mixer_tile(s, slot, finish_previous, xbuf, w_in_ref, conv_w_ref, conv_b_ref, w_ax_ref, b_a_ref, b_x_ref,
                lam_ref, w_pool_ref, b_pool_ref, pool_scale_ref, w_out_ref, cast_in, cast_out,
                pool_hist, conv_hist, h_carry, resid_ref):
    lc, _, D = xbuf.shape[1:]
    T = lc * SUBLANES

    @pl.when(s == 0)
    def _():
        pool_hist[...] = jnp.zeros_like(pool_hist)
        conv_hist[...] = jnp.zeros_like(conv_hist)
        h_carry[...] = jnp.zeros_like(h_carry)

    zeros = finish_previous()
    for src, dst in zip(cast_in, cast_out):
        dst[...] = _pack_rows(src[...])

    x = xbuf[slot].reshape(T, D)
    xb = x.astype(_BF16)
    half = len(zeros) // 2
    up = _dot(xb, _unpack_rows(w_in_ref[:, :POOL_WIDTH])) + _tile_zeros(zeros[:half], T, POOL_WIDTH)
    up = up.reshape(lc, SUBLANES, POOL_WIDTH)
    u_lru = (_dot(xb, _unpack_rows(w_in_ref[:, POOL_WIDTH:POOL_WIDTH + LRU_WIDTH]))
             + _tile_zeros(zeros[half:], T, LRU_WIDTH)).reshape(lc, SUBLANES, LRU_WIDTH)

    kk = lax.broadcasted_iota(jnp.int32, (POOL_HIST, SUBLANES, LANES), 0)
    jj = lax.broadcasted_iota(jnp.int32, (POOL_HIST, SUBLANES, LANES), 1)
    t_head = s * T + jj * lc + kk

    ext = jnp.concatenate([_history_rows(up[lc - POOL_HIST:], pool_hist[...]), up], axis=0)
    pool_hist[...] = up[lc - POOL_HIST:]
    s2 = ext[1:] + ext[:-1]
    s2r = s2[:, :, POOL_GROUP:]
    s4 = s2r[2:] + s2r[:-2]
    s4r = s4[:, :, POOL_GROUP:]
    s8 = s4r[4:] + s4r[:-4]
    s8r = s8[:, :, POOL_GROUP:]
    s16 = s8r[8:] + s8r[:-8]
    sums = (s2[15:, :, :POOL_GROUP], s4[13:, :, :POOL_GROUP], s8[9:, :, :POOL_GROUP], s16[1:])
    y_pool = []
    for g, w in enumerate(POOL_WINDOWS):
        cs = slice(g * POOL_GROUP, (g + 1) * POOL_GROUP)
        inv_head = 1.0 / jnp.minimum(t_head + 1, w).astype(_F32)
        inv_head = jnp.concatenate([inv_head] * (POOL_GROUP // LANES), axis=-1)
        mean = jnp.concatenate([sums[g][:POOL_HIST] * inv_head, sums[g][POOL_HIST:] * (1.0 / w)], axis=0)
        mixed = mean - up[:, :, cs]
        yg = _dot(mixed.reshape(T, POOL_GROUP).astype(_BF16), w_pool_ref[g])
        y_pool.append(((yg + b_pool_ref[:, cs]) * pool_scale_ref[:, cs]).astype(_BF16))
    z_pool = jnp.concatenate(y_pool, axis=1)

    n_hist = CONV_WIDTH - 1
    sub = lax.broadcasted_iota(jnp.int32, (SUBLANES, LRU_HEAD_DIM), 0)
    first = t_head[0:1] == 0
    gate_cols = 2 * LRU_HEAD_DIM
    out_cols = D_MODEL // LRU_HEADS
    y_top, z_lru = [], []
    for h in range(LRU_HEADS):
        hs = slice(h * LRU_HEAD_DIM, (h + 1) * LRU_HEAD_DIM)
        ul = u_lru[:, :, hs]
        ext2 = jnp.concatenate([_history_rows(ul[lc - n_hist:], conv_hist[:, :, hs]), ul], axis=0)
        conv_hist[:, :, hs] = ul[lc - n_hist:]
        xc = ext2[0:lc] * conv_w_ref[0:1, hs]
        for k in range(1, CONV_WIDTH):
            xc = xc + ext2[k:k + lc] * conv_w_ref[k:k + 1, hs]
        xc = xc + conv_b_ref[:, hs]

        pre = _dot(xc.reshape(T, LRU_HEAD_DIM).astype(_BF16), w_ax_ref[h])
        pre = pre.reshape(lc, SUBLANES, 2 * LRU_HEAD_DIM)
        r = jax.nn.sigmoid(pre[:, :, :LRU_HEAD_DIM] + b_a_ref[:, hs])
        i = jax.nn.sigmoid(pre[:, :, LRU_HEAD_DIM:] + b_x_ref[:, hs])

        if h % 2 == 0:
            c0 = POOL_WIDTH + LRU_WIDTH + h * LRU_HEAD_DIM
            u_gate = _dot(xb, _unpack_rows(w_in_ref[:, c0:c0 + gate_cols])).reshape(lc, SUBLANES, gate_cols)
        else:
            c0 = (h // 2) * out_cols
            y_top.append(_dot(z_pool, _unpack_rows(w_out_ref[:POOL_WIDTH // 2, c0:c0 + out_cols])))

        log_a = (-LRU_C * r) * jax.nn.softplus(-lam_ref[:, hs])
        a = jnp.exp(log_a)
        u = -jnp.tanh(log_a) * (a * a + 1.0)
        mult = jnp.where(u == 0.0, 0.0, u * lax.rsqrt(u))
        mult = jnp.concatenate([jnp.where(first, 1.0, mult[0:1]), mult[1:]], axis=0)
        bv = mult * (i * xc)

        hl, ac = [bv[0]], [a[0]]
        for k in range(1, lc):
            hl.append(a[k] * hl[-1] + bv[k])
            ac.append(a[k] * ac[-1])
        c_a, c_b = ac[-1], hl[-1]
        for d in (1, 2, 4):
            a_s = jnp.where(sub < d, 1.0, pltpu.roll(c_a, d, 0))
            b_s = jnp.where(sub < d, 0.0, pltpu.roll(c_b, d, 0))
            c_b = c_a * b_s + c_b
            c_a = c_a * a_s
        h_prev = h_carry[:, hs]
        h_end = c_b + c_a * h_prev
        h_in = jnp.where(sub == 0, h_prev, pltpu.roll(h_end, 1, 0))
        h_carry[:, hs] = jnp.broadcast_to(h_end[SUBLANES - 1:SUBLANES, :], (SUBLANES, LRU_HEAD_DIM))
        hseq = jnp.stack(hl, axis=0) + jnp.stack(ac, axis=0) * h_in
        gate = jax.nn.gelu(u_gate[:, :, (h % 2) * LRU_HEAD_DIM:(h % 2 + 1) * LRU_HEAD_DIM])
        z_lru.append((hseq * gate).reshape(T, LRU_HEAD_DIM).astype(_BF16))

    for q in range(LRU_HEADS // 2, LRU_HEADS):
        y_top.append(_dot(z_pool, _unpack_rows(w_out_ref[:POOL_WIDTH // 2, q * out_cols:(q + 1) * out_cols])))
    y = (jnp.concatenate(y_top, axis=1)
         + _dot(jnp.concatenate(z_lru, axis=1), _unpack_rows(w_out_ref[POOL_WIDTH // 2:, :])))
    resid_ref[slot] = DEEPNORM_ALPHA * x + y


def _const_spec(shape, single_buffer=False):
    nd = len(shape)
    kwargs = {"pipeline_mode": pl.Buffered(1)} if single_buffer else {}
    return pl.BlockSpec(shape, lambda *_: (0,) * nd, **kwargs)


def _mixer(x, w_in, conv_w, conv_b, w_ax, b_a, b_x, lam, w_pool, b_pool, pool_scale, w_out, g, b,
           cast_weights):
    B, S, D = x.shape
    T = MIXER_TILE
    n_seq = S // T
    n_steps = B * n_seq
    lc = T // SUBLANES
    assert T % SUBLANES == 0 and lc >= POOL_HIST and lc % (2 * SUBLANES) == 0 and n_steps >= 3
    assert T % (LN_CHUNKS * SUBLANES) == 0 and LN_CHUNKS % 2 == 0

    chunk_in, chunk_out = [], []
    for w in cast_weights:
        assert w.shape[0] % (2 * SUBLANES * n_steps) == 0
        rows_w = w.shape[0] // n_steps
        chunk_in.append(pl.BlockSpec((rows_w, w.shape[1]), lambda i: (jnp.minimum(i, n_steps - 1), 0)))
        chunk_out.append(pl.BlockSpec((rows_w // 2, w.shape[1]), lambda i: (jnp.minimum(i, n_steps - 1), 0)))
    return pl.pallas_call(
        functools.partial(_mixer_kernel, n_seq=n_seq),
        grid=(n_steps + 1,),
        in_specs=[
            pl.BlockSpec(memory_space=pl.ANY),
            _const_spec(w_in.shape, True),
            _const_spec(conv_w.shape), _const_spec(conv_b.shape),
            _const_spec(w_ax.shape, True),
            _const_spec(b_a.shape), _const_spec(b_x.shape), _const_spec(lam.shape),
            _const_spec(w_pool.shape, True),
            _const_spec(b_pool.shape), _const_spec(pool_scale.shape),
            _const_spec(w_out.shape, True),
            _const_spec(g.shape), _const_spec(b.shape),
        ] + chunk_in,
        out_specs=[pl.BlockSpec(memory_space=pl.ANY)] + chunk_out,
        out_shape=[jax.ShapeDtypeStruct(x.shape, _F32)]
        + [jax.ShapeDtypeStruct((w.shape[0] // 2, w.shape[1]), jnp.uint32) for w in cast_weights],
        scratch_shapes=[
            pltpu.VMEM((2, lc, SUBLANES, D), _F32),
            pltpu.VMEM((2, lc, SUBLANES, D), _F32),
            pltpu.SemaphoreType.DMA((2,)),
            pltpu.SemaphoreType.DMA((2,)),
            pltpu.VMEM((POOL_HIST, SUBLANES, POOL_WIDTH), _F32),
            pltpu.VMEM((CONV_WIDTH - 1, SUBLANES, LRU_WIDTH), _F32),
            pltpu.VMEM((SUBLANES, LRU_WIDTH), _F32),
            pltpu.VMEM((2, T, D), _F32),
        ],
        compiler_params=pltpu.CompilerParams(
            dimension_semantics=("arbitrary",), vmem_limit_bytes=MIXER_VMEM_LIMIT_BYTES),
        name="mixer",
    )(x, w_in, conv_w, conv_b, w_ax, b_a, b_x, lam, w_pool, b_pool, pool_scale, w_out, g, b, *cast_weights)


def _kv_kernel(mem_ref, wk_ref, wv_ref, wa_ref, wx_ref, wpool_ref, *rest):
    n_cast = (len(rest) - 5) // 2
    cast_in, k_ref, v_ref = rest[:n_cast], rest[n_cast], rest[n_cast + 1]
    cast_out = rest[n_cast + 2:2 * n_cast + 2]
    wax_ref, wpool_b_ref, mb_ref = rest[2 * n_cast + 2:]
    for src, dst in zip(cast_in, cast_out):
        dst[...] = _pack_rows(src[...])

    @pl.when(pl.program_id(0) == 0)
    def _():
        mb_ref[...] = mem_ref[...].astype(_BF16)
        wax_ref[...] = jnp.concatenate([wa_ref[...], wx_ref[...]], axis=-1).astype(_BF16)
        wpool_b_ref[...] = wpool_ref[...].astype(_BF16)

    k_ref[...] = _pack_rows(_dot(mb_ref[...], wk_ref[...].astype(_BF16)))
    v_ref[...] = _pack_rows(_dot(mb_ref[...], wv_ref[...].astype(_BF16)))


def _kv_proj(mem2d, w_k, w_v, w_a, w_x, w_pool, cast_weights):
    M, D = mem2d.shape
    tn = KV_TILE_N
    n_steps = D // tn
    wspec = pl.BlockSpec((D, tn), lambda j: (0, j))
    ospec = pl.BlockSpec((M // 2, tn), lambda j: (0, j))
    chunk_in, chunk_out = [], []
    for w in cast_weights:
        assert w.shape[0] % (2 * SUBLANES * n_steps) == 0
        rows_w = w.shape[0] // n_steps
        chunk_in.append(pl.BlockSpec((rows_w, w.shape[1]), lambda j: (j, 0)))
        chunk_out.append(pl.BlockSpec((rows_w // 2, w.shape[1]), lambda j: (j, 0)))
    wax_shape = w_a.shape[:-1] + (w_a.shape[-1] + w_x.shape[-1],)
    return pl.pallas_call(
        _kv_kernel,
        grid=(n_steps,),
        in_specs=[_const_spec(mem2d.shape, True), wspec, wspec,
                  _const_spec(w_a.shape), _const_spec(w_x.shape), _const_spec(w_pool.shape)] + chunk_in,
        out_specs=[ospec, ospec] + chunk_out + [_const_spec(wax_shape), _const_spec(w_pool.shape)],
        out_shape=[jax.ShapeDtypeStruct((M // 2, D), jnp.uint32)] * 2
        + [jax.ShapeDtypeStruct((w.shape[0] // 2, w.shape[1]), jnp.uint32) for w in cast_weights]
        + [jax.ShapeDtypeStruct(wax_shape, _BF16), jax.ShapeDtypeStruct(w_pool.shape, _BF16)],
        scratch_shapes=[pltpu.VMEM((M, D), _BF16)],
        compiler_params=pltpu.CompilerParams(
            dimension_semantics=("arbitrary",), vmem_limit_bytes=VMEM_LIMIT_BYTES),
        name="kv_proj",
    )(mem2d, w_k, w_v, w_a, w_x, w_pool, *cast_weights)


def _xattn_kernel(x_ref, xn_ref, k_ref, v_ref, wq_ref, wo_ref, g_ref, b_ref, *rest):
    n_cast = (len(rest) - 3) // 2
    cast_in, o_ref, cast_out = rest[:n_cast], rest[n_cast], rest[n_cast + 1:2 * n_cast + 1]
    resid_ref, q_ref = rest[2 * n_cast + 1:]
    i = pl.program_id(0)
    n_tiles = pl.num_programs(0) - 1
    slot = i % 2
    T = x_ref.shape[1]
    scale = XATTN_HEAD_DIM ** -0.5
    heads = [slice(h * XATTN_HEAD_DIM, (h + 1) * XATTN_HEAD_DIM) for h in range(XATTN_HEADS)]

    def query(xb, h):
        return _dot(xb, _unpack_rows(wq_ref[:, heads[h]])).astype(_BF16)

    @pl.when(i == 0)
    def _():
        resid_ref[1] = jnp.zeros(resid_ref.shape[1:], _F32)
        xb0 = x_ref[0].astype(_BF16)
        for h in range(XATTN_HEADS):
            q_ref[:, heads[h]] = query(xb0, h)

    def finish_previous():
        return _normalize_previous(resid_ref, slot, g_ref, b_ref, o_ref)

    @pl.when(i < n_tiles)
    def _():
        zeros = finish_previous()
        per_head = len(zeros) // XATTN_HEADS
        for src, dst in zip(cast_in, cast_out):
            dst[...] = _pack_rows(src[...])
        x = x_ref[0]
        xnb = xn_ref[0].astype(_BF16)

        def probs(h):
            sc = lax.dot_general(q_ref[:, heads[h]], _unpack_rows(k_ref[0, :, heads[h]]),
                                 (((1,), (1,)), ((), ())), preferred_element_type=_F32) * scale
            sc = sc + _tile_zeros(zeros[h * per_head:(h + 1) * per_head], T, N_MEM)
            e = jnp.exp(sc - jnp.max(sc, axis=-1, keepdims=True))
            return (e * (1.0 / jnp.sum(e, axis=-1, keepdims=True))).astype(_BF16)

        def attend(p, h):
            return _dot(p, _unpack_rows(v_ref[0, :, heads[h]])).astype(_BF16)

        o_parts = []
        for h in range(XATTN_HEADS):
            p = probs(h)
            q_ref[:, heads[h]] = query(xnb, h)
            o_parts.append(attend(p, h))
        y = _dot(jnp.concatenate(o_parts, axis=1), _unpack_rows(wo_ref[...]))
        resid_ref[slot] = DEEPNORM_ALPHA * x + y

    @pl.when(i == n_tiles)
    def _():
        finish_previous()


def _xattn(x, k, v, w_q, w_o, g, b, cast_weights):
    B, S, D = x.shape
    T = XATTN_TILE
    n_seq = S // T
    n_tiles = B * n_seq
    chunk_in, chunk_out = [], []
    for w in cast_weights:
        assert w.shape[0] % (2 * SUBLANES * n_tiles) == 0
        rows_w = w.shape[0] // n_tiles
        chunk_in.append(pl.BlockSpec((rows_w, w.shape[1]), lambda i: (jnp.minimum(i, n_tiles - 1), 0)))
        chunk_out.append(pl.BlockSpec((rows_w // 2, w.shape[1]), lambda i: (jnp.minimum(i, n_tiles - 1), 0)))
    assert T % (LN_CHUNKS * SUBLANES) == 0 and LN_CHUNKS % XATTN_HEADS == 0

    def tile(t):
        t = jnp.clip(t, 0, n_tiles - 1)
        return t // n_seq, t % n_seq

    kvspec = pl.BlockSpec((1, N_MEM // 2, D), lambda i: (tile(i)[0], 0, 0))
    return pl.pallas_call(
        _xattn_kernel,
        grid=(n_tiles + 1,),
        in_specs=[pl.BlockSpec((1, T, D), lambda i: (*tile(i), 0)),
                  pl.BlockSpec((1, T, D), lambda i: (*tile(i + 1), 0)),
                  kvspec, kvspec,
                  _const_spec(w_q.shape, True), _const_spec(w_o.shape, True),
                  _const_spec(g.shape), _const_spec(b.shape)] + chunk_in,
        out_specs=[pl.BlockSpec((1, T, D), lambda i: (*tile(i - 1), 0))] + chunk_out,
        out_shape=[jax.ShapeDtypeStruct(x.shape, _F32)]
        + [jax.ShapeDtypeStruct((w.shape[0] // 2, w.shape[1]), jnp.uint32) for w in cast_weights],
        scratch_shapes=[pltpu.VMEM((2, T, D), _F32), pltpu.VMEM((T, D), _BF16)],
        compiler_params=pltpu.CompilerParams(
            dimension_semantics=("arbitrary",), vmem_limit_bytes=VMEM_LIMIT_BYTES),
        name="xattn",
    )(x, x, k, v, w_q, w_o, g, b, *cast_weights)


def _mlp_kernel(x_hbm, w1_hbm, w2_hbm, g_ref, b_ref, o_hbm, xstage, xb_ref, acc_ref, oslice, w1buf, w2buf,
                x_sem, o_sem, w_sem, *, n_f, halves):
    i = pl.program_id(0)
    n_tiles = pl.num_programs(0) - 1
    slot = i % 2
    tm = xstage.shape[0]
    rows = tm // n_f
    tf = w1buf.shape[2]

    def aligned(v, m):
        return v if isinstance(v, int) else pl.multiple_of(v, m)

    def w_copies(j, buf):
        c0 = aligned(j * tf, tf)
        r0 = aligned(j * (tf // 2), tf // 2)
        return (pltpu.make_async_copy(w1_hbm.at[:, pl.ds(c0, tf)], w1buf.at[buf], w_sem.at[0, buf]),
                pltpu.make_async_copy(w2_hbm.at[pl.ds(r0, tf // 2), :], w2buf.at[buf], w_sem.at[1, buf]))

    def x_copy(t):
        return pltpu.make_async_copy(x_hbm.at[pl.ds(pl.multiple_of(t * tm, tm), tm), :], xstage, x_sem.at[0])

    def out_copy(step, buf):
        r0 = pl.multiple_of((step // n_f - 1) * tm + (step % n_f) * rows, rows)
        return pltpu.make_async_copy(oslice.at[buf], o_hbm.at[pl.ds(r0, rows), :], o_sem.at[buf])

    @pl.when(i == 0)
    def _():
        x_copy(0).start()
        for c in w_copies(0, 0):
            c.start()
        acc_ref[1] = jnp.zeros(acc_ref.shape[1:], _F32)
        x_copy(0).wait()
        xb_ref[0] = xstage[...].astype(_BF16)

    def finish_previous_rows(j, k):
        r0 = aligned(j * rows, rows)
        oslice[k] = _layer_norm(acc_ref[1 - slot, pl.ds(r0, rows), :], g_ref[...], b_ref[...])

    def accumulate(j, k, first):
        finish_previous_rows(j, k)
        w1 = _unpack_rows(w1buf[k])
        w2 = _unpack_rows(w2buf[k])
        half = tm // halves
        hs = [jnp.square(jnp.maximum(_dot(xb_ref[slot, r * half:(r + 1) * half, :], w1), 0.0)).astype(_BF16)
              for r in range(halves)]
        if not first:
            xrows = tm // X_CAST_CHUNKS
            r0 = pl.multiple_of(jnp.minimum(j - 1, X_CAST_CHUNKS - 1) * xrows, xrows)
            xb_ref[1 - slot, pl.ds(r0, xrows), :] = xstage[pl.ds(r0, xrows), :].astype(_BF16)
        for r in range(halves):
            rs = slice(r * half, (r + 1) * half)
            prior = DEEPNORM_ALPHA * xstage[rs, :] if first else acc_ref[slot, rs, :]
            acc_ref[slot, rs, :] = prior + _dot(hs[r], w2)

    def loop_step(j, first):
        q = i * n_f + j
        k = j % 2

        @pl.when(q >= n_f + 2)
        def _():
            out_copy(q - 2, k).wait()

        @pl.when(i < n_tiles)
        def _():
            for c in w_copies(j, k):
                c.wait()

        @pl.when(q + 1 < n_tiles * n_f)
        def _():
            for c in w_copies((j + 1) % n_f, 1 - k):
                c.start()

        @pl.when(i < n_tiles)
        def _():
            accumulate(j, k, first)

        @pl.when(i == n_tiles)
        def _():
            finish_previous_rows(j, k)

        @pl.when(q >= n_f)
        def _():
            out_copy(q, k).start()

    loop_step(0, first=True)

    @pl.when(i + 1 < n_tiles)
    def _():
        x_copy(i + 1).start()

    def body(j, carry):
        @pl.when((j == 1) & (i + 1 < n_tiles))
        def _():
            x_copy(i + 1).wait()

        loop_step(j, first=False)
        return carry

    lax.fori_loop(1, n_f, body, 0)

    @pl.when(i == n_tiles)
    def _():
        q_last = (n_tiles + 1) * n_f - 1
        out_copy(q_last - 1, 0).wait()
        out_copy(q_last, 1).wait()


def _mlp(x2d, w1, w2, g, b):
    M, D = x2d.shape
    F = w1.shape[1]
    tm, tf = MLP_TILE_M, MLP_TILE_F
    n_tiles, n_f = M // tm, F // tf
    assert tm % (n_f * SUBLANES) == 0 and tm % MLP_ROW_HALVES == 0 and n_f % 2 == 0
    assert n_f > X_CAST_CHUNKS and tm % (X_CAST_CHUNKS * 2 * SUBLANES) == 0

    return pl.pallas_call(
        functools.partial(_mlp_kernel, n_f=n_f, halves=MLP_ROW_HALVES),
        grid=(n_tiles + 1,),
        in_specs=[pl.BlockSpec(memory_space=pl.ANY),
                  pl.BlockSpec(memory_space=pl.ANY),
                  pl.BlockSpec(memory_space=pl.ANY),
                  _const_spec(g.shape), _const_spec(b.shape)],
        out_specs=pl.BlockSpec(memory_space=pl.ANY),
        out_shape=jax.ShapeDtypeStruct((M, D), _F32),
        scratch_shapes=[pltpu.VMEM((tm, D), _F32),
                        pltpu.VMEM((2, tm, D), _BF16),
                        pltpu.VMEM((2, tm, D), _F32),
                        pltpu.VMEM((2, tm // n_f, D), _F32),
                        pltpu.VMEM((2, D // 2, tf), jnp.uint32),
                        pltpu.VMEM((2, tf // 2, D), jnp.uint32),
                        pltpu.SemaphoreType.DMA((1,)),
                        pltpu.SemaphoreType.DMA((2,)),
                        pltpu.SemaphoreType.DMA((2, 2))],
        compiler_params=pltpu.CompilerParams(
            dimension_semantics=("arbitrary",),
            vmem_limit_bytes=VMEM_LIMIT_BYTES),
        name="mlp",
    )(x2d, w1, w2, g, b)


def kernel(x, mem, w_in, conv_w, conv_b, w_a, b_a, w_x, b_x, lru_lambda, w_pool, b_pool, pool_scale,
           w_out, ln1_g, ln1_b, w_q, w_k, w_v, w_o, ln2_g, ln2_b, w_ff1, w_ff2, ln3_g, ln3_b):
    B, S, D = x.shape
    row = lambda p: p.reshape(1, -1)
    for l in range(DEPTH):
        k, v, w_in_b, w_out_b, w_ax, w_pool_b = _kv_proj(
            mem.reshape(B * N_MEM, D), w_k[l], w_v[l], w_a[l], w_x[l], w_pool[l], (w_in[l], w_out[l]))
        x, wq_b, wo_b = _mixer(
            x, w_in_b, conv_w[l], row(conv_b[l]), w_ax,
            row(b_a[l]), row(b_x[l]), row(lru_lambda[l]),
            w_pool_b, row(b_pool[l]), row(pool_scale[l]),
            w_out_b, row(ln1_g[l]), row(ln1_b[l]),
            (w_q[l], w_o[l]))
        x, w1_b, w2_b = _xattn(x, k.reshape(B, N_MEM // 2, D), v.reshape(B, N_MEM // 2, D),
                               wq_b, wo_b, row(ln2_g[l]), row(ln2_b[l]), (w_ff1[l], w_ff2[l]))
        x = _mlp(x.reshape(B * S, D), w1_b, w2_b, row(ln3_g[l]), row(ln3_b[l])).reshape(B, S, D)
    return x
```
